```python
import jax, jax.numpy as jnp
from jax import lax
import numpy as np

D_MODEL = 2048
BATCH = 2
SEQ = 4096
DEPTH = 4
DEC_BATCH = 32
DEC_SEQ = 64
PAST_LEN = 4096

CHUNK = 64
N_AB_LAYERS = (DEPTH + 1) // 2
N_C_LAYERS = DEPTH // 2
H_A = 8
HD_A = 128
Q_BLOCK = 128
FORGET_BIAS = 2.0
H_B = 8
DK_B = 128
DV_B = 128
HG_BLOCK = 16
C_DIM = D_MODEL
C_GROUPS = 8
C_LEN = 2 * CHUNK
D_FF = 5632
EPS = 1e-6
MASK_VALUE = -1e30
TINY = 1e-30

W_A = H_A * HD_A
W_B_K = H_B * DK_B
W_B_V = H_B * DV_B
AB_IN = 3 * W_A + H_A + 2 * W_B_K + 2 * W_B_V
AB_OUT = W_A + W_B_V

kernel_name = 'fox_hgrn2_gmlp_macaron_stream_step'

F32 = jnp.float32


def _rmsnorm(x, g):
    xf = x.astype(F32)
    y = xf * lax.rsqrt(jnp.mean(xf * xf, axis=-1, keepdims=True) + EPS)
    return (y * g.astype(F32)).astype(x.dtype)


def _layernorm(x, g, b):
    xf = x.astype(F32)
    mu = jnp.mean(xf, axis=-1, keepdims=True)
    xc = xf - mu
    var = jnp.mean(xc * xc, axis=-1, keepdims=True)
    return (xc * lax.rsqrt(var + EPS) * g.astype(F32) + b.astype(F32)).astype(x.dtype)


def _swiglu(h, wg, wu, wd):
    return (jax.nn.silu(h @ wg) * (h @ wu)) @ wd


def _heads(z, n_heads):
    B, T, _ = z.shape
    return z.reshape(B, T, n_heads, -1).transpose(0, 2, 1, 3)


def _merge(o):
    B, n, T, d = o.shape
    return o.transpose(0, 2, 1, 3).reshape(B, T, n * d)


def _ab_inputs(h, w_in, b_f, lb):
    z = h @ w_in
    o1 = 3 * W_A + H_A
    cuts = [W_A, 2 * W_A, 3 * W_A, o1, o1 + W_B_K, o1 + 2 * W_B_K, o1 + 2 * W_B_K + W_B_V]
    qa, ka, va, fa, qb, fb, ib, gb = jnp.split(z, cuts, axis=-1)
    logf_a = jax.nn.log_sigmoid(fa.astype(F32) + b_f.astype(F32)).transpose(0, 2, 1)
    xf = fb.astype(F32)
    lb = lb.astype(F32)
    f_b = lb + (1.0 - lb) * jax.nn.sigmoid(xf)
    g_b = jnp.log(jnp.maximum(f_b, TINY))
    k_b = (1.0 - lb) * jax.nn.sigmoid(-xf)
    return (_heads(qa, H_A), _heads(ka, H_A), _heads(va, H_A), logf_a,
            _heads(jax.nn.silu(qb), H_B), _heads(k_b, H_B), _heads(g_b, H_B), _heads(ib, H_B), gb)


def _fox_prompt(q, k, v, logf):
    B, H, S, D = q.shape
    scale = D ** -0.5
    L = jnp.cumsum(logf, axis=-1)
    nb = S // Q_BLOCK
    qb = q.reshape(B, H, nb, Q_BLOCK, D).transpose(2, 0, 1, 3, 4)
    Lq = L.reshape(B, H, nb, Q_BLOCK).transpose(2, 0, 1, 3)
    kpos = jnp.arange(S)

    def block(args):
        qi, Li, i = args
        s = jnp.einsum('bhqd,bhkd->bhqk', qi, k).astype(F32) * scale + (Li[..., :, None] - L[..., None, :])
        qpos = i * Q_BLOCK + jnp.arange(Q_BLOCK)
        s = jnp.where(kpos[None, :] <= qpos[:, None], s, MASK_VALUE)
        p = jax.nn.softmax(s, axis=-1).astype(v.dtype)
        return jnp.einsum('bhqk,bhkd->bhqd', p, v)

    o = lax.map(block, (qb, Lq, jnp.arange(nb)))
    return o.transpose(1, 2, 0, 3, 4).reshape(B, H, S, D)


def _fox_sample(q, k, v, logf, ck, cv, clogf):
    B, H, T, D = q.shape
    P = ck.shape[2]
    scale = D ** -0.5
    Lc = jnp.cumsum(clogf.astype(F32), axis=-1)
    Ln = Lc[..., -1:] + jnp.cumsum(logf, axis=-1)
    s_c = jnp.einsum('bhqd,bhkd->bhqk', q, ck).astype(F32) * scale + (Ln[..., :, None] - Lc[..., None, :])
    s_n = jnp.einsum('bhqd,bhkd->bhqk', q, k).astype(F32) * scale + (Ln[..., :, None] - Ln[..., None, :])
    s_n = jnp.where(jnp.tril(jnp.ones((T, T), bool)), s_n, MASK_VALUE)
    p = jax.nn.softmax(jnp.concatenate([s_c, s_n], axis=-1), axis=-1).astype(v.dtype)
    return jnp.einsum('bhqk,bhkd->bhqd', p[..., :P], cv) + jnp.einsum('bhqk,bhkd->bhqd', p[..., P:], v)


def _hgrn2(q, k, g, i, s0):
    B, H, T, _ = q.shape
    q, k, g, i = (a.astype(F32) for a in (q, k, g, i))
    pad = (-T) % HG_BLOCK
    if pad:
        cfg = ((0, 0), (0, 0), (0, pad), (0, 0))
        q, k, g, i = (jnp.pad(a, cfg) for a in (q, k, g, i))
    n = (T + pad) // HG_BLOCK

    def blocks(a):
        return a.reshape(B, H, n, HG_BLOCK, a.shape[-1]).transpose(2, 0, 1, 3, 4)

    tri = jnp.tril(jnp.ones((HG_BLOCK, HG_BLOCK), bool))[:, :, None]

    def step(S, inp):
        qc, kc, gc, ic = inp
        G = jnp.cumsum(gc, axis=2)
        diff = G[:, :, :, None, :] - G[:, :, None, :, :]
        dec = jnp.where(tri, jnp.exp(jnp.where(tri, diff, 0.0)), 0.0)
        A = jnp.einsum('bhtd,bhsd,bhtsd->bhts', qc, kc, dec)
        o = jnp.einsum('bhts,bhsv->bhtv', A, ic) + jnp.einsum('bhtd,bhdv->bhtv', qc * jnp.exp(G), S)
        GL = G[:, :, -1, :]
        S = jnp.exp(GL)[..., None] * S + jnp.einsum('bhsd,bhsv->bhdv', kc * jnp.exp(GL[:, :, None, :] - G), ic)
        return S, o

    S, o = lax.scan(step, s0.astype(F32), tuple(blocks(a) for a in (q, k, g, i)))
    o = o.transpose(1, 2, 0, 3, 4).reshape(B, H, n * HG_BLOCK, DV_B)[:, :, :T]
    return o, S


def _ab_output(o_a, o_b, og, gnorm, w_out):
    ob = _rmsnorm(o_b.astype(og.dtype), gnorm)
    ob = _merge(ob) * jax.nn.silu(og)
    return jnp.concatenate([_merge(o_a).astype(og.dtype), ob], axis=-1) @ w_out


def _c_mix(h, w_in, ln_g, ln_b, w_s, b_s, w_out):
    z = jax.nn.gelu(h @ w_in)
    u, v = jnp.split(z, 2, axis=-1)
    v = _layernorm(v, ln_g, ln_b)
    B, T, _ = v.shape
    Lc = min(T, C_LEN)
    n = T // Lc
    ws = jnp.tril(w_s[:, :Lc, :Lc])
    vc = v.reshape(B, n, Lc, C_GROUPS, C_DIM // C_GROUPS)
    mixed = jnp.einsum('gts,bnsgc->bntgc', ws.astype(v.dtype), vc) + b_s[:, :Lc].T[None, None, :, :, None]
    y = (u * mixed.reshape(B, T, C_DIM).astype(u.dtype)) @ w_out
    return y, v


def setup_inputs(seed: int = 0) -> dict:
    key = jax.random.key(seed)
    ks = jax.random.split(key, 32)
    d = D_MODEL

    def nrm(k, shape, scale):
        return jax.random.normal(k, shape, F32) * scale

    return {
        'x_prompt': nrm(ks[0], (BATCH, SEQ, d), 1.0),
        'x_sample': nrm(ks[1], (DEC_BATCH, DEC_SEQ, d), 1.0),
        'cache_k': nrm(ks[2], (N_AB_LAYERS, DEC_BATCH, H_A, PAST_LEN, HD_A), 1.0),
        'cache_v': nrm(ks[3], (N_AB_LAYERS, DEC_BATCH, H_A, PAST_LEN, HD_A), 1.0),
        'cache_logf': jax.nn.log_sigmoid(FORGET_BIAS + nrm(ks[4], (N_AB_LAYERS, DEC_BATCH, H_A, PAST_LEN), 1.0)),
        'state_hgrn': nrm(ks[5], (N_AB_LAYERS, DEC_BATCH, H_B, DK_B, DV_B), 1.0),
        'norm_ffn1': 1.0 + nrm(ks[6], (DEPTH, d), 0.02),
        'ffn1_gate': nrm(ks[7], (DEPTH, d, D_FF), d ** -0.5),
        'ffn1_up': nrm(ks[8], (DEPTH, d, D_FF), d ** -0.5),
        'ffn1_down': nrm(ks[9], (DEPTH, D_FF, d), D_FF ** -0.5),
        'norm_mix': 1.0 + nrm(ks[10], (DEPTH, d), 0.02),
        'ab_w_in': nrm(ks[11], (N_AB_LAYERS, d, AB_IN), d ** -0.5),
        'ab_b_f': FORGET_BIAS + nrm(ks[12], (N_AB_LAYERS, H_A), 0.1),
        'hgrn_lb': 1.0 + nrm(ks[13], (N_AB_LAYERS, W_B_K), 0.1),
        'hgrn_gnorm': 1.0 + nrm(ks[14], (N_AB_LAYERS, DV_B), 0.02),
        'ab_w_out': nrm(ks[15], (N_AB_LAYERS, AB_OUT, d), AB_OUT ** -0.5),
        'c_w_in': nrm(ks[16], (N_C_LAYERS, d, 2 * C_DIM), d ** -0.5),
        'c_ln_g': 1.0 + nrm(ks[17], (N_C_LAYERS, C_DIM), 0.02),
        'c_ln_b': nrm(ks[18], (N_C_LAYERS, C_DIM), 0.02),
        'c_w_s': nrm(ks[19], (N_C_LAYERS, C_GROUPS, C_LEN, C_LEN), C_LEN ** -0.5),
        'c_b_s': 1.0 + nrm(ks[20], (N_C_LAYERS, C_GROUPS, C_LEN), 0.02),
        'c_w_out': nrm(ks[21], (N_C_LAYERS, C_DIM, d), C_DIM ** -0.5),
        'norm_ffn2': 1.0 + nrm(ks[22], (DEPTH, d), 0.02),
        'ffn2_gate': nrm(ks[23], (DEPTH, d, D_FF), d ** -0.5),
        'ffn2_up': nrm(ks[24], (DEPTH, d, D_FF), d ** -0.5),
        'ffn2_down': nrm(ks[25], (DEPTH, D_FF, d), D_FF ** -0.5),
        'norm_final': 1.0 + nrm(ks[26], (d,), 0.02),
    }


def reference(x_prompt, x_sample, cache_k, cache_v, cache_logf, state_hgrn,
              norm_ffn1, ffn1_gate, ffn1_up, ffn1_down, norm_mix,
              ab_w_in, ab_b_f, hgrn_lb, hgrn_gnorm, ab_w_out,
              c_w_in, c_ln_g, c_ln_b, c_w_s, c_b_s, c_w_out,
              norm_ffn2, ffn2_gate, ffn2_up, ffn2_down, norm_final):
    sm = jax.nn.softmax(hgrn_lb.astype(F32), axis=0)
    lower_bounds = jnp.cumsum(sm, axis=0) - sm[0:1]

    xp, xs = x_prompt, x_sample
    pk, pv, plf, ps = [], [], [], []
    sk, sv, slf, ss, scv = [], [], [], [], []
    for l in range(DEPTH):
        j = l // 2
        xp = xp + 0.5 * _swiglu(_rmsnorm(xp, norm_ffn1[l]), ffn1_gate[l], ffn1_up[l], ffn1_down[l])
        xs = xs + 0.5 * _swiglu(_rmsnorm(xs, norm_ffn1[l]), ffn1_gate[l], ffn1_up[l], ffn1_down[l])
        hp = _rmsnorm(xp, norm_mix[l])
        hs = _rmsnorm(xs, norm_mix[l])
        if l % 2 == 0:
            qa, ka, va, lfa, qb, kb, gb, ib, og = _ab_inputs(hp, ab_w_in[j], ab_b_f[j], lower_bounds[j])
            oa = _fox_prompt(qa, ka, va, lfa)
            s0 = jnp.zeros((xp.shape[0], H_B, DK_B, DV_B), F32)
            ob, s_fin = _hgrn2(qb, kb, gb, ib, s0)
            xp = xp + _ab_output(oa, ob, og, hgrn_gnorm[j], ab_w_out[j])
            pk.append(ka)
            pv.append(va)
            plf.append(lfa)
            ps.append(s_fin)
            qa, ka, va, lfa, qb, kb, gb, ib, og = _ab_inputs(hs, ab_w_in[j], ab_b_f[j], lower_bounds[j])
            oa = _fox_sample(qa, ka, va, lfa, cache_k[j], cache_v[j], cache_logf[j])
            ob, s_fin = _hgrn2(qb, kb, gb, ib, state_hgrn[j])
            xs = xs + _ab_output(oa, ob, og, hgrn_gnorm[j], ab_w_out[j])
            sk.append(ka)
            sv.append(va)
            slf.append(lfa)
            ss.append(s_fin)
        else:
            yc, _ = _c_mix(hp, c_w_in[j], c_ln_g[j], c_ln_b[j], c_w_s[j], c_b_s[j], c_w_out[j])
            xp = xp + yc
            yc, v_rows = _c_mix(hs, c_w_in[j], c_ln_g[j], c_ln_b[j], c_w_s[j], c_b_s[j], c_w_out[j])
            xs = xs + yc
            scv.append(v_rows)
        xp = xp + 0.5 * _swiglu(_rmsnorm(xp, norm_ffn2[l]), ffn2_gate[l], ffn2_up[l], ffn2_down[l])
        xs = xs + 0.5 * _swiglu(_rmsnorm(xs, norm_ffn2[l]), ffn2_gate[l], ffn2_up[l], ffn2_down[l])

    y_prompt = _rmsnorm(xp, norm_final)
    y_sample = _rmsnorm(xs, norm_final)
    new_k_prompt = jnp.stack(pk)
    new_v_prompt = jnp.stack(pv)
    new_logf_prompt = jnp.stack(plf)
    new_hgrn_prompt = jnp.stack(ps)
    new_k_sample = jnp.stack(sk)
    new_v_sample = jnp.stack(sv)
    new_logf_sample = jnp.stack(slf)
    new_hgrn_sample = jnp.stack(ss)
    new_cv_sample = jnp.stack(scv)
    return (y_prompt, y_sample, new_k_prompt, new_v_prompt, new_logf_prompt, new_hgrn_prompt,
            new_k_sample, new_v_sample, new_logf_sample, new_hgrn_sample, new_cv_sample)
```

```python
import functools

import jax
import jax.numpy as jnp
from jax import lax
from jax.experimental import pallas as pl
from jax.experimental.pallas import tpu as pltpu

F32 = jnp.float32
BF16 = jnp.bfloat16
EPS = 1e-6
TINY = 1e-30
MASK_VALUE = -1e30
HIGHEST = lax.Precision.HIGHEST

LANES = 128
MIB = 1 << 20
VMEM_LIMIT = 56 * MIB


def _cparams(*sem):
    return pltpu.CompilerParams(dimension_semantics=sem, vmem_limit_bytes=VMEM_LIMIT)


def _tile(n, target):
    if n <= target:
        return n
    for t in range(target, 7, -1):
        if n % t == 0 and t % 8 == 0:
            return t
    return n


def _dot(a, b):
    return jnp.dot(a, b, preferred_element_type=F32)


def _dot_nt(a, b):
    return lax.dot_general(a, b, (((1,), (1,)), ((), ())), preferred_element_type=F32)


def _dot_tn(a, b):
    return lax.dot_general(a, b, (((0,), (0,)), ((), ())), preferred_element_type=F32)


def _rms(x, g):
    return x * lax.rsqrt(jnp.mean(x * x, axis=-1, keepdims=True) + EPS) * g


def _ffn_body(x_ref, g_ref, wg_ref, wu_ref, wd_ref, o_ref, h_ref, acc_ref):
    j = pl.program_id(1)

    @pl.when(j == 0)
    def _():
        h_ref[...] = _rms(x_ref[...], g_ref[...]).astype(BF16)
        acc_ref[...] = jnp.zeros_like(acc_ref)

    h = h_ref[...]
    a = _dot(h, wg_ref[...])
    b = _dot(h, wu_ref[...])
    act = (a * jax.nn.sigmoid(a) * b).astype(BF16)
    acc_ref[...] += _dot(act, wd_ref[...])

    @pl.when(j == pl.num_programs(1) - 1)
    def _():
        o_ref[...] = x_ref[...] + 0.5 * acc_ref[...]


def _ffn(x, g, wg, wu, wd):
    T, D = x.shape
    F = wg.shape[1]
    tm = _tile(T, 512)
    tf = _tile(F, 512)
    return pl.pallas_call(
        _ffn_body,
        grid=(T // tm, F // tf),
        in_specs=[
            pl.BlockSpec((tm, D), lambda i, j: (i, 0)),
            pl.BlockSpec((1, D), lambda i, j: (0, 0)),
            pl.BlockSpec((D, tf), lambda i, j: (0, j)),
            pl.BlockSpec((D, tf), lambda i, j: (0, j)),
            pl.BlockSpec((tf, D), lambda i, j: (j, 0)),
        ],
        out_specs=pl.BlockSpec((tm, D), lambda i, j: (i, 0)),
        out_shape=jax.ShapeDtypeStruct((T, D), F32),
        scratch_shapes=[pltpu.VMEM((tm, D), BF16), pltpu.VMEM((tm, D), F32)],
        compiler_params=_cparams("parallel", "arbitrary"),
    )(x, g.reshape(1, D), wg, wu, wd)


def _norm_body(x_ref, g_ref, o_ref):
    o_ref[...] = _rms(x_ref[...], g_ref[...]).astype(o_ref.dtype)


def _norm(x, g, dtype):
    T, D = x.shape
    tm = _tile(T, 512)
    return pl.pallas_call(
        _norm_body,
        grid=(T // tm,),
        in_specs=[pl.BlockSpec((tm, D), lambda i: (i, 0)), pl.BlockSpec((1, D), lambda i: (0, 0))],
        out_specs=pl.BlockSpec((tm, D), lambda i: (i, 0)),
        out_shape=jax.ShapeDtypeStruct((T, D), dtype),
        compiler_params=_cparams("parallel"),
    )(x, g.reshape(1, D))


def _proj_body(h_ref, w_ref, *refs, epilogue, n_aux):
    aux = [r[...] for r in refs[:n_aux]]
    outs = refs[n_aux:]
    z = _dot(h_ref[...], w_ref[...])
    vals = epilogue(z, *aux)
    for o_ref, v in zip(outs, vals):
        o_ref[...] = v.astype(o_ref.dtype)


def _proj(h, w, aux, epilogue, out_dtypes):
    T, K = h.shape
    N = w.shape[1]
    tm = _tile(T, 1024)
    tn = _tile(N, 512)
    aux_specs = [pl.BlockSpec((a.shape[0], tn), lambda i, j: (0, j)) for a in aux]
    return pl.pallas_call(
        functools.partial(_proj_body, epilogue=epilogue, n_aux=len(aux)),
        grid=(T // tm, N // tn),
        in_specs=[pl.BlockSpec((tm, K), lambda i, j: (i, 0)), pl.BlockSpec((K, tn), lambda i, j: (0, j))] + aux_specs,
        out_specs=[pl.BlockSpec((tm, tn), lambda i, j: (i, j)) for _ in out_dtypes],
        out_shape=[jax.ShapeDtypeStruct((T, N), d) for d in out_dtypes],
        compiler_params=_cparams("parallel", "parallel"),
    )(h, w, *aux)


def _heads_body(h_ref, w_ref, *o_refs, nb, ts, hb):
    z = _dot(h_ref[...], w_ref[...])
    for hh in range(hb):
        zz = z[:, hh * LANES:(hh + 1) * LANES].reshape(nb, ts, LANES)
        for o_ref in o_refs:
            o_ref[:, hh] = zz.astype(o_ref.dtype)


def _proj_heads(h, w, row0, B, S, out_dtypes):
    K = h.shape[1]
    N = w.shape[1]
    H = N // LANES
    if S >= 512:
        ts, nb = _tile(S, 512), 1
    else:
        nb = _tile(B, max(1, 512 // S))
        ts = S
    tm = nb * ts
    tn = _tile(N, 512)
    hb = tn // LANES
    spt = S // ts
    assert row0 % tm == 0
    i0 = row0 // tm
    if nb == 1:
        omap = lambda i, j: (i // spt, j, i % spt, 0)
    else:
        omap = lambda i, j: (i, j, 0, 0)
    return pl.pallas_call(
        functools.partial(_heads_body, nb=nb, ts=ts, hb=hb),
        grid=(B * S // tm, N // tn),
        in_specs=[pl.BlockSpec((tm, K), lambda i, j: (i + i0, 0)), pl.BlockSpec((K, tn), lambda i, j: (0, j))],
        out_specs=[pl.BlockSpec((nb, hb, ts, LANES), omap) for _ in out_dtypes],
        out_shape=[jax.ShapeDtypeStruct((B, H, S, LANES), d) for d in out_dtypes],
        compiler_params=_cparams("parallel", "parallel"),
    )(h, w)


def _mm_res_body(*refs, n):
    x_ref, o_ref = refs[2 * n], refs[2 * n + 1]
    z = _dot(refs[0][...], refs[n][...])
    for a_ref, w_ref in zip(refs[1:n], refs[n + 1:2 * n]):
        z += _dot(a_ref[...], w_ref[...])
    o_ref[...] = x_ref[...] + z


def _mm_res(lhs, ws, x):
    T, N = x.shape
    n = len(lhs)
    tm = _tile(T, 1024)
    tn = _tile(N, 512)
    return pl.pallas_call(
        functools.partial(_mm_res_body, n=n),
        grid=(T // tm, N // tn),
        in_specs=([pl.BlockSpec((tm, a.shape[1]), lambda i, j: (i, 0)) for a in lhs]
                  + [pl.BlockSpec((w.shape[0], tn), lambda i, j: (0, j)) for w in ws]
                  + [pl.BlockSpec((tm, tn), lambda i, j: (i, j))]),
        out_specs=pl.BlockSpec((tm, tn), lambda i, j: (i, j)),
        out_shape=jax.ShapeDtypeStruct((T, N), F32),
        compiler_params=_cparams("parallel", "parallel"),
    )(*lhs, *ws, x)


def _cumsum_body(lf_ref, base_ref, ltok_ref, lrow_ref, lfrow_ref, carry_ref, *, ts, seg, tiles_per_seq, n_heads):
    lf = lf_ref[...]
    r = lax.broadcasted_iota(jnp.int32, (ts, ts), 0)
    c = lax.broadcasted_iota(jnp.int32, (ts, ts), 1)
    tri = jnp.where(r >= c, 1.0, 0.0).astype(F32)
    if seg < ts:
        tri = jnp.where(r // seg == c // seg, tri, 0.0)
    L = jnp.dot(tri, lf, precision=HIGHEST, preferred_element_type=F32)
    if tiles_per_seq:
        @pl.when(pl.program_id(0) % tiles_per_seq == 0)
        def _():
            carry_ref[...] = jnp.zeros_like(carry_ref)
        L = L + carry_ref[...]
        carry_ref[...] = L[ts - 1:ts, :]
    else:
        L = L + base_ref[...]
    ltok_ref[...] = L
    Lt = L.T[:n_heads]
    lft = lf.T[:n_heads]
    for s in range(ts // seg):
        lrow_ref[s] = Lt[:, s * seg:(s + 1) * seg]
        lfrow_ref[s] = lft[:, s * seg:(s + 1) * seg]


def _cumsum(lf_tok, base, row0, B, S, n_heads):
    if S >= LANES:
        ts, seg, tps = _tile(S, 512), None, None
        seg = ts
        tps = S // ts
        nseg = 1
    else:
        ts, seg, tps = LANES, S, 0
        nseg = ts // seg
    assert row0 % ts == 0 and (B * S) % ts == 0
    i0 = row0 // ts
    if tps:
        omap = lambda i: (i // tps, 0, i % tps)
    else:
        omap = lambda i: (i, 0, 0)
    oblock = (nseg, n_heads, seg)
    return pl.pallas_call(
        functools.partial(_cumsum_body, ts=ts, seg=seg, tiles_per_seq=tps, n_heads=n_heads),
        grid=(B * S // ts,),
        in_specs=[pl.BlockSpec((ts, LANES), lambda i: (i + i0, 0)), pl.BlockSpec((ts, LANES), lambda i: (i, 0))],
        out_specs=[pl.BlockSpec((ts, LANES), lambda i: (i, 0)), pl.BlockSpec(oblock, omap), pl.BlockSpec(oblock, omap)],
        out_shape=[jax.ShapeDtypeStruct((B * S, LANES), F32), jax.ShapeDtypeStruct((B, n_heads, S), F32),
                   jax.ShapeDtypeStruct((B, n_heads, S), F32)],
        scratch_shapes=[pltpu.VMEM((1, LANES), F32)],
        compiler_params=_cparams("arbitrary"),
    )(lf_tok, base)


def _row_cumsum_body(x_ref, o_ref):
    R, P = x_ref.shape
    r = lax.broadcasted_iota(jnp.int32, (LANES, LANES), 0)
    c = lax.broadcasted_iota(jnp.int32, (LANES, LANES), 1)
    tri = jnp.where(r <= c, 1.0, 0.0).astype(F32)
    carry = jnp.zeros((R, 1), F32)
    for s in range(P // LANES):
        y = jnp.dot(x_ref[:, s * LANES:(s + 1) * LANES], tri, precision=HIGHEST, preferred_element_type=F32) + carry
        o_ref[:, s * LANES:(s + 1) * LANES] = y
        carry = y[:, LANES - 1:LANES]


def _row_cumsum(x):
    R, P = x.shape
    tr = _tile(R, 256)
    return pl.pallas_call(
        _row_cumsum_body,
        grid=(R // tr,),
        in_specs=[pl.BlockSpec((tr, P), lambda i: (i, 0))],
        out_specs=pl.BlockSpec((tr, P), lambda i: (i, 0)),
        out_shape=jax.ShapeDtypeStruct((R, P), F32),
        compiler_params=_cparams("parallel"),
    )(x)


def _head_column(l_tok, h):
    lane = lax.broadcasted_iota(jnp.int32, l_tok.shape, 1)
    return jnp.sum(jnp.where(lane == h, l_tok, 0.0), axis=-1, keepdims=True)


def _fox_prompt_body(q_ref, k_ref, v_ref, lq_ref, lk_ref, o_ref, m_ref, l_ref, acc_ref, lqc_ref, *, scale, tq):
    h, qi, ki = pl.program_id(1), pl.program_id(2), pl.program_id(3)

    @pl.when(ki == 0)
    def _():
        m_ref[...] = jnp.full_like(m_ref, MASK_VALUE)
        l_ref[...] = jnp.zeros_like(l_ref)
        acc_ref[...] = jnp.zeros_like(acc_ref)
        lqc_ref[...] = _head_column(lq_ref[...], h)

    def step(diagonal):
        s = _dot_nt(q_ref[0, 0], k_ref[0, 0]) * scale + (lqc_ref[...] - lk_ref[0, 0])
        if diagonal:
            r = lax.broadcasted_iota(jnp.int32, s.shape, 0)
            c = lax.broadcasted_iota(jnp.int32, s.shape, 1)
            s = jnp.where(c <= r, s, MASK_VALUE)
        m_prev = m_ref[...]
        m_new = jnp.maximum(m_prev, jnp.max(s, axis=-1, keepdims=True))
        alpha = jnp.exp(m_prev - m_new)
        p = jnp.exp(s - m_new)
        l_ref[...] = alpha * l_ref[...] + jnp.sum(p, axis=-1, keepdims=True)
        acc_ref[...] = alpha * acc_ref[...] + _dot(p.astype(BF16), v_ref[0, 0])
        m_ref[...] = m_new

    @pl.when(ki < qi)
    def _():
        step(False)

    @pl.when(ki == qi)
    def _():
        step(True)
        o_ref[...] = (acc_ref[...] / l_ref[...]).astype(o_ref.dtype)


def _fox_prompt(q, k, v, l_tok, l_row):
    B, H, S, D = q.shape
    tq = _tile(S, 512)
    nq = S // tq
    kv_map = lambda b, h, qi, ki: (b, h, jnp.minimum(ki, qi), 0)
    return pl.pallas_call(
        functools.partial(_fox_prompt_body, scale=D ** -0.5, tq=tq),
        grid=(B, H, nq, nq),
        in_specs=[
            pl.BlockSpec((1, 1, tq, D), lambda b, h, qi, ki: (b, h, qi, 0)),
            pl.BlockSpec((1, 1, tq, D), kv_map),
            pl.BlockSpec((1, 1, tq, D), kv_map),
            pl.BlockSpec((tq, LANES), lambda b, h, qi, ki: (b * nq + qi, 0)),
            pl.BlockSpec((1, 1, 1, tq), lambda b, h, qi, ki: (b, h, 0, jnp.minimum(ki, qi))),
        ],
        out_specs=pl.BlockSpec((tq, D), lambda b, h, qi, ki: (b * nq + qi, h)),
        out_shape=jax.ShapeDtypeStruct((B * S, H * D), BF16),
        scratch_shapes=[pltpu.VMEM((tq, 1), F32), pltpu.VMEM((tq, 1), F32), pltpu.VMEM((tq, D), F32),
                        pltpu.VMEM((tq, 1), F32)],
        compiler_params=_cparams("parallel", "parallel", "parallel", "arbitrary"),
    )(q, k, v, l_tok, l_row)


def _fox_sample_body(q_ref, ck_ref, cv_ref, kn_ref, vn_ref, lq_ref, lc_ref, ln_ref, o_ref, *, scale):
    h = pl.program_id(1)
    q = q_ref[0, 0]
    lq = _head_column(lq_ref[...], h)
    s_c = _dot_nt(q, ck_ref[0, 0].astype(BF16)) * scale + (lq - lc_ref[0, 0])
    s_n = _dot_nt(q, kn_ref[0, 0]) * scale + (lq - ln_ref[0, 0])
    r = lax.broadcasted_iota(jnp.int32, s_n.shape, 0)
    c = lax.broadcasted_iota(jnp.int32, s_n.shape, 1)
    s_n = jnp.where(c <= r, s_n, MASK_VALUE)
    m = jnp.maximum(jnp.max(s_c, axis=-1, keepdims=True), jnp.max(s_n, axis=-1, keepdims=True))
    p_c = jnp.exp(s_c - m)
    p_n = jnp.exp(s_n - m)
    l = jnp.sum(p_c, axis=-1, keepdims=True) + jnp.sum(p_n, axis=-1, keepdims=True)
    o = _dot(p_c.astype(BF16), cv_ref[0, 0].astype(BF16)) + _dot(p_n.astype(BF16), vn_ref[0, 0])
    o_ref[...] = (o / l).astype(o_ref.dtype)


def _fox_sample(q, ck, cv, kn, vn, ln_tok, lc_row, ln_row):
    B, H, T, D = q.shape
    P = ck.shape[2]
    new = pl.BlockSpec((1, 1, T, D), lambda b, h: (b, h, 0, 0))
    old = pl.BlockSpec((1, 1, P, D), lambda b, h: (b, h, 0, 0))
    return pl.pallas_call(
        functools.partial(_fox_sample_body, scale=D ** -0.5),
        grid=(B, H),
        in_specs=[new, old, old, new, new,
                  pl.BlockSpec((T, LANES), lambda b, h: (b, 0)),
                  pl.BlockSpec((1, 1, 1, P), lambda b, h: (b, h, 0, 0)),
                  pl.BlockSpec((1, 1, 1, T), lambda b, h: (b, h, 0, 0))],
        out_specs=pl.BlockSpec((T, D), lambda b, h: (b, h)),
        out_shape=jax.ShapeDtypeStruct((B * T, H * D), BF16),
        compiler_params=_cparams("parallel", "parallel"),
    )(q, ck, cv, kn, vn, ln_tok, lc_row, ln_row)


def _hgrn_body(*refs, C, has_init):
    if has_init:
        q_ref, k_ref, g_ref, i_ref, gate_ref, gn_ref, s0_ref, o_ref, sf_ref, st_ref = refs
    else:
        q_ref, k_ref, g_ref, i_ref, gate_ref, gn_ref, o_ref, sf_ref, st_ref = refs
        s0_ref = None
    c = pl.program_id(2)

    @pl.when(c == 0)
    def _():
        if has_init:
            st_ref[...] = s0_ref[0, 0].T
        else:
            st_ref[...] = jnp.zeros_like(st_ref)

    q, k, g, v = q_ref[...], k_ref[...], g_ref[...], i_ref[...]
    row = lax.broadcasted_iota(jnp.int32, (C, C), 0)
    col = lax.broadcasted_iota(jnp.int32, (C, C), 1)
    tri = jnp.where(row >= col, 1.0, 0.0).astype(F32)
    G = jnp.dot(tri, g, precision=HIGHEST, preferred_element_type=F32)
    st = st_ref[...]
    o = _dot_nt((q * jnp.exp(G)).astype(BF16), st.astype(BF16))

    A = jnp.where(row == col, _dot_nt(q.astype(BF16), k.astype(BF16)), 0.0)
    trow = lax.broadcasted_iota(jnp.int32, (C, LANES), 0)
    split = jnp.where(row > col, row ^ col, 0)
    yq, yk = G - g, G
    w, log_w = 1, 0
    while w < C:
        qt = (q * jnp.exp(G - yq)).astype(BF16)
        kt = (k * jnp.exp(yk - G)).astype(BF16)
        A = A + jnp.where((split >> log_w) == 1, _dot_nt(qt, kt), 0.0)
        upper = (trow & w) != 0
        yq = jnp.where(upper, pltpu.roll(yq, w, 0), yq)
        yk = jnp.where(upper, yk, pltpu.roll(yk, C - w, 0))
        w, log_w = 2 * w, log_w + 1
    o = o + _dot(A.astype(BF16), v)

    g_last = G[C - 1:C, :]
    ks = (k * jnp.exp(g_last - G)).astype(BF16)
    st_new = st * jnp.exp(g_last) + _dot_tn(v, ks)
    st_ref[...] = st_new

    o_ref[...] = (_rms(o, gn_ref[...]) * gate_ref[...]).astype(o_ref.dtype)

    @pl.when(c == pl.num_programs(2) - 1)
    def _():
        sf_ref[0, 0] = st_new.T


def _hgrn(q, k, g, i, gate, gnorm, s0, row0, B, S, H):
    C = min(S, LANES)
    nc = S // C
    assert row0 % C == 0
    r0 = row0 // C
    tok = pl.BlockSpec((C, LANES), lambda b, h, c: (r0 + b * nc + c, h))
    state = pl.BlockSpec((1, 1, LANES, LANES), lambda b, h, c: (b, h, 0, 0))
    in_specs = [tok, tok, tok, tok, tok, pl.BlockSpec((1, LANES), lambda b, h, c: (0, 0))]
    args = [q, k, g, i, gate, gnorm.reshape(1, LANES)]
    if s0 is not None:
        in_specs.append(state)
        args.append(s0)
    return pl.pallas_call(
        functools.partial(_hgrn_body, C=C, has_init=s0 is not None),
        grid=(B, H, nc),
        in_specs=in_specs,
        out_specs=[pl.BlockSpec((C, LANES), lambda b, h, c: (b * nc + c, h)), state],
        out_shape=[jax.ShapeDtypeStruct((B * S, H * LANES), BF16), jax.ShapeDtypeStruct((B, H, LANES, LANES), F32)],
        scratch_shapes=[pltpu.VMEM((LANES, LANES), F32)],
        compiler_params=_cparams("parallel", "parallel", "arbitrary"),
    )(*args)


def _cmix_body(u_ref, v_ref, lng_ref, lnb_ref, ws_ref, bs_ref, y_ref, *vout, groups):
    v = v_ref[...]
    mu = jnp.mean(v, axis=-1, keepdims=True)
    xc = v - mu
    var = jnp.mean(xc * xc, axis=-1, keepdims=True)
    vn = xc * lax.rsqrt(var + EPS) * lng_ref[...] + lnb_ref[...]
    if vout:
        vout[0][...] = vn
    vb = vn.astype(BF16)
    Lc, Cd = v.shape
    cg = Cd // groups
    r = lax.broadcasted_iota(jnp.int32, (Lc, Lc), 0)
    c = lax.broadcasted_iota(jnp.int32, (Lc, Lc), 1)
    for gi in range(groups):
        sl = slice(gi * cg, (gi + 1) * cg)
        w = jnp.where(c <= r, ws_ref[gi], 0.0).astype(BF16)
        mixed = _dot(w, vb[:, sl]) + bs_ref[:, sl]
        y_ref[:, sl] = (u_ref[:, sl] * mixed).astype(y_ref.dtype)


def _cmix(u, v, ln_g, ln_b, ws, bs, row0, n_rows, Lc, want_v):
    Cd = u.shape[1]
    G = ws.shape[0]
    assert row0 % Lc == 0
    r0 = row0 // Lc
    tok_in = pl.BlockSpec((Lc, Cd), lambda i: (i + r0, 0))
    tok_out = pl.BlockSpec((Lc, Cd), lambda i: (i, 0))
    vec = pl.BlockSpec((1, Cd), lambda i: (0, 0))
    out_specs = [tok_out]
    out_shape = [jax.ShapeDtypeStruct((n_rows, Cd), BF16)]
    if want_v:
        out_specs.append(tok_out)
        out_shape.append(jax.ShapeDtypeStruct((n_rows, Cd), F32))
    return pl.pallas_call(
        functools.partial(_cmix_body, groups=G),
        grid=(n_rows // Lc,),
        in_specs=[tok_in, tok_in, vec, vec, pl.BlockSpec((G, Lc, Lc), lambda i: (0, 0, 0)),
                  pl.BlockSpec((Lc, Cd), lambda i: (0, 0))],
        out_specs=out_specs,
        out_shape=out_shape,
        compiler_params=_cparams("parallel"),
    )(u, v, ln_g.reshape(1, Cd), ln_b.reshape(1, Cd), ws, bs)


def _epi_id(z):
    return (z,)


def _epi_silu(z):
    return (z * jax.nn.sigmoid(z),)


def _epi_gelu(z):
    return (jax.nn.gelu(z),)


def _epi_logsig(z, b):
    return (jax.nn.log_sigmoid(z + b),)


def _epi_hgrn_forget(z, lb_all, *, layer):
    rows = [lb_all[n:n + 1] for n in range(lb_all.shape[0])]
    top = functools.reduce(jnp.maximum, rows)
    e = [jnp.exp(r - top) for r in rows]
    total = functools.reduce(jnp.add, e)
    sm = [a / total for a in e]
    lb = functools.reduce(jnp.add, sm[:layer + 1]) - sm[0]
    f = lb + (1.0 - lb) * jax.nn.sigmoid(z)
    g = jnp.log(jnp.maximum(f, TINY))
    k = (1.0 - lb) * jax.nn.sigmoid(-z)
    return g, k


def kernel(x_prompt, x_sample, cache_k, cache_v, cache_logf, state_hgrn, norm_ffn1, ffn1_gate, ffn1_up, ffn1_down, norm_mix, ab_w_in, ab_b_f, hgrn_lb, hgrn_gnorm, ab_w_out, c_w_in, c_ln_g, c_ln_b, c_w_s, c_b_s, c_w_out, norm_ffn2, ffn2_gate, ffn2_up, ffn2_down, norm_final):
    B, S, D = x_prompt.shape
    DB, DT, _ = x_sample.shape
    depth = norm_ffn1.shape[0]
    HA, P, HD = cache_k.shape[2], cache_k.shape[3], cache_k.shape[4]
    HB, DK, DV = state_hgrn.shape[2], state_hgrn.shape[3], state_hgrn.shape[4]
    assert HD == LANES and DK == LANES and DV == LANES and HA <= 8
    WA, WBK, WBV = HA * HD, HB * DK, HB * DV
    G, CL = c_w_s.shape[1], c_w_s.shape[2]
    CD = c_w_out.shape[1]
    Tp, Ts = B * S, DB * DT

    x = jnp.concatenate([x_prompt.reshape(Tp, D), x_sample.reshape(Ts, D)], axis=0)
    bf = lambda a: a.astype(BF16)

    pk, pv, plf, ps, sk, sv, slf, ss, scv = ([] for _ in range(9))
    for l in range(depth):
        j = l // 2
        x = _ffn(x, norm_ffn1[l], bf(ffn1_gate[l]), bf(ffn1_up[l]), bf(ffn1_down[l]))
        h = _norm(x, norm_mix[l], BF16)
        if l % 2 == 0:
            w_in = ab_w_in[j]
            o1 = 3 * WA + HA
            w_q, w_k, w_v = (bf(w_in[:, n * WA:(n + 1) * WA]) for n in range(3))
            w_f = bf(jnp.pad(w_in[:, 3 * WA:o1], ((0, 0), (0, LANES - HA))))
            w_qb = bf(w_in[:, o1:o1 + WBK])
            w_fb = bf(w_in[:, o1 + WBK:o1 + 2 * WBK])
            w_ib = bf(w_in[:, o1 + 2 * WBK:o1 + 2 * WBK + WBV])
            w_gb = bf(w_in[:, o1 + 2 * WBK + WBV:])
            b_f = jnp.pad(ab_b_f[j], (0, LANES - HA)).reshape(1, LANES)

            (lf_tok,) = _proj(h, w_f, [b_f], _epi_logsig, [F32])
            lp_tok, lp_row, lfp_row = _cumsum(lf_tok, jnp.zeros((Tp, LANES), F32), 0, B, S, HA)
            lc_row = _row_cumsum(cache_logf[j].reshape(DB * HA, P)).reshape(DB, HA, P)
            base = jnp.pad(lc_row[:, :, P - 1], ((0, 0), (0, LANES - HA)))
            base = jnp.broadcast_to(base[:, None, :], (DB, DT, LANES)).reshape(Ts, LANES)
            ls_tok, ls_row, lfs_row = _cumsum(lf_tok, base, Tp, DB, DT, HA)

            (qp,) = _proj_heads(h, w_q, 0, B, S, [BF16])
            kp, kpb = _proj_heads(h, w_k, 0, B, S, [F32, BF16])
            vp, vpb = _proj_heads(h, w_v, 0, B, S, [F32, BF16])
            (qs,) = _proj_heads(h, w_q, Tp, DB, DT, [BF16])
            ks, ksb = _proj_heads(h, w_k, Tp, DB, DT, [F32, BF16])
            vs, vsb = _proj_heads(h, w_v, Tp, DB, DT, [F32, BF16])
            oa_p = _fox_prompt(qp, kpb, vpb, lp_tok, lp_row.reshape(B, HA, 1, S))
            oa_s = _fox_sample(qs, cache_k[j], cache_v[j], ksb, vsb, ls_tok,
                               lc_row.reshape(DB, HA, 1, P), ls_row.reshape(DB, HA, 1, DT))

            (qb,) = _proj(h, w_qb, [], _epi_silu, [F32])
            gb, kb = _proj(h, w_fb, [hgrn_lb], functools.partial(_epi_hgrn_forget, layer=j), [F32, F32])
            (ib,) = _proj(h, w_ib, [], _epi_id, [BF16])
            (gate,) = _proj(h, w_gb, [], _epi_silu, [F32])
            ob_p, st_p = _hgrn(qb, kb, gb, ib, gate, hgrn_gnorm[j], None, 0, B, S, HB)
            ob_s, st_s = _hgrn(qb, kb, gb, ib, gate, hgrn_gnorm[j], state_hgrn[j], Tp, DB, DT, HB)

            w_out = bf(ab_w_out[j])
            x = _mm_res([jnp.concatenate([oa_p, oa_s], axis=0), jnp.concatenate([ob_p, ob_s], axis=0)],
                        [w_out[:WA], w_out[WA:]], x)
            pk.append(kp)
            pv.append(vp)
            plf.append(lfp_row)
            ps.append(st_p)
            sk.append(ks)
            sv.append(vs)
            slf.append(lfs_row)
            ss.append(st_s)
        else:
            w_in = bf(c_w_in[j])
            (u,) = _proj(h, w_in[:, :CD], [], _epi_gelu, [F32])
            (v,) = _proj(h, w_in[:, CD:], [], _epi_gelu, [F32])
            outs = []
            for row0, n_rows, T in ((0, Tp, S), (Tp, Ts, DT)):
                Lc = min(T, CL)
                ws = c_w_s[j][:, :Lc, :Lc]
                bs = jnp.repeat(c_b_s[j][:, :Lc].T, CD // G, axis=1)
                outs.append(_cmix(u, v, c_ln_g[j], c_ln_b[j], ws, bs, row0, n_rows, Lc, row0 > 0))
            (y_p,), (y_s, v_rows) = outs
            x = _mm_res([jnp.concatenate([y_p, y_s], axis=0)], [bf(c_w_out[j])], x)
            scv.append(v_rows.reshape(DB, DT, CD))
        x = _ffn(x, norm_ffn2[l], bf(ffn2_gate[l]), bf(ffn2_up[l]), bf(ffn2_down[l]))

    y = _norm(x, norm_final, F32)
    return (y[:Tp].reshape(B, S, D), y[Tp:].reshape(DB, DT, D),
            jnp.stack(pk), jnp.stack(pv), jnp.stack(plf), jnp.stack(ps),
            jnp.stack(sk), jnp.stack(sv), jnp.stack(slf), jnp.stack(ss), jnp.stack(scv))
```

```python
import functools
import math

import jax
import jax.numpy as jnp
from jax import lax
from jax.experimental import pallas as pl
from jax.experimental.pallas import tpu as pltpu

F32 = jnp.float32
BF16 = jnp.bfloat16
EPS = 1e-6
TINY = 1e-30
MASK_VALUE = -1e30
LOG2E = math.log2(math.e)
HIGHEST = lax.Precision.HIGHEST

LANES = 128
MIB = 1 << 20
VMEM_LIMIT = 56 * MIB


def _tile(n, target):
    if n <= target:
        return n
    for t in range(target, 7, -1):
        if n % t == 0 and t % 8 == 0:
            return t
    return n


def _call(body, grid, in_specs, args, out_specs, out_shapes, sem, dsts=None, scratch=()):
    in_specs, args = list(in_specs), list(args)
    n_in = len(args)
    aliases = {}
    for k, d in enumerate(dsts or ()):
        if d is not None:
            aliases[len(args)] = k
            in_specs.append(pl.BlockSpec(memory_space=pl.ANY))
            args.append(d)
    n_all = len(args)

    def wrapped(*refs):
        body(*refs[:n_in], *refs[n_all:])

    return pl.pallas_call(
        wrapped, grid=grid, in_specs=in_specs, out_specs=out_specs, out_shape=out_shapes,
        input_output_aliases=aliases, scratch_shapes=list(scratch),
        compiler_params=pltpu.CompilerParams(dimension_semantics=sem, vmem_limit_bytes=VMEM_LIMIT),
    )(*args)


def _dot(a, b):
    return jnp.dot(a, b, preferred_element_type=F32)


def _dot_nt(a, b):
    return lax.dot_general(a, b, (((1,), (1,)), ((), ())), preferred_element_type=F32)


def _dot_tn(a, b):
    return lax.dot_general(a, b, (((0,), (0,)), ((), ())), preferred_element_type=F32)


def _rms(x, g):
    return x * lax.rsqrt(jnp.mean(x * x, axis=-1, keepdims=True) + EPS) * g


def _ffn_body(x_ref, g_ref, wg_ref, wu_ref, wd_ref, o_ref, h_ref, acc_ref):
    j = pl.program_id(1)

    @pl.when(j == 0)
    def _():
        h_ref[...] = _rms(x_ref[...], g_ref[...]).astype(BF16)
        acc_ref[...] = jnp.zeros_like(acc_ref)

    h = h_ref[...]
    a = _dot(h, wg_ref[...])
    b = _dot(h, wu_ref[...])
    act = (a * jax.nn.sigmoid(a) * b).astype(BF16)
    acc_ref[...] += _dot(act, wd_ref[...])

    @pl.when(j == pl.num_programs(1) - 1)
    def _():
        o_ref[...] = x_ref[...] + 0.5 * acc_ref[...]


def _ffn(x, g, wg, wu, wd, out_rows=None, out_row0=0, dst=None):
    T, D = x.shape
    F = wg.shape[1]
    tm = _tile(T, 512)
    tf = _tile(F, 512)
    out_rows = out_rows or T
    assert out_row0 % tm == 0
    i0 = out_row0 // tm
    return _call(
        _ffn_body, (T // tm, F // tf),
        [pl.BlockSpec((tm, D), lambda i, j: (i, 0)),
         pl.BlockSpec((1, D), lambda i, j: (0, 0)),
         pl.BlockSpec((D, tf), lambda i, j: (0, j)),
         pl.BlockSpec((D, tf), lambda i, j: (0, j)),
         pl.BlockSpec((tf, D), lambda i, j: (j, 0))],
        [x, g.reshape(1, D), wg, wu, wd],
        pl.BlockSpec((tm, D), lambda i, j: (i + i0, 0)),
        jax.ShapeDtypeStruct((out_rows, D), F32),
        ("parallel", "arbitrary"), dsts=[dst],
        scratch=[pltpu.VMEM((tm, D), BF16), pltpu.VMEM((tm, D), F32)])


def _norm_body(x_ref, g_ref, o_ref):
    o_ref[...] = _rms(x_ref[...], g_ref[...]).astype(o_ref.dtype)


def _norm(x, g, dtype, row0=0, n_rows=None):
    D = x.shape[1]
    n_rows = n_rows or x.shape[0]
    tm = _tile(n_rows, 512)
    assert row0 % tm == 0
    i0 = row0 // tm
    return _call(
        _norm_body, (n_rows // tm,),
        [pl.BlockSpec((tm, D), lambda i: (i + i0, 0)), pl.BlockSpec((1, D), lambda i: (0, 0))],
        [x, g.reshape(1, D)],
        pl.BlockSpec((tm, D), lambda i: (i, 0)),
        jax.ShapeDtypeStruct((n_rows, D), dtype), ("parallel",))


def _proj_body(h_ref, w_ref, *refs, epilogue, n_aux):
    aux = [r[...] for r in refs[:n_aux]]
    outs = refs[n_aux:]
    z = _dot(h_ref[...], w_ref[...])
    vals = epilogue(z, *aux)
    for o_ref, v in zip(outs, vals):
        o_ref[...] = v.astype(o_ref.dtype)


def _proj(h, w, aux, epilogue, out_dtypes):
    T, K = h.shape
    N = w.shape[1]
    tm = _tile(T, 1024)
    tn = _tile(N, 512)
    aux_specs = [pl.BlockSpec((a.shape[0], tn), lambda i, j: (0, j)) for a in aux]
    return _call(
        functools.partial(_proj_body, epilogue=epilogue, n_aux=len(aux)),
        (T // tm, N // tn),
        [pl.BlockSpec((tm, K), lambda i, j: (i, 0)), pl.BlockSpec((K, tn), lambda i, j: (0, j))] + aux_specs,
        [h, w, *aux],
        [pl.BlockSpec((tm, tn), lambda i, j: (i, j)) for _ in out_dtypes],
        [jax.ShapeDtypeStruct((T, N), d) for d in out_dtypes],
        ("parallel", "parallel"))


def _heads_body(h_ref, w_ref, *o_refs, nb, ts, hb):
    z = _dot(h_ref[...], w_ref[...])
    for hh in range(hb):
        zz = z[:, hh * LANES:(hh + 1) * LANES].reshape(nb, ts, LANES)
        for o_ref in o_refs:
            o_ref[:, hh] = zz.astype(o_ref.dtype)


def _proj_heads(h, w, row0, B, S, outs):
    K = h.shape[1]
    N = w.shape[1]
    H = N // LANES
    if S >= 512:
        ts, nb = _tile(S, 512), 1
    else:
        nb = _tile(B, max(1, 512 // S))
        ts = S
    tm = nb * ts
    tn = _tile(N, 512)
    hb = tn // LANES
    spt = S // ts
    assert row0 % tm == 0
    i0 = row0 // tm
    if nb == 1:
        omap = lambda i, j: (i // spt, j, i % spt, 0)
    else:
        omap = lambda i, j: (i, j, 0, 0)
    out_specs, out_shapes, dsts = [], [], []
    for dtype, layer, n_layers, dst in outs:
        if n_layers:
            out_specs.append(pl.BlockSpec((None, nb, hb, ts, LANES), lambda i, j, layer=layer: (layer, *omap(i, j))))
            out_shapes.append(jax.ShapeDtypeStruct((n_layers, B, H, S, LANES), dtype))
        else:
            out_specs.append(pl.BlockSpec((nb, hb, ts, LANES), omap))
            out_shapes.append(jax.ShapeDtypeStruct((B, H, S, LANES), dtype))
        dsts.append(dst)
    return _call(
        functools.partial(_heads_body, nb=nb, ts=ts, hb=hb),
        (B * S // tm, N // tn),
        [pl.BlockSpec((tm, K), lambda i, j: (i + i0, 0)), pl.BlockSpec((K, tn), lambda i, j: (0, j))],
        [h, w], out_specs, out_shapes, ("parallel", "parallel"), dsts=dsts)


def _mm_res_body(*refs, n):
    x_ref, o_ref = refs[2 * n], refs[2 * n + 1]
    z = _dot(refs[0][...], refs[n][...])
    for a_ref, w_ref in zip(refs[1:n], refs[n + 1:2 * n]):
        z += _dot(a_ref[...], w_ref[...])
    o_ref[...] = x_ref[...] + z


def _mm_res(lhs, ws, x):
    T, N = x.shape
    n = len(lhs)
    tm = _tile(T, 1024)
    tn = _tile(N, 512)
    return _call(
        functools.partial(_mm_res_body, n=n), (T // tm, N // tn),
        ([pl.BlockSpec((tm, a.shape[1]), lambda i, j: (i, 0)) for a in lhs]
         + [pl.BlockSpec((w.shape[0], tn), lambda i, j: (0, j)) for w in ws]
         + [pl.BlockSpec((tm, tn), lambda i, j: (i, j))]),
        [*lhs, *ws, x],
        pl.BlockSpec((tm, tn), lambda i, j: (i, j)),
        jax.ShapeDtypeStruct((T, N), F32), ("parallel", "parallel"))


def _cumsum_body(lf_ref, base_ref, ltok_ref, lrow_ref, lfrow_ref, carry_ref, *, ts, seg, tiles_per_seq, n_heads):
    lf = lf_ref[...]
    r = lax.broadcasted_iota(jnp.int32, (ts, ts), 0)
    c = lax.broadcasted_iota(jnp.int32, (ts, ts), 1)
    tri = jnp.where(r >= c, 1.0, 0.0).astype(F32)
    if seg < ts:
        tri = jnp.where(r // seg == c // seg, tri, 0.0)
    L = jnp.dot(tri, lf, precision=HIGHEST, preferred_element_type=F32)
    if tiles_per_seq:
        @pl.when(pl.program_id(0) % tiles_per_seq == 0)
        def _():
            carry_ref[...] = jnp.zeros_like(carry_ref)
        L = L + carry_ref[...]
        carry_ref[...] = L[ts - 1:ts, :]
    else:
        L = L + base_ref[...]
    ltok_ref[...] = L
    Lt = L.T[:n_heads]
    lft = lf.T[:n_heads]
    for s in range(ts // seg):
        lrow_ref[s] = Lt[:, s * seg:(s + 1) * seg]
        lfrow_ref[s] = lft[:, s * seg:(s + 1) * seg]


def _cumsum(lf_tok, base, row0, B, S, n_heads, layer, n_layers, dst):
    if S >= LANES:
        ts = _tile(S, 512)
        seg, tps, nseg = ts, S // ts, 1
    else:
        ts, seg, tps = LANES, S, 0
        nseg = ts // seg
    assert row0 % ts == 0 and (B * S) % ts == 0
    i0 = row0 // ts
    if tps:
        omap = lambda i: (i // tps, 0, i % tps)
    else:
        omap = lambda i: (i, 0, 0)
    return _call(
        functools.partial(_cumsum_body, ts=ts, seg=seg, tiles_per_seq=tps, n_heads=n_heads),
        (B * S // ts,),
        [pl.BlockSpec((ts, LANES), lambda i: (i + i0, 0)), pl.BlockSpec((ts, LANES), lambda i: (i, 0))],
        [lf_tok, base],
        [pl.BlockSpec((ts, LANES), lambda i: (i, 0)),
         pl.BlockSpec((nseg, n_heads, seg), omap),
         pl.BlockSpec((None, nseg, n_heads, seg), lambda i: (layer, *omap(i)))],
        [jax.ShapeDtypeStruct((B * S, LANES), F32), jax.ShapeDtypeStruct((B, n_heads, S), F32),
         jax.ShapeDtypeStruct((n_layers, B, n_heads, S), F32)],
        ("arbitrary",), dsts=[None, None, dst], scratch=[pltpu.VMEM((1, LANES), F32)])


def _row_cumsum_body(x_ref, o_ref):
    R, P = x_ref.shape
    r = lax.broadcasted_iota(jnp.int32, (LANES, LANES), 0)
    c = lax.broadcasted_iota(jnp.int32, (LANES, LANES), 1)
    tri = jnp.where(r <= c, 1.0, 0.0).astype(F32)
    carry = jnp.zeros((R, 1), F32)
    for s in range(P // LANES):
        y = jnp.dot(x_ref[:, s * LANES:(s + 1) * LANES], tri, precision=HIGHEST, preferred_element_type=F32) + carry
        o_ref[:, s * LANES:(s + 1) * LANES] = y
        carry = y[:, LANES - 1:LANES]


def _row_cumsum(x, row0, n_rows):
    P = x.shape[1]
    tr = _tile(n_rows, 256)
    assert row0 % tr == 0
    i0 = row0 // tr
    return _call(
        _row_cumsum_body, (n_rows // tr,),
        [pl.BlockSpec((tr, P), lambda i: (i + i0, 0))], [x],
        pl.BlockSpec((tr, P), lambda i: (i, 0)),
        jax.ShapeDtypeStruct((n_rows, P), F32), ("parallel",))


def _fox_prompt_body(q_ref, k_ref, v_ref, l_ref, o_ref, m_ref, s_ref, acc_ref, *, scale, tq, n_sub):
    qi = pl.program_id(2)
    c1 = scale * LOG2E
    sub = tq // n_sub
    m_ref[...] = jnp.full_like(m_ref, MASK_VALUE)
    s_ref[...] = jnp.zeros_like(s_ref)
    acc_ref[...] = jnp.zeros_like(acc_ref)
    l_here = l_ref[0, 0, :, pl.ds(pl.multiple_of(qi * tq, tq), tq)][:, 0:1]

    def block(ki, diagonal):
        k0 = pl.multiple_of(ki * tq, tq)
        k = k_ref[0, 0, pl.ds(k0, tq), :]
        v = v_ref[0, 0, pl.ds(k0, tq), :]
        bias = (l_here - l_ref[0, 0, :, pl.ds(k0, tq)]) * LOG2E
        for u in range(n_sub):
            rows = slice(u * sub, (u + 1) * sub)
            s = _dot_nt(q_ref[0, 0, rows, :], k) * c1 + bias
            if diagonal:
                r = lax.broadcasted_iota(jnp.int32, s.shape, 0) + u * sub
                c = lax.broadcasted_iota(jnp.int32, s.shape, 1)
                s = jnp.where(c <= r, s, MASK_VALUE)
            m_prev = m_ref[rows]
            m_new = jnp.maximum(m_prev, jnp.max(s, axis=-1, keepdims=True))
            alpha = jnp.exp2(m_prev - m_new)
            p = jnp.exp2(s - m_new)
            s_ref[rows] = alpha * s_ref[rows] + jnp.sum(p, axis=-1, keepdims=True)
            acc_ref[rows] = alpha * acc_ref[rows] + _dot(p.astype(BF16), v)
            m_ref[rows] = m_new

    def full_block(ki, carry):
        block(ki, False)
        return carry

    lax.fori_loop(0, qi, full_block, 0)
    block(qi, True)
    o_ref[...] = (acc_ref[...] / s_ref[...]).astype(o_ref.dtype)


def _fox_prompt(q, k, v, l_row, out_rows):
    B, H, S, D = q.shape
    tq = _tile(S, 512)
    nq = S // tq
    full = pl.BlockSpec((1, 1, S, D), lambda b, h, qi: (b, h, 0, 0))
    return _call(
        functools.partial(_fox_prompt_body, scale=D ** -0.5, tq=tq, n_sub=2),
        (B, H, nq),
        [pl.BlockSpec((1, 1, tq, D), lambda b, h, qi: (b, h, qi, 0)), full, full,
         pl.BlockSpec((1, 1, 1, S), lambda b, h, qi: (b, h, 0, 0))],
        [q, k, v, l_row],
        pl.BlockSpec((tq, D), lambda b, h, qi: (b * nq + qi, h)),
        jax.ShapeDtypeStruct((out_rows, H * D), BF16),
        ("parallel", "parallel", "arbitrary"),
        scratch=[pltpu.VMEM((tq, 1), F32), pltpu.VMEM((tq, 1), F32), pltpu.VMEM((tq, D), F32)])


def _head_column(l_tok, h):
    lane = lax.broadcasted_iota(jnp.int32, l_tok.shape, 1)
    return jnp.sum(jnp.where(lane == h, l_tok, 0.0), axis=-1, keepdims=True)


def _fox_sample_body(q_ref, ck_ref, cv_ref, kn_ref, vn_ref, lq_ref, lc_ref, ln_ref, o_ref, *, scale):
    h = pl.program_id(1)
    q = q_ref[0, 0]
    lq = _head_column(lq_ref[...], h)
    s_c = _dot_nt(q, ck_ref[0, 0].astype(BF16)) * scale + (lq - lc_ref[0, 0])
    s_n = _dot_nt(q, kn_ref[0, 0]) * scale + (lq - ln_ref[0, 0])
    r = lax.broadcasted_iota(jnp.int32, s_n.shape, 0)
    c = lax.broadcasted_iota(jnp.int32, s_n.shape, 1)
    s_n = jnp.where(c <= r, s_n, MASK_VALUE)
    m = jnp.maximum(jnp.max(s_c, axis=-1, keepdims=True), jnp.max(s_n, axis=-1, keepdims=True))
    p_c = jnp.exp(s_c - m)
    p_n = jnp.exp(s_n - m)
    l = jnp.sum(p_c, axis=-1, keepdims=True) + jnp.sum(p_n, axis=-1, keepdims=True)
    o = _dot(p_c.astype(BF16), cv_ref[0, 0].astype(BF16)) + _dot(p_n.astype(BF16), vn_ref[0, 0])
    o_ref[...] = (o / l).astype(o_ref.dtype)


def _fox_sample(q, cache_k, cache_v, layer, kn, vn, ln_tok, lc_row, ln_row, out_row0, dst):
    B, H, T, D = q.shape
    P = cache_k.shape[3]
    assert out_row0 % T == 0
    r0 = out_row0 // T
    new = pl.BlockSpec((1, 1, T, D), lambda b, h: (b, h, 0, 0))
    old = pl.BlockSpec((None, 1, 1, P, D), lambda b, h: (layer, b, h, 0, 0))
    return _call(
        functools.partial(_fox_sample_body, scale=D ** -0.5),
        (B, H),
        [new, old, old, new, new,
         pl.BlockSpec((T, LANES), lambda b, h: (b, 0)),
         pl.BlockSpec((1, 1, 1, P), lambda b, h: (b, h, 0, 0)),
         pl.BlockSpec((1, 1, 1, T), lambda b, h: (b, h, 0, 0))],
        [q, cache_k, cache_v, kn, vn, ln_tok, lc_row, ln_row],
        pl.BlockSpec((T, D), lambda b, h: (b + r0, h)),
        jax.ShapeDtypeStruct(dst.shape, dst.dtype),
        ("parallel", "parallel"), dsts=[dst])


def _hgrn_body(*refs, C, has_init):
    if has_init:
        q_ref, k_ref, g_ref, i_ref, gate_ref, gn_ref, s0_ref, o_ref, sf_ref, st_ref = refs
    else:
        q_ref, k_ref, g_ref, i_ref, gate_ref, gn_ref, o_ref, sf_ref, st_ref = refs
        s0_ref = None
    c = pl.program_id(2)

    @pl.when(c == 0)
    def _():
        if has_init:
            st_ref[...] = s0_ref[0, 0].T
        else:
            st_ref[...] = jnp.zeros_like(st_ref)

    q, k, g, v = q_ref[...], k_ref[...], g_ref[...], i_ref[...]
    row = lax.broadcasted_iota(jnp.int32, (C, C), 0)
    col = lax.broadcasted_iota(jnp.int32, (C, C), 1)
    tri = jnp.where(row >= col, 1.0, 0.0).astype(F32)
    G = jnp.dot(tri, g, precision=HIGHEST, preferred_element_type=F32)
    st = st_ref[...]
    o = _dot_nt((q * jnp.exp(G)).astype(BF16), st.astype(BF16))

    A = jnp.where(row == col, _dot_nt(q.astype(BF16), k.astype(BF16)), 0.0)
    trow = lax.broadcasted_iota(jnp.int32, (C, LANES), 0)
    split = jnp.where(row > col, row ^ col, 0)
    yq, yk = G - g, G
    w, log_w = 1, 0
    while w < C:
        qt = (q * jnp.exp(G - yq)).astype(BF16)
        kt = (k * jnp.exp(yk - G)).astype(BF16)
        A = A + jnp.where((split >> log_w) == 1, _dot_nt(qt, kt), 0.0)
        upper = (trow & w) != 0
        yq = jnp.where(upper, pltpu.roll(yq, w, 0), yq)
        yk = jnp.where(upper, yk, pltpu.roll(yk, C - w, 0))
        w, log_w = 2 * w, log_w + 1
    o = o + _dot(A.astype(BF16), v)

    g_last = G[C - 1:C, :]
    ks = (k * jnp.exp(g_last - G)).astype(BF16)
    st_new = st * jnp.exp(g_last) + _dot_tn(v, ks)
    st_ref[...] = st_new

    o_ref[...] = (_rms(o, gn_ref[...]) * gate_ref[...]).astype(o_ref.dtype)

    @pl.when(c == pl.num_programs(2) - 1)
    def _():
        sf_ref[0, 0] = st_new.T


def _hgrn(q, k, g, i, gate, gnorm, s0, row0, B, S, H, layer, n_layers, o_dst, s_dst):
    T = q.shape[0]
    C = min(S, LANES)
    nc = S // C
    assert row0 % C == 0
    r0 = row0 // C
    tok = pl.BlockSpec((C, LANES), lambda b, h, c: (r0 + b * nc + c, h))
    state = pl.BlockSpec((None, 1, 1, LANES, LANES), lambda b, h, c: (layer, b, h, 0, 0))
    in_specs = [tok, tok, tok, tok, tok, pl.BlockSpec((1, LANES), lambda b, h, c: (0, 0))]
    args = [q, k, g, i, gate, gnorm.reshape(1, LANES)]
    if s0 is not None:
        in_specs.append(state)
        args.append(s0)
    return _call(
        functools.partial(_hgrn_body, C=C, has_init=s0 is not None),
        (B, H, nc), in_specs, args,
        [tok, state],
        [jax.ShapeDtypeStruct((T, H * LANES), BF16), jax.ShapeDtypeStruct((n_layers, B, H, LANES, LANES), F32)],
        ("parallel", "parallel", "arbitrary"), dsts=[o_dst, s_dst],
        scratch=[pltpu.VMEM((LANES, LANES), F32)])


def _cmix_body(u_ref, v_ref, lng_ref, lnb_ref, ws_ref, bs_ref, y_ref, *vout, groups):
    v = v_ref[...]
    mu = jnp.mean(v, axis=-1, keepdims=True)
    xc = v - mu
    var = jnp.mean(xc * xc, axis=-1, keepdims=True)
    vn = xc * lax.rsqrt(var + EPS) * lng_ref[...] + lnb_ref[...]
    if vout:
        vout[0][...] = vn
    vb = vn.astype(BF16)
    Lc, Cd = v.shape
    cg = Cd // groups
    r = lax.broadcasted_iota(jnp.int32, (Lc, Lc), 0)
    c = lax.broadcasted_iota(jnp.int32, (Lc, Lc), 1)
    for gi in range(groups):
        sl = slice(gi * cg, (gi + 1) * cg)
        w = jnp.where(c <= r, ws_ref[gi], 0.0).astype(BF16)
        mixed = _dot(w, vb[:, sl]) + bs_ref[:, sl]
        y_ref[:, sl] = (u_ref[:, sl] * mixed).astype(y_ref.dtype)


def _cmix(u, v, ln_g, ln_b, ws, bs, row0, n_rows, Lc, y_dst, v_out):
    T, Cd = u.shape
    G = ws.shape[0]
    assert row0 % Lc == 0
    r0 = row0 // Lc
    tok = pl.BlockSpec((Lc, Cd), lambda i: (i + r0, 0))
    vec = pl.BlockSpec((1, Cd), lambda i: (0, 0))
    out_specs, out_shapes, dsts = [tok], [jax.ShapeDtypeStruct((T, Cd), BF16)], [y_dst]
    if v_out:
        layer, n_layers, dst = v_out
        out_specs.append(pl.BlockSpec((None, Lc, Cd), lambda i: (layer, i, 0)))
        out_shapes.append(jax.ShapeDtypeStruct((n_layers, n_rows, Cd), F32))
        dsts.append(dst)
    return _call(
        functools.partial(_cmix_body, groups=G), (n_rows // Lc,),
        [tok, tok, vec, vec, pl.BlockSpec((G, Lc, Lc), lambda i: (0, 0, 0)),
         pl.BlockSpec((Lc, Cd), lambda i: (0, 0))],
        [u, v, ln_g.reshape(1, Cd), ln_b.reshape(1, Cd), ws, bs],
        out_specs, out_shapes, ("parallel",), dsts=dsts)


def _epi_id(z):
    return (z,)


def _epi_silu(z):
    return (z * jax.nn.sigmoid(z),)


def _epi_gelu(z):
    return (jax.nn.gelu(z),)


def _epi_logsig(z, b):
    return (jax.nn.log_sigmoid(z + b),)


def _epi_hgrn_forget(z, lb_all, *, layer):
    rows = [lb_all[n:n + 1] for n in range(lb_all.shape[0])]
    top = functools.reduce(jnp.maximum, rows)
    e = [jnp.exp(r - top) for r in rows]
    total = functools.reduce(jnp.add, e)
    sm = [a / total for a in e]
    lb = functools.reduce(jnp.add, sm[:layer + 1]) - sm[0]
    f = lb + (1.0 - lb) * jax.nn.sigmoid(z)
    g = jnp.log(jnp.maximum(f, TINY))
    k = (1.0 - lb) * jax.nn.sigmoid(-z)
    return g, k


def kernel(x_prompt, x_sample, cache_k, cache_v, cache_logf, state_hgrn, norm_ffn1, ffn1_gate, ffn1_up, ffn1_down, norm_mix, ab_w_in, ab_b_f, hgrn_lb, hgrn_gnorm, ab_w_out, c_w_in, c_ln_g, c_ln_b, c_w_s, c_b_s, c_w_out, norm_ffn2, ffn2_gate, ffn2_up, ffn2_down, norm_final):
    B, S, D = x_prompt.shape
    DB, DT, _ = x_sample.shape
    depth = norm_ffn1.shape[0]
    NAB, NC = ab_w_in.shape[0], c_w_in.shape[0]
    HA, P, HD = cache_k.shape[2], cache_k.shape[3], cache_k.shape[4]
    HB, DK, DV = state_hgrn.shape[2], state_hgrn.shape[3], state_hgrn.shape[4]
    assert HD == LANES and DK == LANES and DV == LANES and HA <= 8
    WA, WBK, WBV = HA * HD, HB * DK, HB * DV
    G, CL = c_w_s.shape[1], c_w_s.shape[2]
    CD = c_w_out.shape[1]
    Tp, Ts = B * S, DB * DT
    T = Tp + Ts
    bf = lambda a: a.astype(BF16)

    pk = pv = plf = ps = sk = sv = slf = ss = scv = None
    x = None
    for l in range(depth):
        j = l // 2
        w1 = (norm_ffn1[l], bf(ffn1_gate[l]), bf(ffn1_up[l]), bf(ffn1_down[l]))
        if l == 0:
            x = _ffn(x_prompt.reshape(Tp, D), *w1, out_rows=T)
            x = _ffn(x_sample.reshape(Ts, D), *w1, out_rows=T, out_row0=Tp, dst=x)
        else:
            x = _ffn(x, *w1)
        h = _norm(x, norm_mix[l], BF16)
        if l % 2 == 0:
            w_in = ab_w_in[j]
            o1 = 3 * WA + HA
            w_q, w_k, w_v = (bf(w_in[:, n * WA:(n + 1) * WA]) for n in range(3))
            w_f = bf(jnp.pad(w_in[:, 3 * WA:o1], ((0, 0), (0, LANES - HA))))
            w_qb = bf(w_in[:, o1:o1 + WBK])
            w_fb = bf(w_in[:, o1 + WBK:o1 + 2 * WBK])
            w_ib = bf(w_in[:, o1 + 2 * WBK:o1 + 2 * WBK + WBV])
            w_gb = bf(w_in[:, o1 + 2 * WBK + WBV:])
            b_f = jnp.pad(ab_b_f[j], (0, LANES - HA)).reshape(1, LANES)

            (lf_tok,) = _proj(h, w_f, [b_f], _epi_logsig, [F32])
            _, lp_row, plf = _cumsum(lf_tok, jnp.zeros((Tp, LANES), F32), 0, B, S, HA, j, NAB, plf)
            lc_row = _row_cumsum(cache_logf.reshape(NAB * DB * HA, P), j * DB * HA, DB * HA).reshape(DB, HA, P)
            base = jnp.pad(lc_row[:, :, P - 1], ((0, 0), (0, LANES - HA)))
            base = jnp.broadcast_to(base[:, None, :], (DB, DT, LANES)).reshape(Ts, LANES)
            ls_tok, ls_row, slf = _cumsum(lf_tok, base, Tp, DB, DT, HA, j, NAB, slf)

            (qp,) = _proj_heads(h, w_q, 0, B, S, [(BF16, 0, 0, None)])
            pk, kpb = _proj_heads(h, w_k, 0, B, S, [(F32, j, NAB, pk), (BF16, 0, 0, None)])
            pv, vpb = _proj_heads(h, w_v, 0, B, S, [(F32, j, NAB, pv), (BF16, 0, 0, None)])
            (qs,) = _proj_heads(h, w_q, Tp, DB, DT, [(BF16, 0, 0, None)])
            sk, ksb = _proj_heads(h, w_k, Tp, DB, DT, [(F32, j, NAB, sk), (BF16, 0, 0, None)])
            sv, vsb = _proj_heads(h, w_v, Tp, DB, DT, [(F32, j, NAB, sv), (BF16, 0, 0, None)])
            oa = _fox_prompt(qp, kpb, vpb, lp_row.reshape(B, HA, 1, S), T)
            oa = _fox_sample(qs, cache_k, cache_v, j, ksb, vsb, ls_tok,
                             lc_row.reshape(DB, HA, 1, P), ls_row.reshape(DB, HA, 1, DT), Tp, oa)

            (qb,) = _proj(h, w_qb, [], _epi_silu, [F32])
            gb, kb = _proj(h, w_fb, [hgrn_lb], functools.partial(_epi_hgrn_forget, layer=j), [F32, F32])
            (ib,) = _proj(h, w_ib, [], _epi_id, [BF16])
            (gate,) = _proj(h, w_gb, [], _epi_silu, [F32])
            ob, ps = _hgrn(qb, kb, gb, ib, gate, hgrn_gnorm[j], None, 0, B, S, HB, j, NAB, None, ps)
            ob, ss = _hgrn(qb, kb, gb, ib, gate, hgrn_gnorm[j], state_hgrn, Tp, DB, DT, HB, j, NAB, ob, ss)

            w_out = bf(ab_w_out[j])
            x = _mm_res([oa, ob], [w_out[:WA], w_out[WA:]], x)
        else:
            w_in = bf(c_w_in[j])
            (u,) = _proj(h, w_in[:, :CD], [], _epi_gelu, [F32])
            (v,) = _proj(h, w_in[:, CD:], [], _epi_gelu, [F32])
            y = None
            for row0, n_rows, seq in ((0, Tp, S), (Tp, Ts, DT)):
                Lc = min(seq, CL)
                ws = c_w_s[j][:, :Lc, :Lc]
                bs = jnp.repeat(c_b_s[j][:, :Lc].T, CD // G, axis=1)
                if row0 == 0:
                    (y,) = _cmix(u, v, c_ln_g[j], c_ln_b[j], ws, bs, row0, n_rows, Lc, y, None)
                else:
                    y, scv = _cmix(u, v, c_ln_g[j], c_ln_b[j], ws, bs, row0, n_rows, Lc, y, (j, NC, scv))
            x = _mm_res([y], [bf(c_w_out[j])], x)
        x = _ffn(x, norm_ffn2[l], bf(ffn2_gate[l]), bf(ffn2_up[l]), bf(ffn2_down[l]))

    y_p = _norm(x, norm_final, F32, 0, Tp)
    y_s = _norm(x, norm_final, F32, Tp, Ts)
    return (y_p.reshape(B, S, D), y_s.reshape(DB, DT, D), pk, pv, plf, ps,
            sk, sv, slf, ss, scv.reshape(NC, DB, DT, CD))
```

```python
import functools
import math

import jax
import jax.numpy as jnp
from jax import lax
from jax.experimental import pallas as pl
from jax.experimental.pallas import tpu as pltpu

F32 = jnp.float32
BF16 = jnp.bfloat16
EPS = 1e-6
TINY = 1e-30
MASK_VALUE = -1e30
LOG2E = math.log2(math.e)
HIGHEST = lax.Precision.HIGHEST

LANES = 128
MIB = 1 << 20
VMEM_LIMIT = 56 * MIB
HGRN_HEADS_PER_STEP = 8
FOX_SAMPLE_HEADS_PER_STEP = 2
FOX_SAMPLE_KEY_SPLITS = 2


def _tile(n, target):
    if n <= target:
        return n
    for t in range(target, 7, -1):
        if n % t == 0 and t % 8 == 0:
            return t
    return n


def _call(body, grid, in_specs, args, out_specs, out_shapes, sem, dsts=None, scratch=()):
    in_specs, args = list(in_specs), list(args)
    n_in = len(args)
    aliases = {}
    for k, d in enumerate(dsts or ()):
        if d is not None:
            aliases[len(args)] = k
            in_specs.append(pl.BlockSpec(memory_space=pl.ANY))
            args.append(d)
    n_all = len(args)

    def wrapped(*refs):
        body(*refs[:n_in], *refs[n_all:])

    return pl.pallas_call(
        wrapped, grid=grid, in_specs=in_specs, out_specs=out_specs, out_shape=out_shapes,
        input_output_aliases=aliases, scratch_shapes=list(scratch),
        compiler_params=pltpu.CompilerParams(dimension_semantics=sem, vmem_limit_bytes=VMEM_LIMIT),
    )(*args)


def _dot(a, b):
    return jnp.dot(a, b, preferred_element_type=F32)


def _dot_nt(a, b):
    return lax.dot_general(a, b, (((1,), (1,)), ((), ())), preferred_element_type=F32)


def _dot_tn(a, b):
    return lax.dot_general(a, b, (((0,), (0,)), ((), ())), preferred_element_type=F32)


def _rms(x, g):
    return x * lax.rsqrt(jnp.mean(x * x, axis=-1, keepdims=True) + EPS) * g


def _ffn_body(x_ref, g_ref, wg_ref, wu_ref, wd_ref, o_ref, h_ref, acc_ref):
    j = pl.program_id(1)

    @pl.when(j == 0)
    def _():
        h_ref[...] = _rms(x_ref[...], g_ref[...]).astype(BF16)
        acc_ref[...] = jnp.zeros_like(acc_ref)

    h = h_ref[...]
    a = _dot(h, wg_ref[...])
    b = _dot(h, wu_ref[...])
    act = (a * jax.nn.sigmoid(a) * b).astype(BF16)
    acc_ref[...] += _dot(act, wd_ref[...])

    @pl.when(j == pl.num_programs(1) - 1)
    def _():
        o_ref[...] = x_ref[...] + 0.5 * acc_ref[...]


def _ffn(x, g, wg, wu, wd, out_rows=None, out_row0=0, dst=None):
    T, D = x.shape
    F = wg.shape[1]
    tm = _tile(T, 512)
    tf = _tile(F, 512)
    out_rows = out_rows or T
    assert out_row0 % tm == 0
    i0 = out_row0 // tm
    return _call(
        _ffn_body, (T // tm, F // tf),
        [pl.BlockSpec((tm, D), lambda i, j: (i, 0)),
         pl.BlockSpec((1, D), lambda i, j: (0, 0)),
         pl.BlockSpec((D, tf), lambda i, j: (0, j)),
         pl.BlockSpec((D, tf), lambda i, j: (0, j)),
         pl.BlockSpec((tf, D), lambda i, j: (j, 0))],
        [x, g.reshape(1, D), wg, wu, wd],
        pl.BlockSpec((tm, D), lambda i, j: (i + i0, 0)),
        jax.ShapeDtypeStruct((out_rows, D), F32),
        ("parallel", "arbitrary"), dsts=[dst],
        scratch=[pltpu.VMEM((tm, D), BF16), pltpu.VMEM((tm, D), F32)])


def _norm_body(x_ref, g_ref, o_ref):
    o_ref[...] = _rms(x_ref[...], g_ref[...]).astype(o_ref.dtype)


def _norm(x, g, dtype, row0=0, n_rows=None):
    D = x.shape[1]
    n_rows = n_rows or x.shape[0]
    tm = _tile(n_rows, 512)
    assert row0 % tm == 0
    i0 = row0 // tm
    return _call(
        _norm_body, (n_rows // tm,),
        [pl.BlockSpec((tm, D), lambda i: (i + i0, 0)), pl.BlockSpec((1, D), lambda i: (0, 0))],
        [x, g.reshape(1, D)],
        pl.BlockSpec((tm, D), lambda i: (i, 0)),
        jax.ShapeDtypeStruct((n_rows, D), dtype), ("parallel",))


def _proj_body(h_ref, w_ref, *refs, epilogue, n_aux):
    aux = [r[...] for r in refs[:n_aux]]
    outs = refs[n_aux:]
    z = _dot(h_ref[...], w_ref[...])
    vals = epilogue(z, *aux)
    for o_ref, v in zip(outs, vals):
        o_ref[...] = v.astype(o_ref.dtype)


def _proj(h, w, aux, epilogue, out_dtypes):
    T, K = h.shape
    N = w.shape[1]
    tm = _tile(T, 1024)
    tn = _tile(N, 512)
    aux_specs = [pl.BlockSpec((a.shape[0], tn), lambda i, j: (0, j)) for a in aux]
    return _call(
        functools.partial(_proj_body, epilogue=epilogue, n_aux=len(aux)),
        (T // tm, N // tn),
        [pl.BlockSpec((tm, K), lambda i, j: (i, 0)), pl.BlockSpec((K, tn), lambda i, j: (0, j))] + aux_specs,
        [h, w, *aux],
        [pl.BlockSpec((tm, tn), lambda i, j: (i, j)) for _ in out_dtypes],
        [jax.ShapeDtypeStruct((T, N), d) for d in out_dtypes],
        ("parallel", "parallel"))


def _heads_body(h_ref, w_ref, *o_refs, nb, ts, hb):
    z = _dot(h_ref[...], w_ref[...])
    for hh in range(hb):
        zz = z[:, hh * LANES:(hh + 1) * LANES].reshape(nb, ts, LANES)
        for o_ref in o_refs:
            o_ref[:, hh] = zz.astype(o_ref.dtype)


def _proj_heads(h, w, row0, B, S, outs):
    K = h.shape[1]
    N = w.shape[1]
    H = N // LANES
    if S >= 512:
        ts, nb = _tile(S, 512), 1
    else:
        nb = _tile(B, max(1, 512 // S))
        ts = S
    tm = nb * ts
    tn = _tile(N, 512)
    hb = tn // LANES
    spt = S // ts
    assert row0 % tm == 0
    i0 = row0 // tm
    if nb == 1:
        omap = lambda i, j: (i // spt, j, i % spt, 0)
    else:
        omap = lambda i, j: (i, j, 0, 0)
    out_specs, out_shapes, dsts = [], [], []
    for dtype, layer, n_layers, dst in outs:
        if n_layers:
            out_specs.append(pl.BlockSpec((None, nb, hb, ts, LANES), lambda i, j, layer=layer: (layer, *omap(i, j))))
            out_shapes.append(jax.ShapeDtypeStruct((n_layers, B, H, S, LANES), dtype))
        else:
            out_specs.append(pl.BlockSpec((nb, hb, ts, LANES), omap))
            out_shapes.append(jax.ShapeDtypeStruct((B, H, S, LANES), dtype))
        dsts.append(dst)
    return _call(
        functools.partial(_heads_body, nb=nb, ts=ts, hb=hb),
        (B * S // tm, N // tn),
        [pl.BlockSpec((tm, K), lambda i, j: (i + i0, 0)), pl.BlockSpec((K, tn), lambda i, j: (0, j))],
        [h, w], out_specs, out_shapes, ("parallel", "parallel"), dsts=dsts)


def _mm_res_body(*refs, n):
    x_ref, o_ref = refs[2 * n], refs[2 * n + 1]
    z = _dot(refs[0][...], refs[n][...])
    for a_ref, w_ref in zip(refs[1:n], refs[n + 1:2 * n]):
        z += _dot(a_ref[...], w_ref[...])
    o_ref[...] = x_ref[...] + z


def _mm_res(lhs, ws, x):
    T, N = x.shape
    n = len(lhs)
    tm = _tile(T, 1024)
    tn = _tile(N, 512)
    return _call(
        functools.partial(_mm_res_body, n=n), (T // tm, N // tn),
        ([pl.BlockSpec((tm, a.shape[1]), lambda i, j: (i, 0)) for a in lhs]
         + [pl.BlockSpec((w.shape[0], tn), lambda i, j: (0, j)) for w in ws]
         + [pl.BlockSpec((tm, tn), lambda i, j: (i, j))]),
        [*lhs, *ws, x],
        pl.BlockSpec((tm, tn), lambda i, j: (i, j)),
        jax.ShapeDtypeStruct((T, N), F32), ("parallel", "parallel"))


def _cumsum_body(lf_ref, base_ref, ltok_ref, lrow_ref, lfrow_ref, carry_ref, *, ts, seg, tiles_per_seq, n_heads):
    lf = lf_ref[...]
    r = lax.broadcasted_iota(jnp.int32, (ts, ts), 0)
    c = lax.broadcasted_iota(jnp.int32, (ts, ts), 1)
    tri = jnp.where(r >= c, 1.0, 0.0).astype(F32)
    if seg < ts:
        tri = jnp.where(r // seg == c // seg, tri, 0.0)
    L = jnp.dot(tri, lf, precision=HIGHEST, preferred_element_type=F32)
    if tiles_per_seq:
        @pl.when(pl.program_id(0) % tiles_per_seq == 0)
        def _():
            carry_ref[...] = jnp.zeros_like(carry_ref)
        L = L + carry_ref[...]
        carry_ref[...] = L[ts - 1:ts, :]
    else:
        L = L + base_ref[...]
    ltok_ref[...] = L
    Lt = L.T[:n_heads]
    lft = lf.T[:n_heads]
    for s in range(ts // seg):
        lrow_ref[s] = Lt[:, s * seg:(s + 1) * seg]
        lfrow_ref[s] = lft[:, s * seg:(s + 1) * seg]


def _cumsum(lf_tok, base, row0, B, S, n_heads, layer, n_layers, dst):
    if S >= LANES:
        ts = _tile(S, 512)
        seg, tps, nseg = ts, S // ts, 1
    else:
        ts, seg, tps = LANES, S, 0
        nseg = ts // seg
    assert row0 % ts == 0 and (B * S) % ts == 0
    i0 = row0 // ts
    if tps:
        omap = lambda i: (i // tps, 0, i % tps)
    else:
        omap = lambda i: (i, 0, 0)
    return _call(
        functools.partial(_cumsum_body, ts=ts, seg=seg, tiles_per_seq=tps, n_heads=n_heads),
        (B * S // ts,),
        [pl.BlockSpec((ts, LANES), lambda i: (i + i0, 0)), pl.BlockSpec((ts, LANES), lambda i: (i, 0))],
        [lf_tok, base],
        [pl.BlockSpec((ts, LANES), lambda i: (i, 0)),
         pl.BlockSpec((nseg, n_heads, seg), omap),
         pl.BlockSpec((None, nseg, n_heads, seg), lambda i: (layer, *omap(i)))],
        [jax.ShapeDtypeStruct((B * S, LANES), F32), jax.ShapeDtypeStruct((B, n_heads, S), F32),
         jax.ShapeDtypeStruct((n_layers, B, n_heads, S), F32)],
        ("arbitrary",), dsts=[None, None, dst], scratch=[pltpu.VMEM((1, LANES), F32)])


def _row_cumsum_body(x_ref, o_ref):
    R, P = x_ref.shape
    r = lax.broadcasted_iota(jnp.int32, (LANES, LANES), 0)
    c = lax.broadcasted_iota(jnp.int32, (LANES, LANES), 1)
    tri = jnp.where(r <= c, 1.0, 0.0).astype(F32)
    carry = jnp.zeros((R, 1), F32)
    for s in range(P // LANES):
        y = jnp.dot(x_ref[:, s * LANES:(s + 1) * LANES], tri, precision=HIGHEST, preferred_element_type=F32) + carry
        o_ref[:, s * LANES:(s + 1) * LANES] = y
        carry = y[:, LANES - 1:LANES]


def _row_cumsum(x, row0, n_rows):
    P = x.shape[1]
    tr = _tile(n_rows, 256)
    assert row0 % tr == 0
    i0 = row0 // tr
    return _call(
        _row_cumsum_body, (n_rows // tr,),
        [pl.BlockSpec((tr, P), lambda i: (i + i0, 0))], [x],
        pl.BlockSpec((tr, P), lambda i: (i, 0)),
        jax.ShapeDtypeStruct((n_rows, P), F32), ("parallel",))


def _fox_prompt_body(q_ref, k_ref, v_ref, l_ref, o_ref, m_ref, d_ref, acc_ref, s0_ref, s1_ref, *, scale, tq, sub):
    qi = pl.program_id(2)
    c1 = scale * LOG2E
    tk = tq // 2
    n_sub = tq // sub
    every = tuple(range(n_sub))
    m_ref[...] = jnp.full_like(m_ref, MASK_VALUE)
    d_ref[...] = jnp.zeros_like(d_ref)
    acc_ref[...] = jnp.zeros_like(acc_ref)
    l_here = l_ref[0, 0, :, pl.ds(pl.multiple_of(qi * tq, tq), LANES)][:, 0:1]

    def scores(ki, s_ref, subs):
        k0 = pl.multiple_of(ki * tk, tk)
        k = k_ref[0, 0, pl.ds(k0, tk), :]
        bias = (l_here - l_ref[0, 0, :, pl.ds(k0, tk)]) * LOG2E
        for u in subs:
            rows = slice(u * sub, (u + 1) * sub)
            s_ref[rows] = _dot_nt(q_ref[0, 0, rows, :], k) * c1 + bias

    def update(ki, s_ref, subs, diagonal=None):
        k0 = pl.multiple_of(ki * tk, tk)
        v = v_ref[0, 0, pl.ds(k0, tk), :]
        for u in subs:
            rows = slice(u * sub, (u + 1) * sub)
            s = s_ref[rows]
            if diagonal is not None and u * sub < (diagonal + 1) * tk - 1:
                r = lax.broadcasted_iota(jnp.int32, s.shape, 0) + u * sub
                c = lax.broadcasted_iota(jnp.int32, s.shape, 1) + diagonal * tk
                s = jnp.where(c <= r, s, MASK_VALUE)
            m_prev = m_ref[rows]
            m_new = jnp.maximum(m_prev, jnp.max(s, axis=-1, keepdims=True))
            alpha = jnp.exp2(m_prev - m_new)
            p = jnp.exp2(s - m_new)
            d_ref[rows] = alpha * d_ref[rows] + jnp.sum(p, axis=-1, keepdims=True)
            acc_ref[rows] = alpha * acc_ref[rows] + _dot(p.astype(BF16), v)
            m_ref[rows] = m_new

    def two_blocks(t, carry):
        scores(2 * t + 1, s1_ref, every)
        update(2 * t, s0_ref, every)
        scores(2 * t + 2, s0_ref, every)
        update(2 * t + 1, s1_ref, every)
        return carry

    scores(0, s0_ref, every)
    lax.fori_loop(0, qi, two_blocks, 0)
    late = tuple(u for u in every if (u + 1) * sub > tk)
    scores(2 * qi + 1, s1_ref, late)
    update(2 * qi, s0_ref, every, diagonal=0)
    update(2 * qi + 1, s1_ref, late, diagonal=1)
    o_ref[...] = (acc_ref[...] / d_ref[...]).astype(o_ref.dtype)


def _fox_prompt(q, k, v, l_row, out_rows):
    B, H, S, D = q.shape
    tq = _tile(S, 1024)
    sub = min(256, tq // 2)
    assert tq % (2 * LANES) == 0 and (tq // 2) % sub == 0
    nq = S // tq
    full = pl.BlockSpec((1, 1, S, D), lambda b, h, qi: (b, h, 0, 0))
    return _call(
        functools.partial(_fox_prompt_body, scale=D ** -0.5, tq=tq, sub=sub),
        (B, H, nq),
        [pl.BlockSpec((1, 1, tq, D), lambda b, h, qi: (b, h, qi, 0)), full, full,
         pl.BlockSpec((1, 1, 1, S), lambda b, h, qi: (b, h, 0, 0))],
        [q, k, v, l_row],
        pl.BlockSpec((tq, D), lambda b, h, qi: (b * nq + qi, h)),
        jax.ShapeDtypeStruct((out_rows, H * D), BF16),
        ("parallel", "parallel", "arbitrary"),
        scratch=[pltpu.VMEM((tq, 1), F32), pltpu.VMEM((tq, 1), F32), pltpu.VMEM((tq, D), F32),
                 pltpu.VMEM((tq, tq // 2), F32), pltpu.VMEM((tq, tq // 2), F32)])


def _head_column(l_tok, h):
    lane = lax.broadcasted_iota(jnp.int32, l_tok.shape, 1)
    return jnp.sum(jnp.where(lane == h, l_tok, 0.0), axis=-1, keepdims=True)


def _fox_sample_body(q_ref, *refs, scale, hps, n_split):
    ck_refs, cv_refs = refs[:n_split], refs[n_split:2 * n_split]
    kn_ref, vn_ref, lq_ref, lc_ref, ln_ref, o_ref = refs[2 * n_split:]
    hg = pl.program_id(1)
    pn = ck_refs[0].shape[-2]
    D = q_ref.shape[-1]
    for hh in range(hps):
        q = q_ref[0, hh]
        lq = _head_column(lq_ref[...], hg * hps + hh)
        s_c = [_dot_nt(q, ck_refs[n][0, hh].astype(BF16)) * scale + (lq - lc_ref[0, hh, :, n * pn:(n + 1) * pn])
               for n in range(n_split)]
        s_n = _dot_nt(q, kn_ref[0, hh]) * scale + (lq - ln_ref[0, hh])
        r = lax.broadcasted_iota(jnp.int32, s_n.shape, 0)
        c = lax.broadcasted_iota(jnp.int32, s_n.shape, 1)
        s_n = jnp.where(c <= r, s_n, MASK_VALUE)
        m = jnp.max(s_n, axis=-1, keepdims=True)
        for s in s_c:
            m = jnp.maximum(m, jnp.max(s, axis=-1, keepdims=True))
        p_n = jnp.exp(s_n - m)
        l = jnp.sum(p_n, axis=-1, keepdims=True)
        o = _dot(p_n.astype(BF16), vn_ref[0, hh])
        for n in range(n_split):
            p = jnp.exp(s_c[n] - m)
            l = l + jnp.sum(p, axis=-1, keepdims=True)
            o = o + _dot(p.astype(BF16), cv_refs[n][0, hh].astype(BF16))
        o_ref[:, hh * D:(hh + 1) * D] = (o / l).astype(o_ref.dtype)


def _fox_sample(q, cache_k, cache_v, layer, kn, vn, ln_tok, lc_row, ln_row, out_row0, dst):
    B, H, T, D = q.shape
    P = cache_k.shape[3]
    assert out_row0 % T == 0
    r0 = out_row0 // T
    hps = math.gcd(H, FOX_SAMPLE_HEADS_PER_STEP)
    n_split = FOX_SAMPLE_KEY_SPLITS if P % (FOX_SAMPLE_KEY_SPLITS * LANES) == 0 else 1
    pn = P // n_split
    new = pl.BlockSpec((1, hps, T, D), lambda b, h: (b, h, 0, 0))
    old = [pl.BlockSpec((None, 1, hps, pn, D), lambda b, h, n=n: (layer, b, h, n, 0)) for n in range(n_split)]
    return _call(
        functools.partial(_fox_sample_body, scale=D ** -0.5, hps=hps, n_split=n_split),
        (B, H // hps),
        [new, *old, *old, new, new,
         pl.BlockSpec((T, LANES), lambda b, h: (b, 0)),
         pl.BlockSpec((1, hps, 1, P), lambda b, h: (b, h, 0, 0)),
         pl.BlockSpec((1, hps, 1, T), lambda b, h: (b, h, 0, 0))],
        [q, *([cache_k] * n_split), *([cache_v] * n_split), kn, vn, ln_tok, lc_row, ln_row],
        pl.BlockSpec((T, hps * D), lambda b, h: (b + r0, h)),
        jax.ShapeDtypeStruct(dst.shape, dst.dtype),
        ("parallel", "parallel"), dsts=[dst])


def _hgrn_body(*refs, C, hpb, has_init):
    if has_init:
        q_ref, k_ref, g_ref, i_ref, gate_ref, gn_ref, s0_ref, o_ref, sf_ref, st_ref = refs
    else:
        q_ref, k_ref, g_ref, i_ref, gate_ref, gn_ref, o_ref, sf_ref, st_ref = refs
        s0_ref = None
    c = pl.program_id(2)

    @pl.when(c == 0)
    def _():
        for hh in range(hpb):
            st_ref[hh] = s0_ref[0, hh].T if has_init else jnp.zeros((LANES, LANES), F32)

    row = lax.broadcasted_iota(jnp.int32, (C, C), 0)
    col = lax.broadcasted_iota(jnp.int32, (C, C), 1)
    tri = jnp.where(row >= col, 1.0, 0.0).astype(F32)
    trow = lax.broadcasted_iota(jnp.int32, (C, LANES), 0)
    split = jnp.where(row > col, row ^ col, 0)

    for hh in range(hpb):
        sl = slice(hh * LANES, (hh + 1) * LANES)
        q, k, g, v = q_ref[:, sl], k_ref[:, sl], g_ref[:, sl], i_ref[:, sl]
        G = jnp.dot(tri, g, precision=HIGHEST, preferred_element_type=F32)
        st = st_ref[hh]
        o = _dot_nt((q * jnp.exp(G)).astype(BF16), st.astype(BF16))

        A = jnp.where(row == col, _dot_nt(q.astype(BF16), k.astype(BF16)), 0.0)
        yq, yk = G - g, G
        w, log_w = 1, 0
        while w < C:
            qt = (q * jnp.exp(G - yq)).astype(BF16)
            kt = (k * jnp.exp(yk - G)).astype(BF16)
            A = A + jnp.where((split >> log_w) == 1, _dot_nt(qt, kt), 0.0)
            upper = (trow & w) != 0
            yq = jnp.where(upper, pltpu.roll(yq, w, 0), yq)
            yk = jnp.where(upper, yk, pltpu.roll(yk, C - w, 0))
            w, log_w = 2 * w, log_w + 1
        o = o + _dot(A.astype(BF16), v)

        g_last = G[C - 1:C, :]
        ks = (k * jnp.exp(g_last - G)).astype(BF16)
        st_new = st * jnp.exp(g_last) + _dot_tn(v, ks)
        st_ref[hh] = st_new
        o_ref[:, sl] = (_rms(o, gn_ref[...]) * gate_ref[:, sl]).astype(o_ref.dtype)

    @pl.when(c == pl.num_programs(2) - 1)
    def _():
        for hh in range(hpb):
            sf_ref[0, hh] = st_ref[hh].T


def _hgrn(q, k, g, i, gate, gnorm, s0, row0, B, S, H, layer, n_layers, o_dst, s_dst):
    T = q.shape[0]
    C = min(S, LANES)
    nc = S // C
    assert row0 % C == 0
    r0 = row0 // C
    hpb = math.gcd(H, HGRN_HEADS_PER_STEP)
    tok = pl.BlockSpec((C, hpb * LANES), lambda b, h, c: (r0 + b * nc + c, h))
    state = pl.BlockSpec((None, 1, hpb, LANES, LANES), lambda b, h, c: (layer, b, h, 0, 0))
    in_specs = [tok, tok, tok, tok, tok, pl.BlockSpec((1, LANES), lambda b, h, c: (0, 0))]
    args = [q, k, g, i, gate, gnorm.reshape(1, LANES)]
    if s0 is not None:
        in_specs.append(state)
        args.append(s0)
    return _call(
        functools.partial(_hgrn_body, C=C, hpb=hpb, has_init=s0 is not None),
        (B, H // hpb, nc), in_specs, args,
        [tok, state],
        [jax.ShapeDtypeStruct((T, H * LANES), BF16), jax.ShapeDtypeStruct((n_layers, B, H, LANES, LANES), F32)],
        ("parallel", "parallel", "arbitrary"), dsts=[o_dst, s_dst],
        scratch=[pltpu.VMEM((hpb, LANES, LANES), F32)])


def _cmix_body(u_ref, v_ref, lng_ref, lnb_ref, ws_ref, bs_ref, y_ref, *vout, groups):
    v = v_ref[...]
    mu = jnp.mean(v, axis=-1, keepdims=True)
    xc = v - mu
    var = jnp.mean(xc * xc, axis=-1, keepdims=True)
    vn = xc * lax.rsqrt(var + EPS) * lng_ref[...] + lnb_ref[...]
    if vout:
        vout[0][...] = vn
    vb = vn.astype(BF16)
    Lc, Cd = v.shape
    cg = Cd // groups
    r = lax.broadcasted_iota(jnp.int32, (Lc, Lc), 0)
    c = lax.broadcasted_iota(jnp.int32, (Lc, Lc), 1)
    for gi in range(groups):
        sl = slice(gi * cg, (gi + 1) * cg)
        w = jnp.where(c <= r, ws_ref[gi], 0.0).astype(BF16)
        mixed = _dot(w, vb[:, sl]) + bs_ref[:, sl]
        y_ref[:, sl] = (u_ref[:, sl] * mixed).astype(y_ref.dtype)


def _cmix(u, v, ln_g, ln_b, ws, bs, row0, n_rows, Lc, y_dst, v_out):
    T, Cd = u.shape
    G = ws.shape[0]
    assert row0 % Lc == 0
    r0 = row0 // Lc
    tok = pl.BlockSpec((Lc, Cd), lambda i: (i + r0, 0))
    vec = pl.BlockSpec((1, Cd), lambda i: (0, 0))
    out_specs, out_shapes, dsts = [tok], [jax.ShapeDtypeStruct((T, Cd), BF16)], [y_dst]
    if v_out:
        layer, n_layers, dst = v_out
        out_specs.append(pl.BlockSpec((None, Lc, Cd), lambda i: (layer, i, 0)))
        out_shapes.append(jax.ShapeDtypeStruct((n_layers, n_rows, Cd), F32))
        dsts.append(dst)
    return _call(
        functools.partial(_cmix_body, groups=G), (n_rows // Lc,),
        [tok, tok, vec, vec, pl.BlockSpec((G, Lc, Lc), lambda i: (0, 0, 0)),
         pl.BlockSpec((Lc, Cd), lambda i: (0, 0))],
        [u, v, ln_g.reshape(1, Cd), ln_b.reshape(1, Cd), ws, bs],
        out_specs, out_shapes, ("parallel",), dsts=dsts)


def _epi_id(z):
    return (z,)


def _epi_silu(z):
    return (z * jax.nn.sigmoid(z),)


def _epi_gelu(z):
    return (jax.nn.gelu(z),)


def _epi_logsig(z, b):
    return (jax.nn.log_sigmoid(z + b),)


def _epi_hgrn_forget(z, lb_all, *, layer):
    rows = [lb_all[n:n + 1] for n in range(lb_all.shape[0])]
    top = functools.reduce(jnp.maximum, rows)
    e = [jnp.exp(r - top) for r in rows]
    total = functools.reduce(jnp.add, e)
    sm = [a / total for a in e]
    lb = functools.reduce(jnp.add, sm[:layer + 1]) - sm[0]
    f = lb + (1.0 - lb) * jax.nn.sigmoid(z)
    g = jnp.log(jnp.maximum(f, TINY))
    k = (1.0 - lb) * jax.nn.sigmoid(-z)
    return g, k


def kernel(x_prompt, x_sample, cache_k, cache_v, cache_logf, state_hgrn, norm_ffn1, ffn1_gate, ffn1_up, ffn1_down, norm_mix, ab_w_in, ab_b_f, hgrn_lb, hgrn_gnorm, ab_w_out, c_w_in, c_ln_g, c_ln_b, c_w_s, c_b_s, c_w_out, norm_ffn2, ffn2_gate, ffn2_up, ffn2_down, norm_final):
    B, S, D = x_prompt.shape
    DB, DT, _ = x_sample.shape
    depth = norm_ffn1.shape[0]
    NAB, NC = ab_w_in.shape[0], c_w_in.shape[0]
    HA, P, HD = cache_k.shape[2], cache_k.shape[3], cache_k.shape[4]
    HB, DK, DV = state_hgrn.shape[2], state_hgrn.shape[3], state_hgrn.shape[4]
    assert HD == LANES and DK == LANES and DV == LANES and HA <= 8
    WA, WBK, WBV = HA * HD, HB * DK, HB * DV
    G, CL = c_w_s.shape[1], c_w_s.shape[2]
    CD = c_w_out.shape[1]
    Tp, Ts = B * S, DB * DT
    T = Tp + Ts
    bf = lambda a: a.astype(BF16)

    pk = pv = plf = ps = sk = sv = slf = ss = scv = None
    x = None
    for l in range(depth):
        j = l // 2
        w1 = (norm_ffn1[l], bf(ffn1_gate[l]), bf(ffn1_up[l]), bf(ffn1_down[l]))
        if l == 0:
            x = _ffn(x_prompt.reshape(Tp, D), *w1, out_rows=T)
            x = _ffn(x_sample.reshape(Ts, D), *w1, out_rows=T, out_row0=Tp, dst=x)
        else:
            x = _ffn(x, *w1)
        h = _norm(x, norm_mix[l], BF16)
        if l % 2 == 0:
            w_in = ab_w_in[j]
            o1 = 3 * WA + HA
            w_q, w_k, w_v = (bf(w_in[:, n * WA:(n + 1) * WA]) for n in range(3))
            w_f = bf(jnp.pad(w_in[:, 3 * WA:o1], ((0, 0), (0, LANES - HA))))
            w_qb = bf(w_in[:, o1:o1 + WBK])
            w_fb = bf(w_in[:, o1 + WBK:o1 + 2 * WBK])
            w_ib = bf(w_in[:, o1 + 2 * WBK:o1 + 2 * WBK + WBV])
            w_gb = bf(w_in[:, o1 + 2 * WBK + WBV:])
            b_f = jnp.pad(ab_b_f[j], (0, LANES - HA)).reshape(1, LANES)

            (lf_tok,) = _proj(h, w_f, [b_f], _epi_logsig, [F32])
            _, lp_row, plf = _cumsum(lf_tok, jnp.zeros((Tp, LANES), F32), 0, B, S, HA, j, NAB, plf)
            lc_row = _row_cumsum(cache_logf.reshape(NAB * DB * HA, P), j * DB * HA, DB * HA).reshape(DB, HA, P)
            base = jnp.pad(lc_row[:, :, P - 1], ((0, 0), (0, LANES - HA)))
            base = jnp.broadcast_to(base[:, None, :], (DB, DT, LANES)).reshape(Ts, LANES)
            ls_tok, ls_row, slf = _cumsum(lf_tok, base, Tp, DB, DT, HA, j, NAB, slf)

            (qp,) = _proj_heads(h, w_q, 0, B, S, [(BF16, 0, 0, None)])
            pk, kpb = _proj_heads(h, w_k, 0, B, S, [(F32, j, NAB, pk), (BF16, 0, 0, None)])
            pv, vpb = _proj_heads(h, w_v, 0, B, S, [(F32, j, NAB, pv), (BF16, 0, 0, None)])
            (qs,) = _proj_heads(h, w_q, Tp, DB, DT, [(BF16, 0, 0, None)])
            sk, ksb = _proj_heads(h, w_k, Tp, DB, DT, [(F32, j, NAB, sk), (BF16, 0, 0, None)])
            sv, vsb = _proj_heads(h, w_v, Tp, DB, DT, [(F32, j, NAB, sv), (BF16, 0, 0, None)])
            oa = _fox_prompt(qp, kpb, vpb, lp_row.reshape(B, HA, 1, S), T)
            oa = _fox_sample(qs, cache_k, cache_v, j, ksb, vsb, ls_tok,
                             lc_row.reshape(DB, HA, 1, P), ls_row.reshape(DB, HA, 1, DT), Tp, oa)

            (qb,) = _proj(h, w_qb, [], _epi_silu, [F32])
            gb, kb = _proj(h, w_fb, [hgrn_lb], functools.partial(_epi_hgrn_forget, layer=j), [F32, F32])
            (ib,) = _proj(h, w_ib, [], _epi_id, [BF16])
            (gate,) = _proj(h, w_gb, [], _epi_silu, [F32])
            ob, ps = _hgrn(qb, kb, gb, ib, gate, hgrn_gnorm[j], None, 0, B, S, HB, j, NAB, None, ps)
            ob, ss = _hgrn(qb, kb, gb, ib, gate, hgrn_gnorm[j], state_hgrn, Tp, DB, DT, HB, j, NAB, ob, ss)

            w_out = bf(ab_w_out[j])
            x = _mm_res([oa, ob], [w_out[:WA], w_out[WA:]], x)
        else:
            w_in = bf(c_w_in[j])
            (u,) = _proj(h, w_in[:, :CD], [], _epi_gelu, [F32])
            (v,) = _proj(h, w_in[:, CD:], [], _epi_gelu, [F32])
            y = None
            for row0, n_rows, seq in ((0, Tp, S), (Tp, Ts, DT)):
                Lc = min(seq, CL)
                ws = c_w_s[j][:, :Lc, :Lc]
                bs = jnp.repeat(c_b_s[j][:, :Lc].T, CD // G, axis=1)
                if row0 == 0:
                    (y,) = _cmix(u, v, c_ln_g[j], c_ln_b[j], ws, bs, row0, n_rows, Lc, y, None)
                else:
                    y, scv = _cmix(u, v, c_ln_g[j], c_ln_b[j], ws, bs, row0, n_rows, Lc, y, (j, NC, scv))
            x = _mm_res([y], [bf(c_w_out[j])], x)
        x = _ffn(x, norm_ffn2[l], bf(ffn2_gate[l]), bf(ffn2_up[l]), bf(ffn2_down[l]))

    y_p = _norm(x, norm_final, F32, 0, Tp)
    y_s = _norm(x, norm_final, F32, Tp, Ts)
    return (y_p.reshape(B, S, D), y_s.reshape(DB, DT, D), pk, pv, plf, ps,
            sk, sv, slf, ss, scv.reshape(NC, DB, DT, CD))
```

```python
import functools
import math

import jax
import jax.numpy as jnp
from jax import lax
from jax.experimental import pallas as pl
from jax.experimental.pallas import tpu as pltpu

F32 = jnp.float32
BF16 = jnp.bfloat16
EPS = 1e-6
TINY = 1e-30
MASK_VALUE = -1e30
LOG2E = math.log2(math.e)
HIGHEST = lax.Precision.HIGHEST

LANES = 128
MIB = 1 << 20
VMEM_LIMIT = 60 * MIB
FFN_TOKEN_TILE = 1024
FFN_FF_TILE = 256
HGRN_HEADS_PER_STEP = 8
FOX_SAMPLE_HEADS_PER_STEP = 2
FOX_SAMPLE_KEY_SPLITS = 2


def _tile(n, target):
    if n <= target:
        return n
    for t in range(target, 7, -1):
        if n % t == 0 and t % 8 == 0:
            return t
    return n


def _call(body, grid, in_specs, args, out_specs, out_shapes, sem, dsts=None, scratch=()):
    in_specs, args = list(in_specs), list(args)
    n_in = len(args)
    aliases = {}
    for k, d in enumerate(dsts or ()):
        if d is not None:
            aliases[len(args)] = k
            in_specs.append(pl.BlockSpec(memory_space=pl.ANY))
            args.append(d)
    n_all = len(args)

    def wrapped(*refs):
        body(*refs[:n_in], *refs[n_all:])

    return pl.pallas_call(
        wrapped, grid=grid, in_specs=in_specs, out_specs=out_specs, out_shape=out_shapes,
        input_output_aliases=aliases, scratch_shapes=list(scratch),
        compiler_params=pltpu.CompilerParams(dimension_semantics=sem, vmem_limit_bytes=VMEM_LIMIT),
    )(*args)


def _dot(a, b):
    return jnp.dot(a, b, preferred_element_type=F32)


def _dot_nt(a, b):
    return lax.dot_general(a, b, (((1,), (1,)), ((), ())), preferred_element_type=F32)


def _dot_tn(a, b):
    return lax.dot_general(a, b, (((0,), (0,)), ((), ())), preferred_element_type=F32)


def _rms(x, g):
    return x * lax.rsqrt(jnp.mean(x * x, axis=-1, keepdims=True) + EPS) * g


def _ffn_body(x_ref, g_ref, wg_ref, wu_ref, wd_ref, *refs, next_norm):
    if next_norm:
        g2_ref, o_ref, h2_ref, h_ref = refs
    else:
        o_ref, h_ref = refs

    @pl.when(pl.program_id(1) == 0)
    def _():
        x = x_ref[...]
        h_ref[...] = _rms(x, g_ref[...]).astype(BF16)
        o_ref[...] = x

    h = h_ref[...]
    a = _dot(h, wg_ref[...].astype(BF16))
    b = _dot(h, wu_ref[...].astype(BF16))
    act = (0.5 * a * jax.nn.sigmoid(a) * b).astype(BF16)
    o_ref[...] += _dot(act, wd_ref[...].astype(BF16))

    if next_norm:
        @pl.when(pl.program_id(1) == pl.num_programs(1) - 1)
        def _():
            h2_ref[...] = _rms(o_ref[...], g2_ref[...]).astype(h2_ref.dtype)


def _ffn(x, g, wg, wu, wd, layer, g_next=None, out_rows=None, out_row0=0, dsts=(None, None)):
    T, D = x.shape
    F = wg.shape[2]
    tm = _tile(T, FFN_TOKEN_TILE)
    tf = _tile(F, FFN_FF_TILE)
    out_rows = out_rows or T
    assert out_row0 % tm == 0
    i0 = out_row0 // tm
    vec = pl.BlockSpec((1, D), lambda i, j: (0, 0))
    tok = pl.BlockSpec((tm, D), lambda i, j: (i + i0, 0))
    in_specs = [pl.BlockSpec((tm, D), lambda i, j: (i, 0)), vec,
                pl.BlockSpec((None, D, tf), lambda i, j: (layer, 0, j)),
                pl.BlockSpec((None, D, tf), lambda i, j: (layer, 0, j)),
                pl.BlockSpec((None, tf, D), lambda i, j: (layer, j, 0))]
    args = [x, g.reshape(1, D), wg, wu, wd]
    out_specs, out_shapes = [tok], [jax.ShapeDtypeStruct((out_rows, D), F32)]
    if g_next is not None:
        in_specs.append(vec)
        args.append(g_next.reshape(1, D))
        out_specs.append(tok)
        out_shapes.append(jax.ShapeDtypeStruct((out_rows, D), BF16))
    outs = _call(
        functools.partial(_ffn_body, next_norm=g_next is not None), (T // tm, F // tf),
        in_specs, args, out_specs, out_shapes,
        ("parallel", "arbitrary"), dsts=list(dsts[:len(out_specs)]),
        scratch=[pltpu.VMEM((tm, D), BF16)])
    return outs if g_next is not None else outs[0]


def _norm_body(x_ref, g_ref, o_ref):
    o_ref[...] = _rms(x_ref[...], g_ref[...]).astype(o_ref.dtype)


def _norm(x, g, dtype, row0=0, n_rows=None):
    D = x.shape[1]
    n_rows = n_rows or x.shape[0]
    tm = _tile(n_rows, 512)
    assert row0 % tm == 0
    i0 = row0 // tm
    return _call(
        _norm_body, (n_rows // tm,),
        [pl.BlockSpec((tm, D), lambda i: (i + i0, 0)), pl.BlockSpec((1, D), lambda i: (0, 0))],
        [x, g.reshape(1, D)],
        pl.BlockSpec((tm, D), lambda i: (i, 0)),
        jax.ShapeDtypeStruct((n_rows, D), dtype), ("parallel",))


def _proj_body(h_ref, w_ref, *refs, epilogue, n_aux):
    aux = [r[...] for r in refs[:n_aux]]
    outs = refs[n_aux:]
    z = _dot(h_ref[...], w_ref[...])
    vals = epilogue(z, *aux)
    for o_ref, v in zip(outs, vals):
        o_ref[...] = v.astype(o_ref.dtype)


def _proj(h, w, aux, epilogue, out_dtypes):
    T, K = h.shape
    N = w.shape[1]
    tm = _tile(T, 1024)
    tn = _tile(N, 512)
    aux_specs = [pl.BlockSpec((a.shape[0], tn), lambda i, j: (0, j)) for a in aux]
    return _call(
        functools.partial(_proj_body, epilogue=epilogue, n_aux=len(aux)),
        (T // tm, N // tn),
        [pl.BlockSpec((tm, K), lambda i, j: (i, 0)), pl.BlockSpec((K, tn), lambda i, j: (0, j))] + aux_specs,
        [h, w, *aux],
        [pl.BlockSpec((tm, tn), lambda i, j: (i, j)) for _ in out_dtypes],
        [jax.ShapeDtypeStruct((T, N), d) for d in out_dtypes],
        ("parallel", "parallel"))


def _proj_hgrn_body(h_ref, w_ref, lb_ref, main_ref, k_ref, *, tiles_per_seg, layer):
    seg = pl.program_id(1) // tiles_per_seg
    z = _dot(h_ref[...], w_ref[...])

    @pl.when((seg == 0) | (seg == 3))
    def _():
        main_ref[...] = z * jax.nn.sigmoid(z)

    @pl.when(seg == 1)
    def _():
        g, k = _epi_hgrn_forget(z, lb_ref[...], layer=layer)
        main_ref[...] = g
        k_ref[...] = k

    @pl.when(seg == 2)
    def _():
        main_ref[...] = z


def _proj_hgrn(h, w, lb_all, layer):
    T, K = h.shape
    W = w.shape[1] // 4
    tm = _tile(T, 1024)
    tn = _tile(W, 512)
    tps = W // tn
    seg_tile = lambda j: jnp.clip(j - tps, 0, tps - 1)
    return _call(
        functools.partial(_proj_hgrn_body, tiles_per_seg=tps, layer=layer),
        (T // tm, 4 * tps),
        [pl.BlockSpec((tm, K), lambda i, j: (i, 0)), pl.BlockSpec((K, tn), lambda i, j: (0, j)),
         pl.BlockSpec((lb_all.shape[0], tn), lambda i, j: (0, seg_tile(j)))],
        [h, w, lb_all],
        [pl.BlockSpec((tm, tn), lambda i, j: (i, j)), pl.BlockSpec((tm, tn), lambda i, j: (i, seg_tile(j)))],
        [jax.ShapeDtypeStruct((T, 4 * W), F32), jax.ShapeDtypeStruct((T, W), F32)],
        ("parallel", "arbitrary"))


def _heads_body(h_ref, w_ref, *o_refs, nb, ts, hb):
    z = _dot(h_ref[...], w_ref[...])
    for hh in range(hb):
        zz = z[:, hh * LANES:(hh + 1) * LANES].reshape(nb, ts, LANES)
        for o_ref in o_refs:
            o_ref[:, hh] = zz.astype(o_ref.dtype)


def _proj_heads(h, w, row0, B, S, outs):
    K = h.shape[1]
    N = w.shape[1]
    H = N // LANES
    if S >= 512:
        ts, nb = _tile(S, 512), 1
    else:
        nb = _tile(B, max(1, 512 // S))
        ts = S
    tm = nb * ts
    tn = _tile(N, 512)
    hb = tn // LANES
    spt = S // ts
    assert row0 % tm == 0
    i0 = row0 // tm
    if nb == 1:
        omap = lambda i, j: (i // spt, j, i % spt, 0)
    else:
        omap = lambda i, j: (i, j, 0, 0)
    out_specs, out_shapes, dsts = [], [], []
    for dtype, layer, n_layers, dst in outs:
        if n_layers:
            out_specs.append(pl.BlockSpec((None, nb, hb, ts, LANES), lambda i, j, layer=layer: (layer, *omap(i, j))))
            out_shapes.append(jax.ShapeDtypeStruct((n_layers, B, H, S, LANES), dtype))
        else:
            out_specs.append(pl.BlockSpec((nb, hb, ts, LANES), omap))
            out_shapes.append(jax.ShapeDtypeStruct((B, H, S, LANES), dtype))
        dsts.append(dst)
    return _call(
        functools.partial(_heads_body, nb=nb, ts=ts, hb=hb),
        (B * S // tm, N // tn),
        [pl.BlockSpec((tm, K), lambda i, j: (i + i0, 0)), pl.BlockSpec((K, tn), lambda i, j: (0, j))],
        [h, w], out_specs, out_shapes, ("parallel", "parallel"), dsts=dsts)


def _mm_res_body(*refs, n):
    x_ref, o_ref = refs[2 * n], refs[2 * n + 1]
    z = _dot(refs[0][...], refs[n][...])
    for a_ref, w_ref in zip(refs[1:n], refs[n + 1:2 * n]):
        z += _dot(a_ref[...], w_ref[...])
    o_ref[...] = x_ref[...] + z


def _mm_res(lhs, ws, x):
    T, N = x.shape
    n = len(lhs)
    tm = _tile(T, 1024)
    tn = _tile(N, 512)
    return _call(
        functools.partial(_mm_res_body, n=n), (T // tm, N // tn),
        ([pl.BlockSpec((tm, a.shape[1]), lambda i, j: (i, 0)) for a in lhs]
         + [pl.BlockSpec((w.shape[0], tn), lambda i, j: (0, j)) for w in ws]
         + [pl.BlockSpec((tm, tn), lambda i, j: (i, j))]),
        [*lhs, *ws, x],
        pl.BlockSpec((tm, tn), lambda i, j: (i, j)),
        jax.ShapeDtypeStruct((T, N), F32), ("parallel", "parallel"))


def _cumsum_body(lf_ref, base_ref, ltok_ref, lrow_ref, lfrow_ref, carry_ref, *, ts, seg, tiles_per_seq, n_heads):
    lf = lf_ref[...]
    r = lax.broadcasted_iota(jnp.int32, (ts, ts), 0)
    c = lax.broadcasted_iota(jnp.int32, (ts, ts), 1)
    tri = jnp.where(r >= c, 1.0, 0.0).astype(F32)
    if seg < ts:
        tri = jnp.where(r // seg == c // seg, tri, 0.0)
    L = jnp.dot(tri, lf, precision=HIGHEST, preferred_element_type=F32)
    if tiles_per_seq:
        @pl.when(pl.program_id(0) % tiles_per_seq == 0)
        def _():
            carry_ref[...] = jnp.zeros_like(carry_ref)
        L = L + carry_ref[...]
        carry_ref[...] = L[ts - 1:ts, :]
    else:
        L = L + base_ref[...]
    ltok_ref[...] = L
    Lt = L.T[:n_heads]
    lft = lf.T[:n_heads]
    for s in range(ts // seg):
        lrow_ref[s] = Lt[:, s * seg:(s + 1) * seg]
        lfrow_ref[s] = lft[:, s * seg:(s + 1) * seg]


def _cumsum(lf_tok, base, row0, B, S, n_heads, layer, n_layers, dst):
    if S >= LANES:
        ts = _tile(S, 512)
        seg, tps, nseg = ts, S // ts, 1
    else:
        ts, seg, tps = LANES, S, 0
        nseg = ts // seg
    assert row0 % ts == 0 and (B * S) % ts == 0
    i0 = row0 // ts
    if tps:
        omap = lambda i: (i // tps, 0, i % tps)
    else:
        omap = lambda i: (i, 0, 0)
    return _call(
        functools.partial(_cumsum_body, ts=ts, seg=seg, tiles_per_seq=tps, n_heads=n_heads),
        (B * S // ts,),
        [pl.BlockSpec((ts, LANES), lambda i: (i + i0, 0)), pl.BlockSpec((ts, LANES), lambda i: (i, 0))],
        [lf_tok, base],
        [pl.BlockSpec((ts, LANES), lambda i: (i, 0)),
         pl.BlockSpec((nseg, n_heads, seg), omap),
         pl.BlockSpec((None, nseg, n_heads, seg), lambda i: (layer, *omap(i)))],
        [jax.ShapeDtypeStruct((B * S, LANES), F32), jax.ShapeDtypeStruct((B, n_heads, S), F32),
         jax.ShapeDtypeStruct((n_layers, B, n_heads, S), F32)],
        ("arbitrary",), dsts=[None, None, dst], scratch=[pltpu.VMEM((1, LANES), F32)])


def _row_cumsum_body(x_ref, o_ref):
    R, P = x_ref.shape
    r = lax.broadcasted_iota(jnp.int32, (LANES, LANES), 0)
    c = lax.broadcasted_iota(jnp.int32, (LANES, LANES), 1)
    tri = jnp.where(r <= c, 1.0, 0.0).astype(F32)
    carry = jnp.zeros((R, 1), F32)
    for s in range(P // LANES):
        y = jnp.dot(x_ref[:, s * LANES:(s + 1) * LANES], tri, precision=HIGHEST, preferred_element_type=F32) + carry
        o_ref[:, s * LANES:(s + 1) * LANES] = y
        carry = y[:, LANES - 1:LANES]


def _row_cumsum(x, row0, n_rows):
    P = x.shape[1]
    tr = _tile(n_rows, 256)
    assert row0 % tr == 0
    i0 = row0 // tr
    return _call(
        _row_cumsum_body, (n_rows // tr,),
        [pl.BlockSpec((tr, P), lambda i: (i + i0, 0))], [x],
        pl.BlockSpec((tr, P), lambda i: (i, 0)),
        jax.ShapeDtypeStruct((n_rows, P), F32), ("parallel",))


def _fox_prompt_body(q_ref, k_ref, v_ref, l_ref, o_ref, m_ref, d_ref, acc_ref, s0_ref, s1_ref, *, scale, tq, sub):
    qi = pl.program_id(2)
    c1 = scale * LOG2E
    tk = tq // 2
    n_sub = tq // sub
    every = tuple(range(n_sub))
    m_ref[...] = jnp.full_like(m_ref, MASK_VALUE)
    d_ref[...] = jnp.zeros_like(d_ref)
    acc_ref[...] = jnp.zeros_like(acc_ref)
    l_here = l_ref[0, 0, :, pl.ds(pl.multiple_of(qi * tq, tq), LANES)][:, 0:1]

    def scores(ki, s_ref, subs):
        k0 = pl.multiple_of(ki * tk, tk)
        k = k_ref[0, 0, pl.ds(k0, tk), :]
        bias = (l_here - l_ref[0, 0, :, pl.ds(k0, tk)]) * LOG2E
        for u in subs:
            rows = slice(u * sub, (u + 1) * sub)
            s_ref[rows] = _dot_nt(q_ref[0, 0, rows, :], k) * c1 + bias

    def update(ki, s_ref, subs, diagonal=None):
        k0 = pl.multiple_of(ki * tk, tk)
        v = v_ref[0, 0, pl.ds(k0, tk), :]
        for u in subs:
            rows = slice(u * sub, (u + 1) * sub)
            s = s_ref[rows]
            if diagonal is not None and u * sub < (diagonal + 1) * tk - 1:
                r = lax.broadcasted_iota(jnp.int32, s.shape, 0) + u * sub
                c = lax.broadcasted_iota(jnp.int32, s.shape, 1) + diagonal * tk
                s = jnp.where(c <= r, s, MASK_VALUE)
            m_prev = m_ref[rows]
            m_new = jnp.maximum(m_prev, jnp.max(s, axis=-1, keepdims=True))
            alpha = jnp.exp2(m_prev - m_new)
            p = jnp.exp2(s - m_new)
            d_ref[rows] = alpha * d_ref[rows] + jnp.sum(p, axis=-1, keepdims=True)
            acc_ref[rows] = alpha * acc_ref[rows] + _dot(p.astype(BF16), v)
            m_ref[rows] = m_new

    def two_blocks(t, carry):
        scores(2 * t + 1, s1_ref, every)
        update(2 * t, s0_ref, every)
        scores(2 * t + 2, s0_ref, every)
        update(2 * t + 1, s1_ref, every)
        return carry

    scores(0, s0_ref, every)
    lax.fori_loop(0, qi, two_blocks, 0)
    late = tuple(u for u in every if (u + 1) * sub > tk)
    scores(2 * qi + 1, s1_ref, late)
    update(2 * qi, s0_ref, every, diagonal=0)
    update(2 * qi + 1, s1_ref, late, diagonal=1)
    o_ref[...] = (acc_ref[...] / d_ref[...]).astype(o_ref.dtype)


def _fox_prompt(q, k, v, l_row, out_rows):
    B, H, S, D = q.shape
    tq = _tile(S, 1024)
    sub = min(256, tq // 2)
    assert tq % (2 * LANES) == 0 and (tq // 2) % sub == 0
    nq = S // tq
    full = pl.BlockSpec((1, 1, S, D), lambda b, h, qi: (b, h, 0, 0))
    return _call(
        functools.partial(_fox_prompt_body, scale=D ** -0.5, tq=tq, sub=sub),
        (B, H, nq),
        [pl.BlockSpec((1, 1, tq, D), lambda b, h, qi: (b, h, qi, 0)), full, full,
         pl.BlockSpec((1, 1, 1, S), lambda b, h, qi: (b, h, 0, 0))],
        [q, k, v, l_row],
        pl.BlockSpec((tq, D), lambda b, h, qi: (b * nq + qi, h)),
        jax.ShapeDtypeStruct((out_rows, H * D), BF16),
        ("parallel", "parallel", "arbitrary"),
        scratch=[pltpu.VMEM((tq, 1), F32), pltpu.VMEM((tq, 1), F32), pltpu.VMEM((tq, D), F32),
                 pltpu.VMEM((tq, tq // 2), F32), pltpu.VMEM((tq, tq // 2), F32)])


def _head_column(l_tok, h):
    lane = lax.broadcasted_iota(jnp.int32, l_tok.shape, 1)
    return jnp.sum(jnp.where(lane == h, l_tok, 0.0), axis=-1, keepdims=True)


def _fox_sample_body(q_ref, *refs, scale, hps, n_split):
    ck_refs, cv_refs = refs[:n_split], refs[n_split:2 * n_split]
    kn_ref, vn_ref, lq_ref, lc_ref, ln_ref, o_ref = refs[2 * n_split:]
    hg = pl.program_id(1)
    pn = ck_refs[0].shape[-2]
    D = q_ref.shape[-1]
    for hh in range(hps):
        q = q_ref[0, hh]
        lq = _head_column(lq_ref[...], hg * hps + hh)
        s_c = [_dot_nt(q, ck_refs[n][0, hh].astype(BF16)) * scale + (lq - lc_ref[0, hh, :, n * pn:(n + 1) * pn])
               for n in range(n_split)]
        s_n = _dot_nt(q, kn_ref[0, hh]) * scale + (lq - ln_ref[0, hh])
        r = lax.broadcasted_iota(jnp.int32, s_n.shape, 0)
        c = lax.broadcasted_iota(jnp.int32, s_n.shape, 1)
        s_n = jnp.where(c <= r, s_n, MASK_VALUE)
        m = jnp.max(s_n, axis=-1, keepdims=True)
        for s in s_c:
            m = jnp.maximum(m, jnp.max(s, axis=-1, keepdims=True))
        p_n = jnp.exp(s_n - m)
        l = jnp.sum(p_n, axis=-1, keepdims=True)
        o = _dot(p_n.astype(BF16), vn_ref[0, hh])
        for n in range(n_split):
            p = jnp.exp(s_c[n] - m)
            l = l + jnp.sum(p, axis=-1, keepdims=True)
            o = o + _dot(p.astype(BF16), cv_refs[n][0, hh].astype(BF16))
        o_ref[:, hh * D:(hh + 1) * D] = (o / l).astype(o_ref.dtype)


def _fox_sample(q, cache_k, cache_v, layer, kn, vn, ln_tok, lc_row, ln_row, out_row0, dst):
    B, H, T, D = q.shape
    P = cache_k.shape[3]
    assert out_row0 % T == 0
    r0 = out_row0 // T
    hps = math.gcd(H, FOX_SAMPLE_HEADS_PER_STEP)
    n_split = FOX_SAMPLE_KEY_SPLITS if P % (FOX_SAMPLE_KEY_SPLITS * LANES) == 0 else 1
    pn = P // n_split
    new = pl.BlockSpec((1, hps, T, D), lambda b, h: (b, h, 0, 0))
    old = [pl.BlockSpec((None, 1, hps, pn, D), lambda b, h, n=n: (layer, b, h, n, 0)) for n in range(n_split)]
    return _call(
        functools.partial(_fox_sample_body, scale=D ** -0.5, hps=hps, n_split=n_split),
        (B, H // hps),
        [new, *old, *old, new, new,
         pl.BlockSpec((T, LANES), lambda b, h: (b, 0)),
         pl.BlockSpec((1, hps, 1, P), lambda b, h: (b, h, 0, 0)),
         pl.BlockSpec((1, hps, 1, T), lambda b, h: (b, h, 0, 0))],
        [q, *([cache_k] * n_split), *([cache_v] * n_split), kn, vn, ln_tok, lc_row, ln_row],
        pl.BlockSpec((T, hps * D), lambda b, h: (b + r0, h)),
        jax.ShapeDtypeStruct(dst.shape, dst.dtype),
        ("parallel", "parallel"), dsts=[dst])


def _hgrn_body(*refs, C, hpb, has_init):
    if has_init:
        q_ref, k_ref, g_ref, i_ref, gate_ref, gn_ref, s0_ref, o_ref, sf_ref, st_ref = refs
    else:
        q_ref, k_ref, g_ref, i_ref, gate_ref, gn_ref, o_ref, sf_ref, st_ref = refs
        s0_ref = None
    c = pl.program_id(2)

    @pl.when(c == 0)
    def _():
        for hh in range(hpb):
            st_ref[hh] = s0_ref[0, hh].T if has_init else jnp.zeros((LANES, LANES), F32)

    row = lax.broadcasted_iota(jnp.int32, (C, C), 0)
    col = lax.broadcasted_iota(jnp.int32, (C, C), 1)
    tri = jnp.where(row >= col, 1.0, 0.0).astype(F32)
    trow = lax.broadcasted_iota(jnp.int32, (C, LANES), 0)
    split = jnp.where(row > col, row ^ col, 0)

    for hh in range(hpb):
        sl = slice(hh * LANES, (hh + 1) * LANES)
        q, k, g, v = q_ref[:, sl], k_ref[:, sl], g_ref[:, sl], i_ref[:, sl].astype(BF16)
        G = jnp.dot(tri, g, precision=HIGHEST, preferred_element_type=F32)
        st = st_ref[hh]
        o = _dot_nt((q * jnp.exp(G)).astype(BF16), st.astype(BF16))

        A = jnp.where(row == col, _dot_nt(q.astype(BF16), k.astype(BF16)), 0.0)
        yq, yk = G - g, G
        w, log_w = 1, 0
        while w < C:
            qt = (q * jnp.exp(G - yq)).astype(BF16)
            kt = (k * jnp.exp(yk - G)).astype(BF16)
            A = A + jnp.where((split >> log_w) == 1, _dot_nt(qt, kt), 0.0)
            upper = (trow & w) != 0
            yq = jnp.where(upper, pltpu.roll(yq, w, 0), yq)
            yk = jnp.where(upper, yk, pltpu.roll(yk, C - w, 0))
            w, log_w = 2 * w, log_w + 1
        o = o + _dot(A.astype(BF16), v)

        g_last = G[C - 1:C, :]
        ks = (k * jnp.exp(g_last - G)).astype(BF16)
        st_new = st * jnp.exp(g_last) + _dot_tn(v, ks)
        st_ref[hh] = st_new
        o_ref[:, sl] = (_rms(o, gn_ref[...]) * gate_ref[:, sl]).astype(o_ref.dtype)

    @pl.when(c == pl.num_programs(2) - 1)
    def _():
        for hh in range(hpb):
            sf_ref[0, hh] = st_ref[hh].T


def _hgrn(main, k, gnorm, s0, row0, B, S, H, layer, n_layers, o_dst, s_dst):
    T = k.shape[0]
    C = min(S, LANES)
    nc = S // C
    assert row0 % C == 0
    r0 = row0 // C
    hpb = math.gcd(H, HGRN_HEADS_PER_STEP)
    nhb = H // hpb
    tok = pl.BlockSpec((C, hpb * LANES), lambda b, h, c: (r0 + b * nc + c, h))
    seg = lambda n: pl.BlockSpec((C, hpb * LANES), lambda b, h, c: (r0 + b * nc + c, n * nhb + h))
    state = pl.BlockSpec((None, 1, hpb, LANES, LANES), lambda b, h, c: (layer, b, h, 0, 0))
    in_specs = [seg(0), tok, seg(1), seg(2), seg(3), pl.BlockSpec((1, LANES), lambda b, h, c: (0, 0))]
    args = [main, k, main, main, main, gnorm.reshape(1, LANES)]
    if s0 is not None:
        in_specs.append(state)
        args.append(s0)
    return _call(
        functools.partial(_hgrn_body, C=C, hpb=hpb, has_init=s0 is not None),
        (B, H // hpb, nc), in_specs, args,
        [tok, state],
        [jax.ShapeDtypeStruct((T, H * LANES), BF16), jax.ShapeDtypeStruct((n_layers, B, H, LANES, LANES), F32)],
        ("parallel", "parallel", "arbitrary"), dsts=[o_dst, s_dst],
        scratch=[pltpu.VMEM((hpb, LANES, LANES), F32)])


def _cmix_body(u_ref, v_ref, lng_ref, lnb_ref, ws_ref, bs_ref, y_ref, *vout, groups):
    v = v_ref[...]
    mu = jnp.mean(v, axis=-1, keepdims=True)
    xc = v - mu
    var = jnp.mean(xc * xc, axis=-1, keepdims=True)
    vn = xc * lax.rsqrt(var + EPS) * lng_ref[...] + lnb_ref[...]
    if vout:
        vout[0][...] = vn
    vb = vn.astype(BF16)
    Lc, Cd = v.shape
    cg = Cd // groups
    r = lax.broadcasted_iota(jnp.int32, (Lc, Lc), 0)
    c = lax.broadcasted_iota(jnp.int32, (Lc, Lc), 1)
    for gi in range(groups):
        sl = slice(gi * cg, (gi + 1) * cg)
        w = jnp.where(c <= r, ws_ref[gi], 0.0).astype(BF16)
        mixed = _dot(w, vb[:, sl]) + bs_ref[:, sl]
        y_ref[:, sl] = (u_ref[:, sl] * mixed).astype(y_ref.dtype)


def _cmix(uv, ln_g, ln_b, ws, bs, row0, n_rows, Lc, y_dst, v_out):
    T, Cd = uv.shape[0], uv.shape[1] // 2
    G = ws.shape[0]
    assert row0 % Lc == 0
    r0 = row0 // Lc
    tok = pl.BlockSpec((Lc, Cd), lambda i: (i + r0, 0))
    tok_v = pl.BlockSpec((Lc, Cd), lambda i: (i + r0, 1))
    vec = pl.BlockSpec((1, Cd), lambda i: (0, 0))
    out_specs, out_shapes, dsts = [tok], [jax.ShapeDtypeStruct((T, Cd), BF16)], [y_dst]
    if v_out:
        layer, n_layers, dst = v_out
        out_specs.append(pl.BlockSpec((None, Lc, Cd), lambda i: (layer, i, 0)))
        out_shapes.append(jax.ShapeDtypeStruct((n_layers, n_rows, Cd), F32))
        dsts.append(dst)
    return _call(
        functools.partial(_cmix_body, groups=G), (n_rows // Lc,),
        [tok, tok_v, vec, vec, pl.BlockSpec((G, Lc, Lc), lambda i: (0, 0, 0)),
         pl.BlockSpec((Lc, Cd), lambda i: (0, 0))],
        [uv, uv, ln_g.reshape(1, Cd), ln_b.reshape(1, Cd), ws, bs],
        out_specs, out_shapes, ("parallel",), dsts=dsts)


def _epi_id(z):
    return (z,)


def _epi_silu(z):
    return (z * jax.nn.sigmoid(z),)


def _epi_gelu(z):
    return (jax.nn.gelu(z),)


def _epi_logsig(z, b):
    return (jax.nn.log_sigmoid(z + b),)


def _epi_hgrn_forget(z, lb_all, *, layer):
    rows = [lb_all[n:n + 1] for n in range(lb_all.shape[0])]
    top = functools.reduce(jnp.maximum, rows)
    e = [jnp.exp(r - top) for r in rows]
    total = functools.reduce(jnp.add, e)
    sm = [a / total for a in e]
    lb = functools.reduce(jnp.add, sm[:layer + 1]) - sm[0]
    f = lb + (1.0 - lb) * jax.nn.sigmoid(z)
    g = jnp.log(jnp.maximum(f, TINY))
    k = (1.0 - lb) * jax.nn.sigmoid(-z)
    return g, k


def kernel(x_prompt, x_sample, cache_k, cache_v, cache_logf, state_hgrn, norm_ffn1, ffn1_gate, ffn1_up, ffn1_down, norm_mix, ab_w_in, ab_b_f, hgrn_lb, hgrn_gnorm, ab_w_out, c_w_in, c_ln_g, c_ln_b, c_w_s, c_b_s, c_w_out, norm_ffn2, ffn2_gate, ffn2_up, ffn2_down, norm_final):
    B, S, D = x_prompt.shape
    DB, DT, _ = x_sample.shape
    depth = norm_ffn1.shape[0]
    NAB, NC = ab_w_in.shape[0], c_w_in.shape[0]
    HA, P, HD = cache_k.shape[2], cache_k.shape[3], cache_k.shape[4]
    HB, DK, DV = state_hgrn.shape[2], state_hgrn.shape[3], state_hgrn.shape[4]
    assert HD == LANES and DK == LANES and DV == LANES and HA <= 8
    WA, WBK, WBV = HA * HD, HB * DK, HB * DV
    G, CL = c_w_s.shape[1], c_w_s.shape[2]
    CD = c_w_out.shape[1]
    Tp, Ts = B * S, DB * DT
    T = Tp + Ts
    bf = lambda a: a.astype(BF16)

    pk = pv = plf = ps = sk = sv = slf = ss = scv = None
    x = None
    for l in range(depth):
        j = l // 2
        w1 = (norm_ffn1[l], ffn1_gate, ffn1_up, ffn1_down, l, norm_mix[l])
        if l == 0:
            x, h = _ffn(x_prompt.reshape(Tp, D), *w1, out_rows=T)
            x, h = _ffn(x_sample.reshape(Ts, D), *w1, out_rows=T, out_row0=Tp, dsts=(x, h))
        else:
            x, h = _ffn(x, *w1)
        if l % 2 == 0:
            w_in = ab_w_in[j]
            o1 = 3 * WA + HA
            w_q, w_k, w_v = (bf(w_in[:, n * WA:(n + 1) * WA]) for n in range(3))
            w_f = bf(jnp.pad(w_in[:, 3 * WA:o1], ((0, 0), (0, LANES - HA))))
            w_b = bf(w_in[:, o1:])
            b_f = jnp.pad(ab_b_f[j], (0, LANES - HA)).reshape(1, LANES)

            (lf_tok,) = _proj(h, w_f, [b_f], _epi_logsig, [F32])
            _, lp_row, plf = _cumsum(lf_tok, jnp.zeros((Tp, LANES), F32), 0, B, S, HA, j, NAB, plf)
            lc_row = _row_cumsum(cache_logf.reshape(NAB * DB * HA, P), j * DB * HA, DB * HA).reshape(DB, HA, P)
            base = jnp.pad(lc_row[:, :, P - 1], ((0, 0), (0, LANES - HA)))
            base = jnp.broadcast_to(base[:, None, :], (DB, DT, LANES)).reshape(Ts, LANES)
            ls_tok, ls_row, slf = _cumsum(lf_tok, base, Tp, DB, DT, HA, j, NAB, slf)

            (qp,) = _proj_heads(h, w_q, 0, B, S, [(BF16, 0, 0, None)])
            pk, kpb = _proj_heads(h, w_k, 0, B, S, [(F32, j, NAB, pk), (BF16, 0, 0, None)])
            pv, vpb = _proj_heads(h, w_v, 0, B, S, [(F32, j, NAB, pv), (BF16, 0, 0, None)])
            (qs,) = _proj_heads(h, w_q, Tp, DB, DT, [(BF16, 0, 0, None)])
            sk, ksb = _proj_heads(h, w_k, Tp, DB, DT, [(F32, j, NAB, sk), (BF16, 0, 0, None)])
            sv, vsb = _proj_heads(h, w_v, Tp, DB, DT, [(F32, j, NAB, sv), (BF16, 0, 0, None)])
            oa = _fox_prompt(qp, kpb, vpb, lp_row.reshape(B, HA, 1, S), T)
            oa = _fox_sample(qs, cache_k, cache_v, j, ksb, vsb, ls_tok,
                             lc_row.reshape(DB, HA, 1, P), ls_row.reshape(DB, HA, 1, DT), Tp, oa)

            assert WBK == WBV
            zb, kb = _proj_hgrn(h, w_b, hgrn_lb, j)
            ob, ps = _hgrn(zb, kb, hgrn_gnorm[j], None, 0, B, S, HB, j, NAB, None, ps)
            ob, ss = _hgrn(zb, kb, hgrn_gnorm[j], state_hgrn, Tp, DB, DT, HB, j, NAB, ob, ss)

            w_out = bf(ab_w_out[j])
            x = _mm_res([oa, ob], [w_out[:WA], w_out[WA:]], x)
        else:
            (uv,) = _proj(h, bf(c_w_in[j]), [], _epi_gelu, [F32])
            y = None
            for row0, n_rows, seq in ((0, Tp, S), (Tp, Ts, DT)):
                Lc = min(seq, CL)
                ws = c_w_s[j][:, :Lc, :Lc]
                bs = jnp.repeat(c_b_s[j][:, :Lc].T, CD // G, axis=1)
                if row0 == 0:
                    (y,) = _cmix(uv, c_ln_g[j], c_ln_b[j], ws, bs, row0, n_rows, Lc, y, None)
                else:
                    y, scv = _cmix(uv, c_ln_g[j], c_ln_b[j], ws, bs, row0, n_rows, Lc, y, (j, NC, scv))
            x = _mm_res([y], [bf(c_w_out[j])], x)
        x = _ffn(x, norm_ffn2[l], ffn2_gate, ffn2_up, ffn2_down, l)

    y_p = _norm(x, norm_final, F32, 0, Tp)
    y_s = _norm(x, norm_final, F32, Tp, Ts)
    return (y_p.reshape(B, S, D), y_s.reshape(DB, DT, D), pk, pv, plf, ps,
            sk, sv, slf, ss, scv.reshape(NC, DB, DT, CD))
```

```python
import functools
import math

import jax
import jax.numpy as jnp
from jax import lax
from jax.experimental import pallas as pl
from jax.experimental.pallas import tpu as pltpu

F32 = jnp.float32
BF16 = jnp.bfloat16
EPS = 1e-6
TINY = 1e-30
MASK_VALUE = -1e30
LOG2E = math.log2(math.e)
HIGHEST = lax.Precision.HIGHEST

LANES = 128
MIB = 1 << 20
VMEM_LIMIT = 60 * MIB
FFN_TOKEN_TILE = 1024
FFN_FF_TILE = 256
PROJ_COL_TILE = 512
HGRN_HEADS_PER_STEP = 8
FOX_SAMPLE_HEADS_PER_STEP = 2
FOX_SAMPLE_KEY_SPLITS = 2


def _tile(n, target):
    if n <= target:
        return n
    for t in range(target, 7, -1):
        if n % t == 0 and t % 8 == 0:
            return t
    return n


def _call(body, grid, in_specs, args, out_specs, out_shapes, sem, dsts=None, scratch=()):
    in_specs, args = list(in_specs), list(args)
    n_in = len(args)
    aliases = {}
    for k, d in enumerate(dsts or ()):
        if d is not None:
            aliases[len(args)] = k
            in_specs.append(pl.BlockSpec(memory_space=pl.ANY))
            args.append(d)
    n_all = len(args)

    def wrapped(*refs):
        body(*refs[:n_in], *refs[n_all:])

    return pl.pallas_call(
        wrapped, grid=grid, in_specs=in_specs, out_specs=out_specs, out_shape=out_shapes,
        input_output_aliases=aliases, scratch_shapes=list(scratch),
        compiler_params=pltpu.CompilerParams(dimension_semantics=sem, vmem_limit_bytes=VMEM_LIMIT),
    )(*args)


def _dot(a, b):
    return jnp.dot(a, b, preferred_element_type=F32)


def _dot_nt(a, b):
    return lax.dot_general(a, b, (((1,), (1,)), ((), ())), preferred_element_type=F32)


def _dot_tn(a, b):
    return lax.dot_general(a, b, (((0,), (0,)), ((), ())), preferred_element_type=F32)


def _rms(x, g):
    return x * lax.rsqrt(jnp.mean(x * x, axis=-1, keepdims=True) + EPS) * g


def _ffn_body(x_ref, g_ref, wg_ref, wu_ref, wd_ref, *refs, next_norm):
    if next_norm:
        g2_ref, o_ref, h2_ref, h_ref = refs
    else:
        o_ref, h_ref = refs

    @pl.when(pl.program_id(1) == 0)
    def _():
        x = x_ref[...]
        h_ref[...] = _rms(x, g_ref[...]).astype(BF16)
        o_ref[...] = x

    h = h_ref[...]
    a = _dot(h, wg_ref[...].astype(BF16))
    b = _dot(h, wu_ref[...].astype(BF16))
    act = (0.5 * a * jax.nn.sigmoid(a) * b).astype(BF16)
    o_ref[...] += _dot(act, wd_ref[...].astype(BF16))

    if next_norm:
        @pl.when(pl.program_id(1) == pl.num_programs(1) - 1)
        def _():
            h2_ref[...] = _rms(o_ref[...], g2_ref[...]).astype(h2_ref.dtype)


def _ffn(x, g, wg, wu, wd, layer, g_next=None, out_rows=None, out_row0=0, dsts=(None, None)):
    T, D = x.shape
    F = wg.shape[2]
    tm = _tile(T, FFN_TOKEN_TILE)
    tf = _tile(F, FFN_FF_TILE)
    out_rows = out_rows or T
    assert out_row0 % tm == 0
    i0 = out_row0 // tm
    vec = pl.BlockSpec((1, D), lambda i, j: (0, 0))
    tok = pl.BlockSpec((tm, D), lambda i, j: (i + i0, 0))
    in_specs = [pl.BlockSpec((tm, D), lambda i, j: (i, 0)), vec,
                pl.BlockSpec((None, D, tf), lambda i, j: (layer, 0, j)),
                pl.BlockSpec((None, D, tf), lambda i, j: (layer, 0, j)),
                pl.BlockSpec((None, tf, D), lambda i, j: (layer, j, 0))]
    args = [x, g.reshape(1, D), wg, wu, wd]
    out_specs, out_shapes = [tok], [jax.ShapeDtypeStruct((out_rows, D), F32)]
    if g_next is not None:
        in_specs.append(vec)
        args.append(g_next.reshape(1, D))
        out_specs.append(tok)
        out_shapes.append(jax.ShapeDtypeStruct((out_rows, D), BF16))
    outs = _call(
        functools.partial(_ffn_body, next_norm=g_next is not None), (T // tm, F // tf),
        in_specs, args, out_specs, out_shapes,
        ("parallel", "arbitrary"), dsts=list(dsts[:len(out_specs)]),
        scratch=[pltpu.VMEM((tm, D), BF16)])
    return outs if g_next is not None else outs[0]


def _norm_body(x_ref, g_ref, o_ref):
    o_ref[...] = _rms(x_ref[...], g_ref[...]).astype(o_ref.dtype)


def _norm(x, g, dtype, row0=0, n_rows=None):
    D = x.shape[1]
    n_rows = n_rows or x.shape[0]
    tm = _tile(n_rows, 512)
    assert row0 % tm == 0
    i0 = row0 // tm
    return _call(
        _norm_body, (n_rows // tm,),
        [pl.BlockSpec((tm, D), lambda i: (i + i0, 0)), pl.BlockSpec((1, D), lambda i: (0, 0))],
        [x, g.reshape(1, D)],
        pl.BlockSpec((tm, D), lambda i: (i, 0)),
        jax.ShapeDtypeStruct((n_rows, D), dtype), ("parallel",))


def _resident(shape):
    return pl.BlockSpec(shape, lambda i: (0,) * len(shape), pipeline_mode=pl.Buffered(1))


def _col_tiles(n):
    tn = _tile(n, PROJ_COL_TILE)
    return [slice(c * tn, (c + 1) * tn) for c in range(n // tn)]


def _proj_body(h_ref, w_ref, *refs, epilogue, n_aux):
    aux, outs = refs[:n_aux], refs[n_aux:]
    h = h_ref[...]
    for cols in _col_tiles(w_ref.shape[1]):
        vals = epilogue(_dot(h, w_ref[:, cols]), *[a[:, cols] for a in aux])
        for o_ref, v in zip(outs, vals):
            o_ref[:, cols] = v.astype(o_ref.dtype)


def _proj(h, w, aux, epilogue, out_dtypes, tm_target):
    T, K = h.shape
    N = w.shape[1]
    tm = _tile(T, tm_target)
    return _call(
        functools.partial(_proj_body, epilogue=epilogue, n_aux=len(aux)),
        (T // tm,),
        [pl.BlockSpec((tm, K), lambda i: (i, 0)), _resident((K, N))] + [_resident(a.shape) for a in aux],
        [h, w, *aux],
        [pl.BlockSpec((tm, N), lambda i: (i, 0)) for _ in out_dtypes],
        [jax.ShapeDtypeStruct((T, N), d) for d in out_dtypes],
        ("parallel",))


def _proj_hgrn_body(h_ref, w_ref, lb_ref, main_ref, k_ref, *, layer):
    W = k_ref.shape[1]
    h = h_ref[...]
    for seg in range(4):
        for kc in _col_tiles(W):
            cols = slice(seg * W + kc.start, seg * W + kc.stop)
            z = _dot(h, w_ref[:, cols])
            if seg == 1:
                g, k = _epi_hgrn_forget(z, lb_ref[:, kc], layer=layer)
                main_ref[:, cols] = g
                k_ref[:, kc] = k
            else:
                main_ref[:, cols] = z if seg == 2 else z * jax.nn.sigmoid(z)


def _proj_hgrn(h, w, lb_all, layer):
    T, K = h.shape
    W = w.shape[1] // 4
    tm = _tile(T, 512)
    return _call(
        functools.partial(_proj_hgrn_body, layer=layer),
        (T // tm,),
        [pl.BlockSpec((tm, K), lambda i: (i, 0)), _resident((K, 4 * W)), _resident(lb_all.shape)],
        [h, w, lb_all],
        [pl.BlockSpec((tm, 4 * W), lambda i: (i, 0)), pl.BlockSpec((tm, W), lambda i: (i, 0))],
        [jax.ShapeDtypeStruct((T, 4 * W), F32), jax.ShapeDtypeStruct((T, W), F32)],
        ("parallel",))


def _qkv_body(h_ref, w_ref, q_ref, k_ref, kb_ref, v_ref, vb_ref, *, nb, ts):
    H = q_ref.shape[1]
    h = h_ref[...]
    targets = ((q_ref,), (k_ref, kb_ref), (v_ref, vb_ref))
    for cols in _col_tiles(w_ref.shape[1]):
        z = _dot(h, w_ref[:, cols])
        for n in range((cols.stop - cols.start) // LANES):
            head = cols.start // LANES + n
            zz = z[:, n * LANES:(n + 1) * LANES].reshape(nb, ts, LANES)
            for o_ref in targets[head // H]:
                o_ref[:, head % H] = zz.astype(o_ref.dtype)


def _proj_qkv(h, w, row0, B, S, layer, n_layers, k_dst, v_dst):
    K = h.shape[1]
    H = w.shape[1] // (3 * LANES)
    if S >= 512:
        ts, nb = _tile(S, 512), 1
    else:
        nb = _tile(B, max(1, 512 // S))
        ts = S
    tm = nb * ts
    spt = S // ts
    assert row0 % tm == 0
    i0 = row0 // tm
    if nb == 1:
        omap = lambda i: (i // spt, 0, i % spt, 0)
    else:
        omap = lambda i: (i, 0, 0, 0)
    plain = pl.BlockSpec((nb, H, ts, LANES), omap)
    slab = pl.BlockSpec((None, nb, H, ts, LANES), lambda i: (layer, *omap(i)))
    bf_shape = jax.ShapeDtypeStruct((B, H, S, LANES), BF16)
    slab_shape = jax.ShapeDtypeStruct((n_layers, B, H, S, LANES), F32)
    return _call(
        functools.partial(_qkv_body, nb=nb, ts=ts),
        (B * S // tm,),
        [pl.BlockSpec((tm, K), lambda i: (i + i0, 0)), _resident(w.shape)],
        [h, w],
        [plain, slab, plain, slab, plain],
        [bf_shape, slab_shape, bf_shape, slab_shape, bf_shape],
        ("parallel",), dsts=[None, k_dst, None, v_dst, None])


def _mm_res_body(*refs, n):
    lhs = [r[...] for r in refs[:n]]
    ws = refs[n:2 * n]
    x_ref, o_ref = refs[2 * n], refs[2 * n + 1]
    for cols in _col_tiles(x_ref.shape[1]):
        z = _dot(lhs[0], ws[0][:, cols])
        for a, w_ref in zip(lhs[1:], ws[1:]):
            z += _dot(a, w_ref[:, cols])
        o_ref[:, cols] = x_ref[:, cols] + z


def _mm_res(lhs, ws, x):
    T, N = x.shape
    n = len(lhs)
    tm = _tile(T, 512)
    tok = pl.BlockSpec((tm, N), lambda i: (i, 0))
    return _call(
        functools.partial(_mm_res_body, n=n), (T // tm,),
        ([pl.BlockSpec((tm, a.shape[1]), lambda i: (i, 0)) for a in lhs]
         + [_resident(w.shape) for w in ws] + [tok]),
        [*lhs, *ws, x],
        tok, jax.ShapeDtypeStruct((T, N), F32), ("parallel",))


def _cumsum_body(lf_ref, base_ref, ltok_ref, lrow_ref, lfrow_ref, carry_ref, *, ts, seg, tiles_per_seq, n_heads):
    lf = lf_ref[...]
    r = lax.broadcasted_iota(jnp.int32, (ts, ts), 0)
    c = lax.broadcasted_iota(jnp.int32, (ts, ts), 1)
    tri = jnp.where(r >= c, 1.0, 0.0).astype(F32)
    if seg < ts:
        tri = jnp.where(r // seg == c // seg, tri, 0.0)
    L = jnp.dot(tri, lf, precision=HIGHEST, preferred_element_type=F32)
    if tiles_per_seq:
        @pl.when(pl.program_id(0) % tiles_per_seq == 0)
        def _():
            carry_ref[...] = jnp.zeros_like(carry_ref)
        L = L + carry_ref[...]
        carry_ref[...] = L[ts - 1:ts, :]
    else:
        L = L + base_ref[...]
    ltok_ref[...] = L
    Lt = L.T[:n_heads]
    lft = lf.T[:n_heads]
    for s in range(ts // seg):
        lrow_ref[s] = Lt[:, s * seg:(s + 1) * seg]
        lfrow_ref[s] = lft[:, s * seg:(s + 1) * seg]


def _cumsum(lf_tok, base, row0, B, S, n_heads, layer, n_layers, dst):
    if S >= LANES:
        ts = _tile(S, 512)
        seg, tps, nseg = ts, S // ts, 1
    else:
        ts, seg, tps = LANES, S, 0
        nseg = ts // seg
    assert row0 % ts == 0 and (B * S) % ts == 0
    i0 = row0 // ts
    if tps:
        omap = lambda i: (i // tps, 0, i % tps)
    else:
        omap = lambda i: (i, 0, 0)
    return _call(
        functools.partial(_cumsum_body, ts=ts, seg=seg, tiles_per_seq=tps, n_heads=n_heads),
        (B * S // ts,),
        [pl.BlockSpec((ts, LANES), lambda i: (i + i0, 0)), pl.BlockSpec((ts, LANES), lambda i: (i, 0))],
        [lf_tok, base],
        [pl.BlockSpec((ts, LANES), lambda i: (i, 0)),
         pl.BlockSpec((nseg, n_heads, seg), omap),
         pl.BlockSpec((None, nseg, n_heads, seg), lambda i: (layer, *omap(i)))],
        [jax.ShapeDtypeStruct((B * S, LANES), F32), jax.ShapeDtypeStruct((B, n_heads, S), F32),
         jax.ShapeDtypeStruct((n_layers, B, n_heads, S), F32)],
        ("arbitrary",), dsts=[None, None, dst], scratch=[pltpu.VMEM((1, LANES), F32)])


def _row_cumsum_body(x_ref, o_ref):
    R, P = x_ref.shape
    r = lax.broadcasted_iota(jnp.int32, (LANES, LANES), 0)
    c = lax.broadcasted_iota(jnp.int32, (LANES, LANES), 1)
    tri = jnp.where(r <= c, 1.0, 0.0).astype(F32)
    carry = jnp.zeros((R, 1), F32)
    for s in range(P // LANES):
        y = jnp.dot(x_ref[:, s * LANES:(s + 1) * LANES], tri, precision=HIGHEST, preferred_element_type=F32) + carry
        o_ref[:, s * LANES:(s + 1) * LANES] = y
        carry = y[:, LANES - 1:LANES]


def _row_cumsum(x, row0, n_rows):
    P = x.shape[1]
    tr = _tile(n_rows, 256)
    assert row0 % tr == 0
    i0 = row0 // tr
    return _call(
        _row_cumsum_body, (n_rows // tr,),
        [pl.BlockSpec((tr, P), lambda i: (i + i0, 0))], [x],
        pl.BlockSpec((tr, P), lambda i: (i, 0)),
        jax.ShapeDtypeStruct((n_rows, P), F32), ("parallel",))


def _fox_prompt_body(q_ref, k_ref, v_ref, l_ref, o_ref, m_ref, d_ref, acc_ref, s0_ref, s1_ref, *, scale, tq, sub):
    qi = pl.program_id(2)
    c1 = scale * LOG2E
    tk = tq // 2
    n_sub = tq // sub
    every = tuple(range(n_sub))
    m_ref[...] = jnp.full_like(m_ref, MASK_VALUE)
    d_ref[...] = jnp.zeros_like(d_ref)
    acc_ref[...] = jnp.zeros_like(acc_ref)
    l_here = l_ref[0, 0, :, pl.ds(pl.multiple_of(qi * tq, tq), LANES)][:, 0:1]

    def scores(ki, s_ref, subs):
        k0 = pl.multiple_of(ki * tk, tk)
        k = k_ref[0, 0, pl.ds(k0, tk), :]
        bias = (l_here - l_ref[0, 0, :, pl.ds(k0, tk)]) * LOG2E
        for u in subs:
            rows = slice(u * sub, (u + 1) * sub)
            s_ref[rows] = _dot_nt(q_ref[0, 0, rows, :], k) * c1 + bias

    def update(ki, s_ref, subs, diagonal=None):
        k0 = pl.multiple_of(ki * tk, tk)
        v = v_ref[0, 0, pl.ds(k0, tk), :]
        for u in subs:
            rows = slice(u * sub, (u + 1) * sub)
            s = s_ref[rows]
            if diagonal is not None and u * sub < (diagonal + 1) * tk - 1:
                r = lax.broadcasted_iota(jnp.int32, s.shape, 0) + u * sub
                c = lax.broadcasted_iota(jnp.int32, s.shape, 1) + diagonal * tk
                s = jnp.where(c <= r, s, MASK_VALUE)
            m_prev = m_ref[rows]
            m_new = jnp.maximum(m_prev, jnp.max(s, axis=-1, keepdims=True))
            alpha = jnp.exp2(m_prev - m_new)
            p = jnp.exp2(s - m_new)
            d_ref[rows] = alpha * d_ref[rows] + jnp.sum(p, axis=-1, keepdims=True)
            acc_ref[rows] = alpha * acc_ref[rows] + _dot(p.astype(BF16), v)
            m_ref[rows] = m_new

    def two_blocks(t, carry):
        scores(2 * t + 1, s1_ref, every)
        update(2 * t, s0_ref, every)
        scores(2 * t + 2, s0_ref, every)
        update(2 * t + 1, s1_ref, every)
        return carry

    scores(0, s0_ref, every)
    lax.fori_loop(0, qi, two_blocks, 0)
    late = tuple(u for u in every if (u + 1) * sub > tk)
    scores(2 * qi + 1, s1_ref, late)
    update(2 * qi, s0_ref, every, diagonal=0)
    update(2 * qi + 1, s1_ref, late, diagonal=1)
    o_ref[...] = (acc_ref[...] / d_ref[...]).astype(o_ref.dtype)


def _fox_prompt(q, k, v, l_row, out_rows):
    B, H, S, D = q.shape
    tq = _tile(S, 1024)
    sub = min(256, tq // 2)
    assert tq % (2 * LANES) == 0 and (tq // 2) % sub == 0
    nq = S // tq
    full = pl.BlockSpec((1, 1, S, D), lambda b, h, qi: (b, h, 0, 0))
    return _call(
        functools.partial(_fox_prompt_body, scale=D ** -0.5, tq=tq, sub=sub),
        (B, H, nq),
        [pl.BlockSpec((1, 1, tq, D), lambda b, h, qi: (b, h, qi, 0)), full, full,
         pl.BlockSpec((1, 1, 1, S), lambda b, h, qi: (b, h, 0, 0))],
        [q, k, v, l_row],
        pl.BlockSpec((tq, D), lambda b, h, qi: (b * nq + qi, h)),
        jax.ShapeDtypeStruct((out_rows, H * D), BF16),
        ("parallel", "parallel", "arbitrary"),
        scratch=[pltpu.VMEM((tq, 1), F32), pltpu.VMEM((tq, 1), F32), pltpu.VMEM((tq, D), F32),
                 pltpu.VMEM((tq, tq // 2), F32), pltpu.VMEM((tq, tq // 2), F32)])


def _head_column(l_tok, h):
    lane = lax.broadcasted_iota(jnp.int32, l_tok.shape, 1)
    return jnp.sum(jnp.where(lane == h, l_tok, 0.0), axis=-1, keepdims=True)


def _fox_sample_body(q_ref, *refs, scale, hps, n_split):
    ck_refs, cv_refs = refs[:n_split], refs[n_split:2 * n_split]
    kn_ref, vn_ref, lq_ref, lc_ref, ln_ref, o_ref = refs[2 * n_split:]
    hg = pl.program_id(1)
    pn = ck_refs[0].shape[-2]
    D = q_ref.shape[-1]
    for hh in range(hps):
        q = q_ref[0, hh]
        lq = _head_column(lq_ref[...], hg * hps + hh)
        s_c = [_dot_nt(q, ck_refs[n][0, hh].astype(BF16)) * scale + (lq - lc_ref[0, hh, :, n * pn:(n + 1) * pn])
               for n in range(n_split)]
        s_n = _dot_nt(q, kn_ref[0, hh]) * scale + (lq - ln_ref[0, hh])
        r = lax.broadcasted_iota(jnp.int32, s_n.shape, 0)
        c = lax.broadcasted_iota(jnp.int32, s_n.shape, 1)
        s_n = jnp.where(c <= r, s_n, MASK_VALUE)
        m = jnp.max(s_n, axis=-1, keepdims=True)
        for s in s_c:
            m = jnp.maximum(m, jnp.max(s, axis=-1, keepdims=True))
        p_n = jnp.exp(s_n - m)
        l = jnp.sum(p_n, axis=-1, keepdims=True)
        o = _dot(p_n.astype(BF16), vn_ref[0, hh])
        for n in range(n_split):
            p = jnp.exp(s_c[n] - m)
            l = l + jnp.sum(p, axis=-1, keepdims=True)
            o = o + _dot(p.astype(BF16), cv_refs[n][0, hh].astype(BF16))
        o_ref[:, hh * D:(hh + 1) * D] = (o / l).astype(o_ref.dtype)


def _fox_sample(q, cache_k, cache_v, layer, kn, vn, ln_tok, lc_row, ln_row, out_row0, dst):
    B, H, T, D = q.shape
    P = cache_k.shape[3]
    assert out_row0 % T == 0
    r0 = out_row0 // T
    hps = math.gcd(H, FOX_SAMPLE_HEADS_PER_STEP)
    n_split = FOX_SAMPLE_KEY_SPLITS if P % (FOX_SAMPLE_KEY_SPLITS * LANES) == 0 else 1
    pn = P // n_split
    new = pl.BlockSpec((1, hps, T, D), lambda b, h: (b, h, 0, 0))
    old = [pl.BlockSpec((None, 1, hps, pn, D), lambda b, h, n=n: (layer, b, h, n, 0)) for n in range(n_split)]
    return _call(
        functools.partial(_fox_sample_body, scale=D ** -0.5, hps=hps, n_split=n_split),
        (B, H // hps),
        [new, *old, *old, new, new,
         pl.BlockSpec((T, LANES), lambda b, h: (b, 0)),
         pl.BlockSpec((1, hps, 1, P), lambda b, h: (b, h, 0, 0)),
         pl.BlockSpec((1, hps, 1, T), lambda b, h: (b, h, 0, 0))],
        [q, *([cache_k] * n_split), *([cache_v] * n_split), kn, vn, ln_tok, lc_row, ln_row],
        pl.BlockSpec((T, hps * D), lambda b, h: (b + r0, h)),
        jax.ShapeDtypeStruct(dst.shape, dst.dtype),
        ("parallel", "parallel"), dsts=[dst])


def _hgrn_body(*refs, C, hpb, has_init):
    if has_init:
        q_ref, k_ref, g_ref, i_ref, gate_ref, gn_ref, s0_ref, o_ref, sf_ref, st_ref = refs
    else:
        q_ref, k_ref, g_ref, i_ref, gate_ref, gn_ref, o_ref, sf_ref, st_ref = refs
        s0_ref = None
    c = pl.program_id(2)

    @pl.when(c == 0)
    def _():
        for hh in range(hpb):
            st_ref[hh] = s0_ref[0, hh].T if has_init else jnp.zeros((LANES, LANES), F32)

    row = lax.broadcasted_iota(jnp.int32, (C, C), 0)
    col = lax.broadcasted_iota(jnp.int32, (C, C), 1)
    tri = jnp.where(row >= col, 1.0, 0.0).astype(F32)
    trow = lax.broadcasted_iota(jnp.int32, (C, LANES), 0)
    split = jnp.where(row > col, row ^ col, 0)

    for hh in range(hpb):
        sl = slice(hh * LANES, (hh + 1) * LANES)
        q, k, g, v = q_ref[:, sl], k_ref[:, sl], g_ref[:, sl], i_ref[:, sl].astype(BF16)
        G = jnp.dot(tri, g, precision=HIGHEST, preferred_element_type=F32)
        st = st_ref[hh]
        o = _dot_nt((q * jnp.exp(G)).astype(BF16), st.astype(BF16))

        A = jnp.where(row == col, _dot_nt(q.astype(BF16), k.astype(BF16)), 0.0)
        yq, yk = G - g, G
        w, log_w = 1, 0
        while w < C:
            qt = (q * jnp.exp(G - yq)).astype(BF16)
            kt = (k * jnp.exp(yk - G)).astype(BF16)
            A = A + jnp.where((split >> log_w) == 1, _dot_nt(qt, kt), 0.0)
            upper = (trow & w) != 0
            yq = jnp.where(upper, pltpu.roll(yq, w, 0), yq)
            yk = jnp.where(upper, yk, pltpu.roll(yk, C - w, 0))
            w, log_w = 2 * w, log_w + 1
        o = o + _dot(A.astype(BF16), v)

        g_last = G[C - 1:C, :]
        ks = (k * jnp.exp(g_last - G)).astype(BF16)
        st_new = st * jnp.exp(g_last) + _dot_tn(v, ks)
        st_ref[hh] = st_new
        o_ref[:, sl] = (_rms(o, gn_ref[...]) * gate_ref[:, sl]).astype(o_ref.dtype)

    @pl.when(c == pl.num_programs(2) - 1)
    def _():
        for hh in range(hpb):
            sf_ref[0, hh] = st_ref[hh].T


def _hgrn(main, k, gnorm, s0, row0, B, S, H, layer, n_layers, o_dst, s_dst):
    T = k.shape[0]
    C = min(S, LANES)
    nc = S // C
    assert row0 % C == 0
    r0 = row0 // C
    hpb = math.gcd(H, HGRN_HEADS_PER_STEP)
    nhb = H // hpb
    tok = pl.BlockSpec((C, hpb * LANES), lambda b, h, c: (r0 + b * nc + c, h))
    seg = lambda n: pl.BlockSpec((C, hpb * LANES), lambda b, h, c: (r0 + b * nc + c, n * nhb + h))
    state = pl.BlockSpec((None, 1, hpb, LANES, LANES), lambda b, h, c: (layer, b, h, 0, 0))
    in_specs = [seg(0), tok, seg(1), seg(2), seg(3), pl.BlockSpec((1, LANES), lambda b, h, c: (0, 0))]
    args = [main, k, main, main, main, gnorm.reshape(1, LANES)]
    if s0 is not None:
        in_specs.append(state)
        args.append(s0)
    return _call(
        functools.partial(_hgrn_body, C=C, hpb=hpb, has_init=s0 is not None),
        (B, H // hpb, nc), in_specs, args,
        [tok, state],
        [jax.ShapeDtypeStruct((T, H * LANES), BF16), jax.ShapeDtypeStruct((n_layers, B, H, LANES, LANES), F32)],
        ("parallel", "parallel", "arbitrary"), dsts=[o_dst, s_dst],
        scratch=[pltpu.VMEM((hpb, LANES, LANES), F32)])


def _cmix_body(u_ref, v_ref, lng_ref, lnb_ref, ws_ref, bs_ref, y_ref, *vout, groups):
    v = v_ref[...]
    mu = jnp.mean(v, axis=-1, keepdims=True)
    xc = v - mu
    var = jnp.mean(xc * xc, axis=-1, keepdims=True)
    vn = xc * lax.rsqrt(var + EPS) * lng_ref[...] + lnb_ref[...]
    if vout:
        vout[0][...] = vn
    vb = vn.astype(BF16)
    Lc, Cd = v.shape
    cg = Cd // groups
    r = lax.broadcasted_iota(jnp.int32, (Lc, Lc), 0)
    c = lax.broadcasted_iota(jnp.int32, (Lc, Lc), 1)
    for gi in range(groups):
        sl = slice(gi * cg, (gi + 1) * cg)
        w = jnp.where(c <= r, ws_ref[gi], 0.0).astype(BF16)
        mixed = _dot(w, vb[:, sl]) + bs_ref[:, sl]
        y_ref[:, sl] = (u_ref[:, sl] * mixed).astype(y_ref.dtype)


def _cmix(uv, ln_g, ln_b, ws, bs, row0, n_rows, Lc, y_dst, v_out):
    T, Cd = uv.shape[0], uv.shape[1] // 2
    G = ws.shape[0]
    assert row0 % Lc == 0
    r0 = row0 // Lc
    tok = pl.BlockSpec((Lc, Cd), lambda i: (i + r0, 0))
    tok_v = pl.BlockSpec((Lc, Cd), lambda i: (i + r0, 1))
    vec = pl.BlockSpec((1, Cd), lambda i: (0, 0))
    out_specs, out_shapes, dsts = [tok], [jax.ShapeDtypeStruct((T, Cd), BF16)], [y_dst]
    if v_out:
        layer, n_layers, dst = v_out
        out_specs.append(pl.BlockSpec((None, Lc, Cd), lambda i: (layer, i, 0)))
        out_shapes.append(jax.ShapeDtypeStruct((n_layers, n_rows, Cd), F32))
        dsts.append(dst)
    return _call(
        functools.partial(_cmix_body, groups=G), (n_rows // Lc,),
        [tok, tok_v, vec, vec, pl.BlockSpec((G, Lc, Lc), lambda i: (0, 0, 0)),
         pl.BlockSpec((Lc, Cd), lambda i: (0, 0))],
        [uv, uv, ln_g.reshape(1, Cd), ln_b.reshape(1, Cd), ws, bs],
        out_specs, out_shapes, ("parallel",), dsts=dsts)


def _epi_id(z):
    return (z,)


def _epi_silu(z):
    return (z * jax.nn.sigmoid(z),)


def _epi_gelu(z):
    return (jax.nn.gelu(z),)


def _epi_logsig(z, b):
    return (jax.nn.log_sigmoid(z + b),)


def _epi_hgrn_forget(z, lb_all, *, layer):
    rows = [lb_all[n:n + 1] for n in range(lb_all.shape[0])]
    top = functools.reduce(jnp.maximum, rows)
    e = [jnp.exp(r - top) for r in rows]
    total = functools.reduce(jnp.add, e)
    sm = [a / total for a in e]
    lb = functools.reduce(jnp.add, sm[:layer + 1]) - sm[0]
    f = lb + (1.0 - lb) * jax.nn.sigmoid(z)
    g = jnp.log(jnp.maximum(f, TINY))
    k = (1.0 - lb) * jax.nn.sigmoid(-z)
    return g, k


def kernel(x_prompt, x_sample, cache_k, cache_v, cache_logf, state_hgrn, norm_ffn1, ffn1_gate, ffn1_up, ffn1_down, norm_mix, ab_w_in, ab_b_f, hgrn_lb, hgrn_gnorm, ab_w_out, c_w_in, c_ln_g, c_ln_b, c_w_s, c_b_s, c_w_out, norm_ffn2, ffn2_gate, ffn2_up, ffn2_down, norm_final):
    B, S, D = x_prompt.shape
    DB, DT, _ = x_sample.shape
    depth = norm_ffn1.shape[0]
    NAB, NC = ab_w_in.shape[0], c_w_in.shape[0]
    HA, P, HD = cache_k.shape[2], cache_k.shape[3], cache_k.shape[4]
    HB, DK, DV = state_hgrn.shape[2], state_hgrn.shape[3], state_hgrn.shape[4]
    assert HD == LANES and DK == LANES and DV == LANES and HA <= 8
    WA, WBK, WBV = HA * HD, HB * DK, HB * DV
    G, CL = c_w_s.shape[1], c_w_s.shape[2]
    CD = c_w_out.shape[1]
    Tp, Ts = B * S, DB * DT
    T = Tp + Ts
    bf = lambda a: a.astype(BF16)

    pk = pv = plf = ps = sk = sv = slf = ss = scv = None
    x = None
    for l in range(depth):
        j = l // 2
        w1 = (norm_ffn1[l], ffn1_gate, ffn1_up, ffn1_down, l, norm_mix[l])
        if l == 0:
            x, h = _ffn(x_prompt.reshape(Tp, D), *w1, out_rows=T)
            x, h = _ffn(x_sample.reshape(Ts, D), *w1, out_rows=T, out_row0=Tp, dsts=(x, h))
        else:
            x, h = _ffn(x, *w1)
        if l % 2 == 0:
            w_in = ab_w_in[j]
            o1 = 3 * WA + HA
            w_qkv = bf(w_in[:, :3 * WA])
            w_f = bf(jnp.pad(w_in[:, 3 * WA:o1], ((0, 0), (0, LANES - HA))))
            w_b = bf(w_in[:, o1:])
            b_f = jnp.pad(ab_b_f[j], (0, LANES - HA)).reshape(1, LANES)

            (lf_tok,) = _proj(h, w_f, [b_f], _epi_logsig, [F32], 1024)
            _, lp_row, plf = _cumsum(lf_tok, jnp.zeros((Tp, LANES), F32), 0, B, S, HA, j, NAB, plf)
            lc_row = _row_cumsum(cache_logf.reshape(NAB * DB * HA, P), j * DB * HA, DB * HA).reshape(DB, HA, P)
            base = jnp.pad(lc_row[:, :, P - 1], ((0, 0), (0, LANES - HA)))
            base = jnp.broadcast_to(base[:, None, :], (DB, DT, LANES)).reshape(Ts, LANES)
            ls_tok, ls_row, slf = _cumsum(lf_tok, base, Tp, DB, DT, HA, j, NAB, slf)

            qp, pk, kpb, pv, vpb = _proj_qkv(h, w_qkv, 0, B, S, j, NAB, pk, pv)
            qs, sk, ksb, sv, vsb = _proj_qkv(h, w_qkv, Tp, DB, DT, j, NAB, sk, sv)
            oa = _fox_prompt(qp, kpb, vpb, lp_row.reshape(B, HA, 1, S), T)
            oa = _fox_sample(qs, cache_k, cache_v, j, ksb, vsb, ls_tok,
                             lc_row.reshape(DB, HA, 1, P), ls_row.reshape(DB, HA, 1, DT), Tp, oa)

            assert WBK == WBV
            zb, kb = _proj_hgrn(h, w_b, hgrn_lb, j)
            ob, ps = _hgrn(zb, kb, hgrn_gnorm[j], None, 0, B, S, HB, j, NAB, None, ps)
            ob, ss = _hgrn(zb, kb, hgrn_gnorm[j], state_hgrn, Tp, DB, DT, HB, j, NAB, ob, ss)

            w_out = bf(ab_w_out[j])
            x = _mm_res([oa, ob], [w_out[:WA], w_out[WA:]], x)
        else:
            (uv,) = _proj(h, bf(c_w_in[j]), [], _epi_gelu, [F32], 512)
            y = None
            for row0, n_rows, seq in ((0, Tp, S), (Tp, Ts, DT)):
                Lc = min(seq, CL)
                ws = c_w_s[j][:, :Lc, :Lc]
                bs = jnp.repeat(c_b_s[j][:, :Lc].T, CD // G, axis=1)
                if row0 == 0:
                    (y,) = _cmix(uv, c_ln_g[j], c_ln_b[j], ws, bs, row0, n_rows, Lc, y, None)
                else:
                    y, scv = _cmix(uv, c_ln_g[j], c_ln_b[j], ws, bs, row0, n_rows, Lc, y, (j, NC, scv))
            x = _mm_res([y], [bf(c_w_out[j])], x)
        x = _ffn(x, norm_ffn2[l], ffn2_gate, ffn2_up, ffn2_down, l)

    y_p = _norm(x, norm_final, F32, 0, Tp)
    y_s = _norm(x, norm_final, F32, Tp, Ts)
    return (y_p.reshape(B, S, D), y_s.reshape(DB, DT, D), pk, pv, plf, ps,
            sk, sv, slf, ss, scv.reshape(NC, DB, DT, CD))
```

```python
import functools
import math

import jax
import jax.numpy as jnp
from jax import lax
from jax.experimental import pallas as pl
from jax.experimental.pallas import tpu as pltpu

F32 = jnp.float32
BF16 = jnp.bfloat16
EPS = 1e-6
TINY = 1e-30
MASK_VALUE = -1e30
LOG2E = math.log2(math.e)
HIGHEST = lax.Precision.HIGHEST

LANES = 128
MIB = 1 << 20
VMEM_LIMIT = 60 * MIB
FFN_TOKEN_TILE = 1024
FFN_FF_TILE = 256
PROJ_COL_TILE = 512
HGRN_HEADS_PER_STEP = 8
HGRN_HEADS_PER_GROUP = 8
FOX_SAMPLE_HEADS_PER_STEP = 2
FOX_SAMPLE_KEY_SPLITS = 2


def _tile(n, target):
    if n <= target:
        return n
    for t in range(target, 7, -1):
        if n % t == 0 and t % 8 == 0:
            return t
    return n


def _call(body, grid, in_specs, args, out_specs, out_shapes, sem, dsts=None, scratch=()):
    in_specs, args = list(in_specs), list(args)
    n_in = len(args)
    aliases = {}
    for k, d in enumerate(dsts or ()):
        if d is not None:
            aliases[len(args)] = k
            in_specs.append(pl.BlockSpec(memory_space=pl.ANY))
            args.append(d)
    n_all = len(args)

    def wrapped(*refs):
        body(*refs[:n_in], *refs[n_all:])

    return pl.pallas_call(
        wrapped, grid=grid, in_specs=in_specs, out_specs=out_specs, out_shape=out_shapes,
        input_output_aliases=aliases, scratch_shapes=list(scratch),
        compiler_params=pltpu.CompilerParams(dimension_semantics=sem, vmem_limit_bytes=VMEM_LIMIT),
    )(*args)


def _dot(a, b):
    return jnp.dot(a, b, preferred_element_type=F32)


def _dot_nt(a, b):
    return lax.dot_general(a, b, (((1,), (1,)), ((), ())), preferred_element_type=F32)


def _dot_tn(a, b):
    return lax.dot_general(a, b, (((0,), (0,)), ((), ())), preferred_element_type=F32)


def _rms(x, g):
    return x * lax.rsqrt(jnp.mean(x * x, axis=-1, keepdims=True) + EPS) * g


def _ffn_body(x_ref, g_ref, wg_ref, wu_ref, wd_ref, *refs, next_norm):
    if next_norm:
        g2_ref, o_ref, h2_ref, h_ref = refs
    else:
        o_ref, h_ref = refs

    @pl.when(pl.program_id(1) == 0)
    def _():
        x = x_ref[...]
        h_ref[...] = _rms(x, g_ref[...]).astype(BF16)
        o_ref[...] = x

    h = h_ref[...]
    a = _dot(h, wg_ref[...].astype(BF16))
    b = _dot(h, wu_ref[...].astype(BF16))
    act = (0.5 * a * jax.nn.sigmoid(a) * b).astype(BF16)
    o_ref[...] += _dot(act, wd_ref[...].astype(BF16))

    if next_norm:
        @pl.when(pl.program_id(1) == pl.num_programs(1) - 1)
        def _():
            h2_ref[...] = _rms(o_ref[...], g2_ref[...]).astype(h2_ref.dtype)


def _ffn(x, g, wg, wu, wd, layer, g_next=None, out_rows=None, out_row0=0, dsts=(None, None)):
    T, D = x.shape
    F = wg.shape[2]
    tm = _tile(T, FFN_TOKEN_TILE)
    tf = _tile(F, FFN_FF_TILE)
    out_rows = out_rows or T
    assert out_row0 % tm == 0
    i0 = out_row0 // tm
    vec = pl.BlockSpec((1, D), lambda i, j: (0, 0))
    tok = pl.BlockSpec((tm, D), lambda i, j: (i + i0, 0))
    in_specs = [pl.BlockSpec((tm, D), lambda i, j: (i, 0)), vec,
                pl.BlockSpec((None, D, tf), lambda i, j: (layer, 0, j)),
                pl.BlockSpec((None, D, tf), lambda i, j: (layer, 0, j)),
                pl.BlockSpec((None, tf, D), lambda i, j: (layer, j, 0))]
    args = [x, g.reshape(1, D), wg, wu, wd]
    out_specs, out_shapes = [tok], [jax.ShapeDtypeStruct((out_rows, D), F32)]
    if g_next is not None:
        in_specs.append(vec)
        args.append(g_next.reshape(1, D))
        out_specs.append(tok)
        out_shapes.append(jax.ShapeDtypeStruct((out_rows, D), BF16))
    outs = _call(
        functools.partial(_ffn_body, next_norm=g_next is not None), (T // tm, F // tf),
        in_specs, args, out_specs, out_shapes,
        ("parallel", "arbitrary"), dsts=list(dsts[:len(out_specs)]),
        scratch=[pltpu.VMEM((tm, D), BF16)])
    return outs if g_next is not None else outs[0]


def _norm_body(x_ref, g_ref, o_ref):
    o_ref[...] = _rms(x_ref[...], g_ref[...]).astype(o_ref.dtype)


def _norm(x, g, dtype, row0=0, n_rows=None):
    D = x.shape[1]
    n_rows = n_rows or x.shape[0]
    tm = _tile(n_rows, 512)
    assert row0 % tm == 0
    i0 = row0 // tm
    return _call(
        _norm_body, (n_rows // tm,),
        [pl.BlockSpec((tm, D), lambda i: (i + i0, 0)), pl.BlockSpec((1, D), lambda i: (0, 0))],
        [x, g.reshape(1, D)],
        pl.BlockSpec((tm, D), lambda i: (i, 0)),
        jax.ShapeDtypeStruct((n_rows, D), dtype), ("parallel",))


def _resident(shape):
    return pl.BlockSpec(shape, lambda i: (0,) * len(shape), pipeline_mode=pl.Buffered(1))


def _col_tiles(n):
    tn = _tile(n, PROJ_COL_TILE)
    return [slice(c * tn, (c + 1) * tn) for c in range(n // tn)]


def _proj_body(h_ref, w_ref, *refs, epilogue, n_aux):
    aux, outs = refs[:n_aux], refs[n_aux:]
    h = h_ref[...]
    for cols in _col_tiles(w_ref.shape[1]):
        vals = epilogue(_dot(h, w_ref[:, cols]), *[a[:, cols] for a in aux])
        for o_ref, v in zip(outs, vals):
            o_ref[:, cols] = v.astype(o_ref.dtype)


def _proj(h, w, aux, epilogue, out_dtypes, tm_target):
    T, K = h.shape
    N = w.shape[1]
    tm = _tile(T, tm_target)
    return _call(
        functools.partial(_proj_body, epilogue=epilogue, n_aux=len(aux)),
        (T // tm,),
        [pl.BlockSpec((tm, K), lambda i: (i, 0)), _resident((K, N))] + [_resident(a.shape) for a in aux],
        [h, w, *aux],
        [pl.BlockSpec((tm, N), lambda i: (i, 0)) for _ in out_dtypes],
        [jax.ShapeDtypeStruct((T, N), d) for d in out_dtypes],
        ("parallel",))


def _proj_hgrn_body(h_ref, w_ref, lb_ref, main_ref, k_ref, *, layer):
    W = k_ref.shape[1]
    h = h_ref[...]
    for seg in range(4):
        for kc in _col_tiles(W):
            cols = slice(seg * W + kc.start, seg * W + kc.stop)
            z = _dot(h, w_ref[:, cols])
            if seg == 1:
                g, k = _epi_hgrn_forget(z, lb_ref[:, kc], layer=layer)
                main_ref[:, cols] = g
                k_ref[:, kc] = k
            else:
                main_ref[:, cols] = z if seg == 2 else z * jax.nn.sigmoid(z)


def _proj_hgrn(h, w, lb_all, layer):
    T, K = h.shape
    W = w.shape[1] // 4
    tm = _tile(T, 512)
    return _call(
        functools.partial(_proj_hgrn_body, layer=layer),
        (T // tm,),
        [pl.BlockSpec((tm, K), lambda i: (i, 0)), _resident((K, 4 * W)), _resident(lb_all.shape)],
        [h, w, lb_all],
        [pl.BlockSpec((tm, 4 * W), lambda i: (i, 0)), pl.BlockSpec((tm, W), lambda i: (i, 0))],
        [jax.ShapeDtypeStruct((T, 4 * W), F32), jax.ShapeDtypeStruct((T, W), F32)],
        ("parallel",))


def _qkv_body(h_ref, w_ref, q_ref, k_ref, kb_ref, v_ref, vb_ref, *, nb, ts):
    H = q_ref.shape[1]
    h = h_ref[...]
    targets = ((q_ref,), (k_ref, kb_ref), (v_ref, vb_ref))
    for cols in _col_tiles(w_ref.shape[1]):
        z = _dot(h, w_ref[:, cols])
        for n in range((cols.stop - cols.start) // LANES):
            head = cols.start // LANES + n
            zz = z[:, n * LANES:(n + 1) * LANES].reshape(nb, ts, LANES)
            for o_ref in targets[head // H]:
                o_ref[:, head % H] = zz.astype(o_ref.dtype)


def _proj_qkv(h, w, row0, B, S, layer, n_layers, k_dst, v_dst):
    K = h.shape[1]
    H = w.shape[1] // (3 * LANES)
    if S >= 512:
        ts, nb = _tile(S, 512), 1
    else:
        nb = _tile(B, max(1, 512 // S))
        ts = S
    tm = nb * ts
    spt = S // ts
    assert row0 % tm == 0
    i0 = row0 // tm
    if nb == 1:
        omap = lambda i: (i // spt, 0, i % spt, 0)
    else:
        omap = lambda i: (i, 0, 0, 0)
    plain = pl.BlockSpec((nb, H, ts, LANES), omap)
    slab = pl.BlockSpec((None, nb, H, ts, LANES), lambda i: (layer, *omap(i)))
    bf_shape = jax.ShapeDtypeStruct((B, H, S, LANES), BF16)
    slab_shape = jax.ShapeDtypeStruct((n_layers, B, H, S, LANES), F32)
    return _call(
        functools.partial(_qkv_body, nb=nb, ts=ts),
        (B * S // tm,),
        [pl.BlockSpec((tm, K), lambda i: (i + i0, 0)), _resident(w.shape)],
        [h, w],
        [plain, slab, plain, slab, plain],
        [bf_shape, slab_shape, bf_shape, slab_shape, bf_shape],
        ("parallel",), dsts=[None, k_dst, None, v_dst, None])


def _mm_res_body(*refs, n):
    lhs = [r[...] for r in refs[:n]]
    ws = refs[n:2 * n]
    x_ref, o_ref = refs[2 * n], refs[2 * n + 1]
    for cols in _col_tiles(x_ref.shape[1]):
        z = _dot(lhs[0], ws[0][:, cols])
        for a, w_ref in zip(lhs[1:], ws[1:]):
            z += _dot(a, w_ref[:, cols])
        o_ref[:, cols] = x_ref[:, cols] + z


def _mm_res(lhs, ws, x):
    T, N = x.shape
    n = len(lhs)
    tm = _tile(T, 512)
    tok = pl.BlockSpec((tm, N), lambda i: (i, 0))
    return _call(
        functools.partial(_mm_res_body, n=n), (T // tm,),
        ([pl.BlockSpec((tm, a.shape[1]), lambda i: (i, 0)) for a in lhs]
         + [_resident(w.shape) for w in ws] + [tok]),
        [*lhs, *ws, x],
        tok, jax.ShapeDtypeStruct((T, N), F32), ("parallel",))


def _cumsum_body(lf_ref, base_ref, ltok_ref, lrow_ref, lfrow_ref, carry_ref, *, ts, seg, tiles_per_seq, n_heads):
    lf = lf_ref[...]
    r = lax.broadcasted_iota(jnp.int32, (ts, ts), 0)
    c = lax.broadcasted_iota(jnp.int32, (ts, ts), 1)
    tri = jnp.where(r >= c, 1.0, 0.0).astype(F32)
    if seg < ts:
        tri = jnp.where(r // seg == c // seg, tri, 0.0)
    L = jnp.dot(tri, lf, precision=HIGHEST, preferred_element_type=F32)
    if tiles_per_seq:
        @pl.when(pl.program_id(0) % tiles_per_seq == 0)
        def _():
            carry_ref[...] = jnp.zeros_like(carry_ref)
        L = L + carry_ref[...]
        carry_ref[...] = L[ts - 1:ts, :]
    else:
        L = L + base_ref[...]
    ltok_ref[...] = L
    Lt = L.T[:n_heads]
    lft = lf.T[:n_heads]
    for s in range(ts // seg):
        lrow_ref[s] = Lt[:, s * seg:(s + 1) * seg]
        lfrow_ref[s] = lft[:, s * seg:(s + 1) * seg]


def _cumsum(lf_tok, base, row0, B, S, n_heads, layer, n_layers, dst):
    if S >= LANES:
        ts = _tile(S, 512)
        seg, tps, nseg = ts, S // ts, 1
    else:
        ts, seg, tps = LANES, S, 0
        nseg = ts // seg
    assert row0 % ts == 0 and (B * S) % ts == 0
    i0 = row0 // ts
    if tps:
        omap = lambda i: (i // tps, 0, i % tps)
    else:
        omap = lambda i: (i, 0, 0)
    return _call(
        functools.partial(_cumsum_body, ts=ts, seg=seg, tiles_per_seq=tps, n_heads=n_heads),
        (B * S // ts,),
        [pl.BlockSpec((ts, LANES), lambda i: (i + i0, 0)), pl.BlockSpec((ts, LANES), lambda i: (i, 0))],
        [lf_tok, base],
        [pl.BlockSpec((ts, LANES), lambda i: (i, 0)),
         pl.BlockSpec((nseg, n_heads, seg), omap),
         pl.BlockSpec((None, nseg, n_heads, seg), lambda i: (layer, *omap(i)))],
        [jax.ShapeDtypeStruct((B * S, LANES), F32), jax.ShapeDtypeStruct((B, n_heads, S), F32),
         jax.ShapeDtypeStruct((n_layers, B, n_heads, S), F32)],
        ("arbitrary",), dsts=[None, None, dst], scratch=[pltpu.VMEM((1, LANES), F32)])


def _row_cumsum_body(x_ref, o_ref):
    R, P = x_ref.shape
    r = lax.broadcasted_iota(jnp.int32, (LANES, LANES), 0)
    c = lax.broadcasted_iota(jnp.int32, (LANES, LANES), 1)
    tri = jnp.where(r <= c, 1.0, 0.0).astype(F32)
    carry = jnp.zeros((R, 1), F32)
    for s in range(P // LANES):
        y = jnp.dot(x_ref[:, s * LANES:(s + 1) * LANES], tri, precision=HIGHEST, preferred_element_type=F32) + carry
        o_ref[:, s * LANES:(s + 1) * LANES] = y
        carry = y[:, LANES - 1:LANES]


def _row_cumsum(x, row0, n_rows):
    P = x.shape[1]
    tr = _tile(n_rows, 256)
    assert row0 % tr == 0
    i0 = row0 // tr
    return _call(
        _row_cumsum_body, (n_rows // tr,),
        [pl.BlockSpec((tr, P), lambda i: (i + i0, 0))], [x],
        pl.BlockSpec((tr, P), lambda i: (i, 0)),
        jax.ShapeDtypeStruct((n_rows, P), F32), ("parallel",))


def _head_row(rows, h):
    sublane = lax.broadcasted_iota(jnp.int32, rows.shape, 0)
    return jnp.sum(jnp.where(sublane == h, rows, 0.0), axis=0, keepdims=True)


def _fox_prompt_body(q_ref, k_ref, v_ref, l_ref, o_ref, m_ref, d_ref, acc_ref, s0_ref, s1_ref, *, scale, tq, sub):
    qi = pl.program_id(2)
    c1 = scale * LOG2E
    tk = tq // 2
    n_sub = tq // sub
    every = tuple(range(n_sub))
    m_ref[...] = jnp.full_like(m_ref, MASK_VALUE)
    d_ref[...] = jnp.zeros_like(d_ref)
    acc_ref[...] = jnp.zeros_like(acc_ref)
    head = pl.program_id(1)
    l_here = _head_row(l_ref[0, :, pl.ds(pl.multiple_of(qi * tq, tq), LANES)], head)[:, 0:1]

    def scores(ki, s_ref, subs):
        k0 = pl.multiple_of(ki * tk, tk)
        k = k_ref[0, 0, pl.ds(k0, tk), :]
        bias = (l_here - _head_row(l_ref[0, :, pl.ds(k0, tk)], head)) * LOG2E
        for u in subs:
            rows = slice(u * sub, (u + 1) * sub)
            s_ref[rows] = _dot_nt(q_ref[0, 0, rows, :], k) * c1 + bias

    def update(ki, s_ref, subs, diagonal=None):
        k0 = pl.multiple_of(ki * tk, tk)
        v = v_ref[0, 0, pl.ds(k0, tk), :]
        for n in range(0, len(subs), 2):
            update_pair(s_ref, v, subs[n:n + 2], diagonal)

    def update_pair(s_ref, v, subs, diagonal):
        rows = [slice(u * sub, (u + 1) * sub) for u in subs]
        s = []
        for u, rw in zip(subs, rows):
            su = s_ref[rw]
            if diagonal is not None and u * sub < (diagonal + 1) * tk - 1:
                r = lax.broadcasted_iota(jnp.int32, su.shape, 0) + u * sub
                c = lax.broadcasted_iota(jnp.int32, su.shape, 1) + diagonal * tk
                su = jnp.where(c <= r, su, MASK_VALUE)
            s.append(su)
        m_prev = [m_ref[rw] for rw in rows]
        m_new = [jnp.maximum(mp, jnp.max(su, axis=-1, keepdims=True)) for mp, su in zip(m_prev, s)]
        alpha = [jnp.exp2(mp - mn) for mp, mn in zip(m_prev, m_new)]
        p = [jnp.exp2(su - mn) for su, mn in zip(s, m_new)]
        for n, rw in enumerate(rows):
            d_ref[rw] = alpha[n] * d_ref[rw] + jnp.sum(p[n], axis=-1, keepdims=True)
            m_ref[rw] = m_new[n]
        for n, rw in enumerate(rows):
            acc_ref[rw] = alpha[n] * acc_ref[rw] + _dot(p[n].astype(BF16), v)

    def two_blocks(t, carry):
        scores(2 * t + 1, s1_ref, every)
        update(2 * t, s0_ref, every)
        scores(2 * t + 2, s0_ref, every)
        update(2 * t + 1, s1_ref, every)
        return carry

    scores(0, s0_ref, every)
    lax.fori_loop(0, qi, two_blocks, 0)
    late = tuple(u for u in every if (u + 1) * sub > tk)
    scores(2 * qi + 1, s1_ref, late)
    update(2 * qi, s0_ref, every, diagonal=0)
    update(2 * qi + 1, s1_ref, late, diagonal=1)
    o_ref[...] = (acc_ref[...] / d_ref[...]).astype(o_ref.dtype)


def _fox_prompt(q, k, v, l_row, out_rows):
    B, H, S, D = q.shape
    tq = _tile(S, 1024)
    sub = min(256, tq // 2)
    assert tq % (2 * LANES) == 0 and (tq // 2) % sub == 0
    nq = S // tq
    full = pl.BlockSpec((1, 1, S, D), lambda b, h, qi: (b, h, 0, 0))
    return _call(
        functools.partial(_fox_prompt_body, scale=D ** -0.5, tq=tq, sub=sub),
        (B, H, nq),
        [pl.BlockSpec((1, 1, tq, D), lambda b, h, qi: (b, h, qi, 0)), full, full,
         pl.BlockSpec((1, H, S), lambda b, h, qi: (b, 0, 0))],
        [q, k, v, l_row],
        pl.BlockSpec((tq, D), lambda b, h, qi: (b * nq + qi, h)),
        jax.ShapeDtypeStruct((out_rows, H * D), BF16),
        ("parallel", "parallel", "arbitrary"),
        scratch=[pltpu.VMEM((tq, 1), F32), pltpu.VMEM((tq, 1), F32), pltpu.VMEM((tq, D), F32),
                 pltpu.VMEM((tq, tq // 2), F32), pltpu.VMEM((tq, tq // 2), F32)])


def _head_column(l_tok, h):
    lane = lax.broadcasted_iota(jnp.int32, l_tok.shape, 1)
    return jnp.sum(jnp.where(lane == h, l_tok, 0.0), axis=-1, keepdims=True)


def _fox_sample_body(q_ref, *refs, scale, hps, n_split):
    ck_refs, cv_refs = refs[:n_split], refs[n_split:2 * n_split]
    kn_ref, vn_ref, lq_ref, lc_ref, ln_ref, o_ref = refs[2 * n_split:]
    hg = pl.program_id(1)
    pn = ck_refs[0].shape[-2]
    D = q_ref.shape[-1]
    for hh in range(hps):
        q = q_ref[0, hh]
        lq = _head_column(lq_ref[...], hg * hps + hh)
        s_c = [_dot_nt(q, ck_refs[n][0, hh].astype(BF16)) * scale + (lq - _head_row(lc_ref[0, :, n * pn:(n + 1) * pn], hg * hps + hh))
               for n in range(n_split)]
        s_n = _dot_nt(q, kn_ref[0, hh]) * scale + (lq - _head_row(ln_ref[0], hg * hps + hh))
        r = lax.broadcasted_iota(jnp.int32, s_n.shape, 0)
        c = lax.broadcasted_iota(jnp.int32, s_n.shape, 1)
        s_n = jnp.where(c <= r, s_n, MASK_VALUE)
        m = jnp.max(s_n, axis=-1, keepdims=True)
        for s in s_c:
            m = jnp.maximum(m, jnp.max(s, axis=-1, keepdims=True))
        p_n = jnp.exp(s_n - m)
        l = jnp.sum(p_n, axis=-1, keepdims=True)
        o = _dot(p_n.astype(BF16), vn_ref[0, hh])
        for n in range(n_split):
            p = jnp.exp(s_c[n] - m)
            l = l + jnp.sum(p, axis=-1, keepdims=True)
            o = o + _dot(p.astype(BF16), cv_refs[n][0, hh].astype(BF16))
        o_ref[:, hh * D:(hh + 1) * D] = (o / l).astype(o_ref.dtype)


def _fox_sample(q, cache_k, cache_v, layer, kn, vn, ln_tok, lc_row, ln_row, out_row0, dst):
    B, H, T, D = q.shape
    P = cache_k.shape[3]
    assert out_row0 % T == 0
    r0 = out_row0 // T
    hps = math.gcd(H, FOX_SAMPLE_HEADS_PER_STEP)
    n_split = FOX_SAMPLE_KEY_SPLITS if P % (FOX_SAMPLE_KEY_SPLITS * LANES) == 0 else 1
    pn = P // n_split
    new = pl.BlockSpec((1, hps, T, D), lambda b, h: (b, h, 0, 0))
    old = [pl.BlockSpec((None, 1, hps, pn, D), lambda b, h, n=n: (layer, b, h, n, 0)) for n in range(n_split)]
    return _call(
        functools.partial(_fox_sample_body, scale=D ** -0.5, hps=hps, n_split=n_split),
        (B, H // hps),
        [new, *old, *old, new, new,
         pl.BlockSpec((T, LANES), lambda b, h: (b, 0)),
         pl.BlockSpec((1, H, P), lambda b, h: (b, 0, 0)),
         pl.BlockSpec((1, H, T), lambda b, h: (b, 0, 0))],
        [q, *([cache_k] * n_split), *([cache_v] * n_split), kn, vn, ln_tok, lc_row, ln_row],
        pl.BlockSpec((T, hps * D), lambda b, h: (b + r0, h)),
        jax.ShapeDtypeStruct(dst.shape, dst.dtype),
        ("parallel", "parallel"), dsts=[dst])


def _hgrn_body(*refs, C, hpb, has_init):
    if has_init:
        q_ref, k_ref, g_ref, i_ref, gate_ref, gn_ref, s0_ref, o_ref, sf_ref, st_ref = refs
    else:
        q_ref, k_ref, g_ref, i_ref, gate_ref, gn_ref, o_ref, sf_ref, st_ref = refs
        s0_ref = None
    c = pl.program_id(2)

    @pl.when(c == 0)
    def _():
        for hh in range(hpb):
            st_ref[hh] = s0_ref[0, hh].T if has_init else jnp.zeros((LANES, LANES), F32)

    row = lax.broadcasted_iota(jnp.int32, (C, C), 0)
    col = lax.broadcasted_iota(jnp.int32, (C, C), 1)
    tri = jnp.where(row >= col, 1.0, 0.0).astype(F32)
    trow = lax.broadcasted_iota(jnp.int32, (C, LANES), 0)
    level = 31 - lax.clz(jnp.where(row > col, row ^ col, 0))
    level = jnp.where(row == col, C.bit_length() - 1, level)

    def recur(heads):
        sls = {hh: slice(hh * LANES, (hh + 1) * LANES) for hh in heads}
        q = {hh: q_ref[:, sls[hh]] for hh in heads}
        k = {hh: k_ref[:, sls[hh]] for hh in heads}
        v = {hh: i_ref[:, sls[hh]].astype(BF16) for hh in heads}
        g = {hh: g_ref[:, sls[hh]] * LOG2E for hh in heads}
        G = {hh: jnp.dot(tri, g[hh], precision=HIGHEST, preferred_element_type=F32) for hh in heads}
        st = {hh: st_ref[hh] for hh in heads}
        o = {hh: _dot_nt((q[hh] * jnp.exp2(G[hh])).astype(BF16), st[hh].astype(BF16)) for hh in heads}
        A = {hh: jnp.where(level == C.bit_length() - 1, _dot_nt(q[hh].astype(BF16), k[hh].astype(BF16)), 0.0)
             for hh in heads}
        yq = {hh: G[hh] - g[hh] for hh in heads}
        yk = dict(G)
        w, log_w = 1, 0
        while w < C:
            upper = (trow & w) != 0
            for hh in heads:
                qt = (q[hh] * jnp.exp2(G[hh] - yq[hh])).astype(BF16)
                kt = (k[hh] * jnp.exp2(yk[hh] - G[hh])).astype(BF16)
                A[hh] = jnp.where(level == log_w, _dot_nt(qt, kt), A[hh])
                yq[hh] = jnp.where(upper, pltpu.roll(yq[hh], w, 0), yq[hh])
                yk[hh] = jnp.where(upper, yk[hh], pltpu.roll(yk[hh], C - w, 0))
            w, log_w = 2 * w, log_w + 1
        for hh in heads:
            out = o[hh] + _dot(A[hh].astype(BF16), v[hh])
            g_last = G[hh][C - 1:C, :]
            ks = (k[hh] * jnp.exp2(g_last - G[hh])).astype(BF16)
            st_ref[hh] = st[hh] * jnp.exp2(g_last) + _dot_tn(v[hh], ks)
            o_ref[:, sls[hh]] = (_rms(out, gn_ref[...]) * gate_ref[:, sls[hh]]).astype(o_ref.dtype)

    group = math.gcd(hpb, HGRN_HEADS_PER_GROUP)
    for h0 in range(0, hpb, group):
        recur(range(h0, h0 + group))

    @pl.when(c == pl.num_programs(2) - 1)
    def _():
        for hh in range(hpb):
            sf_ref[0, hh] = st_ref[hh].T


def _hgrn(main, k, gnorm, s0, row0, B, S, H, layer, n_layers, o_dst, s_dst):
    T = k.shape[0]
    C = min(S, LANES)
    nc = S // C
    assert row0 % C == 0
    r0 = row0 // C
    hpb = math.gcd(H, HGRN_HEADS_PER_STEP)
    nhb = H // hpb
    tok = pl.BlockSpec((C, hpb * LANES), lambda b, h, c: (r0 + b * nc + c, h))
    seg = lambda n: pl.BlockSpec((C, hpb * LANES), lambda b, h, c: (r0 + b * nc + c, n * nhb + h))
    state = pl.BlockSpec((None, 1, hpb, LANES, LANES), lambda b, h, c: (layer, b, h, 0, 0))
    in_specs = [seg(0), tok, seg(1), seg(2), seg(3), pl.BlockSpec((1, LANES), lambda b, h, c: (0, 0))]
    args = [main, k, main, main, main, gnorm.reshape(1, LANES)]
    if s0 is not None:
        in_specs.append(state)
        args.append(s0)
    return _call(
        functools.partial(_hgrn_body, C=C, hpb=hpb, has_init=s0 is not None),
        (B, H // hpb, nc), in_specs, args,
        [tok, state],
        [jax.ShapeDtypeStruct((T, H * LANES), BF16), jax.ShapeDtypeStruct((n_layers, B, H, LANES, LANES), F32)],
        ("parallel", "parallel", "arbitrary"), dsts=[o_dst, s_dst],
        scratch=[pltpu.VMEM((hpb, LANES, LANES), F32)])


def _cmix_body(u_ref, v_ref, lng_ref, lnb_ref, ws_ref, bs_ref, y_ref, *vout, groups):
    v = v_ref[...]
    mu = jnp.mean(v, axis=-1, keepdims=True)
    xc = v - mu
    var = jnp.mean(xc * xc, axis=-1, keepdims=True)
    vn = xc * lax.rsqrt(var + EPS) * lng_ref[...] + lnb_ref[...]
    if vout:
        vout[0][...] = vn
    vb = vn.astype(BF16)
    Lc, Cd = v.shape
    cg = Cd // groups
    r = lax.broadcasted_iota(jnp.int32, (Lc, Lc), 0)
    c = lax.broadcasted_iota(jnp.int32, (Lc, Lc), 1)
    for gi in range(groups):
        sl = slice(gi * cg, (gi + 1) * cg)
        w = jnp.where(c <= r, ws_ref[gi], 0.0).astype(BF16)
        mixed = _dot(w, vb[:, sl]) + bs_ref[:, sl]
        y_ref[:, sl] = (u_ref[:, sl] * mixed).astype(y_ref.dtype)


def _cmix(uv, ln_g, ln_b, ws, bs, row0, n_rows, Lc, y_dst, v_out):
    T, Cd = uv.shape[0], uv.shape[1] // 2
    G = ws.shape[0]
    assert row0 % Lc == 0
    r0 = row0 // Lc
    tok = pl.BlockSpec((Lc, Cd), lambda i: (i + r0, 0))
    tok_v = pl.BlockSpec((Lc, Cd), lambda i: (i + r0, 1))
    vec = pl.BlockSpec((1, Cd), lambda i: (0, 0))
    out_specs, out_shapes, dsts = [tok], [jax.ShapeDtypeStruct((T, Cd), BF16)], [y_dst]
    if v_out:
        layer, n_layers, dst = v_out
        out_specs.append(pl.BlockSpec((None, Lc, Cd), lambda i: (layer, i, 0)))
        out_shapes.append(jax.ShapeDtypeStruct((n_layers, n_rows, Cd), F32))
        dsts.append(dst)
    return _call(
        functools.partial(_cmix_body, groups=G), (n_rows // Lc,),
        [tok, tok_v, vec, vec, pl.BlockSpec((G, Lc, Lc), lambda i: (0, 0, 0)),
         pl.BlockSpec((Lc, Cd), lambda i: (0, 0))],
        [uv, uv, ln_g.reshape(1, Cd), ln_b.reshape(1, Cd), ws, bs],
        out_specs, out_shapes, ("parallel",), dsts=dsts)


def _epi_gelu(z):
    return (jax.nn.gelu(z),)


def _epi_logsig(z, b):
    return (jax.nn.log_sigmoid(z + b),)


def _epi_hgrn_forget(z, lb_all, *, layer):
    rows = [lb_all[n:n + 1] for n in range(lb_all.shape[0])]
    top = functools.reduce(jnp.maximum, rows)
    e = [jnp.exp(r - top) for r in rows]
    total = functools.reduce(jnp.add, e)
    sm = [a / total for a in e]
    lb = functools.reduce(jnp.add, sm[:layer + 1]) - sm[0]
    f = lb + (1.0 - lb) * jax.nn.sigmoid(z)
    g = jnp.log(jnp.maximum(f, TINY))
    k = (1.0 - lb) * jax.nn.sigmoid(-z)
    return g, k


def kernel(x_prompt, x_sample, cache_k, cache_v, cache_logf, state_hgrn, norm_ffn1, ffn1_gate, ffn1_up, ffn1_down, norm_mix, ab_w_in, ab_b_f, hgrn_lb, hgrn_gnorm, ab_w_out, c_w_in, c_ln_g, c_ln_b, c_w_s, c_b_s, c_w_out, norm_ffn2, ffn2_gate, ffn2_up, ffn2_down, norm_final):
    B, S, D = x_prompt.shape
    DB, DT, _ = x_sample.shape
    depth = norm_ffn1.shape[0]
    NAB, NC = ab_w_in.shape[0], c_w_in.shape[0]
    HA, P, HD = cache_k.shape[2], cache_k.shape[3], cache_k.shape[4]
    HB, DK, DV = state_hgrn.shape[2], state_hgrn.shape[3], state_hgrn.shape[4]
    assert HD == LANES and DK == LANES and DV == LANES and HA <= 8
    WA, WBK, WBV = HA * HD, HB * DK, HB * DV
    G, CL = c_w_s.shape[1], c_w_s.shape[2]
    CD = c_w_out.shape[1]
    Tp, Ts = B * S, DB * DT
    T = Tp + Ts
    bf = lambda a: a.astype(BF16)

    pk = pv = plf = ps = sk = sv = slf = ss = scv = None
    x = None
    for l in range(depth):
        j = l // 2
        w1 = (norm_ffn1[l], ffn1_gate, ffn1_up, ffn1_down, l, norm_mix[l])
        if l == 0:
            x, h = _ffn(x_prompt.reshape(Tp, D), *w1, out_rows=T)
            x, h = _ffn(x_sample.reshape(Ts, D), *w1, out_rows=T, out_row0=Tp, dsts=(x, h))
        else:
            x, h = _ffn(x, *w1)
        if l % 2 == 0:
            w_in = ab_w_in[j]
            o1 = 3 * WA + HA
            w_qkv = bf(w_in[:, :3 * WA])
            w_f = bf(jnp.pad(w_in[:, 3 * WA:o1], ((0, 0), (0, LANES - HA))))
            w_b = bf(w_in[:, o1:])
            b_f = jnp.pad(ab_b_f[j], (0, LANES - HA)).reshape(1, LANES)

            (lf_tok,) = _proj(h, w_f, [b_f], _epi_logsig, [F32], 1024)
            _, lp_row, plf = _cumsum(lf_tok, jnp.zeros((Tp, LANES), F32), 0, B, S, HA, j, NAB, plf)
            lc_row = _row_cumsum(cache_logf.reshape(NAB * DB * HA, P), j * DB * HA, DB * HA).reshape(DB, HA, P)
            base = jnp.pad(lc_row[:, :, P - 1], ((0, 0), (0, LANES - HA)))
            base = jnp.broadcast_to(base[:, None, :], (DB, DT, LANES)).reshape(Ts, LANES)
            ls_tok, ls_row, slf = _cumsum(lf_tok, base, Tp, DB, DT, HA, j, NAB, slf)

            qp, pk, kpb, pv, vpb = _proj_qkv(h, w_qkv, 0, B, S, j, NAB, pk, pv)
            qs, sk, ksb, sv, vsb = _proj_qkv(h, w_qkv, Tp, DB, DT, j, NAB, sk, sv)
            oa = _fox_prompt(qp, kpb, vpb, lp_row, T)
            oa = _fox_sample(qs, cache_k, cache_v, j, ksb, vsb, ls_tok,
                             lc_row, ls_row, Tp, oa)

            assert WBK == WBV
            zb, kb = _proj_hgrn(h, w_b, hgrn_lb, j)
            ob, ps = _hgrn(zb, kb, hgrn_gnorm[j], None, 0, B, S, HB, j, NAB, None, ps)
            ob, ss = _hgrn(zb, kb, hgrn_gnorm[j], state_hgrn, Tp, DB, DT, HB, j, NAB, ob, ss)

            w_out = bf(ab_w_out[j])
            x = _mm_res([oa, ob], [w_out[:WA], w_out[WA:]], x)
        else:
            (uv,) = _proj(h, bf(c_w_in[j]), [], _epi_gelu, [F32], 512)
            y = None
            for row0, n_rows, seq in ((0, Tp, S), (Tp, Ts, DT)):
                Lc = min(seq, CL)
                ws = c_w_s[j][:, :Lc, :Lc]
                bs = jnp.repeat(c_b_s[j][:, :Lc].T, CD // G, axis=1)
                if row0 == 0:
                    (y,) = _cmix(uv, c_ln_g[j], c_ln_b[j], ws, bs, row0, n_rows, Lc, y, None)
                else:
                    y, scv = _cmix(uv, c_ln_g[j], c_ln_b[j], ws, bs, row0, n_rows, Lc, y, (j, NC, scv))
            x = _mm_res([y], [bf(c_w_out[j])], x)
        x = _ffn(x, norm_ffn2[l], ffn2_gate, ffn2_up, ffn2_down, l)

    y_p = _norm(x, norm_final, F32, 0, Tp)
    y_s = _norm(x, norm_final, F32, Tp, Ts)
    return (y_p.reshape(B, S, D), y_s.reshape(DB, DT, D), pk, pv, plf, ps,
            sk, sv, slf, ss, scv.reshape(NC, DB, DT, CD))
```

```python
import functools
import math

import jax
import jax.numpy as jnp
from jax import lax
from jax.experimental import pallas as pl
from jax.experimental.pallas import tpu as pltpu

F32 = jnp.float32
BF16 = jnp.bfloat16
EPS = 1e-6
TINY = 1e-30
MASK_VALUE = -1e30
LOG2E = math.log2(math.e)
HIGHEST = lax.Precision.HIGHEST

LANES = 128
MIB = 1 << 20
VMEM_LIMIT = 60 * MIB
FFN_TOKEN_TILE = 1024
FFN_FF_TILE = 256
PROJ_COL_TILE = 512
HGRN_HEADS_PER_STEP = 8
HGRN_HEADS_PER_GROUP = 8
FOX_SAMPLE_HEADS_PER_STEP = 2
FOX_SAMPLE_MAX_HEADS_PER_STEP = 4
FOX_SAMPLE_KEY_SPLITS = 2


def _tile(n, target):
    if n <= target:
        return n
    for t in range(target, 7, -1):
        if n % t == 0 and t % 8 == 0:
            return t
    return n


def _call(body, grid, in_specs, args, out_specs, out_shapes, sem, dsts=None, scratch=()):
    in_specs, args = list(in_specs), list(args)
    n_in = len(args)
    aliases = {}
    for k, d in enumerate(dsts or ()):
        if d is not None:
            aliases[len(args)] = k
            in_specs.append(pl.BlockSpec(memory_space=pl.ANY))
            args.append(d)
    n_all = len(args)

    def wrapped(*refs):
        body(*refs[:n_in], *refs[n_all:])

    return pl.pallas_call(
        wrapped, grid=grid, in_specs=in_specs, out_specs=out_specs, out_shape=out_shapes,
        input_output_aliases=aliases, scratch_shapes=list(scratch),
        compiler_params=pltpu.CompilerParams(dimension_semantics=sem, vmem_limit_bytes=VMEM_LIMIT),
    )(*args)


def _run_stages(stages):
    steps = stages[0]["steps"]
    assert all(st["steps"] == steps for st in stages)
    n_in = [len(st["args"]) for st in stages]
    n_out = [len(st["out_specs"]) for st in stages]
    n_scr = [len(st["scratch"]) for st in stages]

    def body(*refs):
        ins, outs, scr = refs[:sum(n_in)], refs[sum(n_in):sum(n_in) + sum(n_out)], refs[sum(n_in) + sum(n_out):]
        a = b = c = 0
        for st, ni, no, ns in zip(stages, n_in, n_out, n_scr):
            st["body"](*ins[a:a + ni], *outs[b:b + no], *scr[c:c + ns])
            a, b, c = a + ni, b + no, c + ns

    cat = lambda key: [x for st in stages for x in st[key]]
    return _call(body, (steps,), cat("in_specs"), cat("args"), cat("out_specs"), cat("out_shapes"),
                 ("arbitrary",), dsts=cat("dsts"), scratch=cat("scratch"))


def _dot(a, b):
    return jnp.dot(a, b, preferred_element_type=F32)


def _dot_nt(a, b):
    return lax.dot_general(a, b, (((1,), (1,)), ((), ())), preferred_element_type=F32)


def _dot_tn(a, b):
    return lax.dot_general(a, b, (((0,), (0,)), ((), ())), preferred_element_type=F32)


def _rms(x, g):
    return x * lax.rsqrt(jnp.mean(x * x, axis=-1, keepdims=True) + EPS) * g


def _ffn_body(x_ref, g_ref, wg_ref, wu_ref, wd_ref, *refs, next_norm):
    if next_norm:
        g2_ref, o_ref, h2_ref, h_ref = refs
    else:
        o_ref, h_ref = refs

    @pl.when(pl.program_id(1) == 0)
    def _():
        x = x_ref[...]
        h_ref[...] = _rms(x, g_ref[...]).astype(BF16)
        o_ref[...] = x

    h = h_ref[...]
    a = _dot(h, wg_ref[...].astype(BF16))
    b = _dot(h, wu_ref[...].astype(BF16))
    act = (0.5 * a * jax.nn.sigmoid(a) * b).astype(BF16)
    o_ref[...] += _dot(act, wd_ref[...].astype(BF16))

    if next_norm:
        @pl.when(pl.program_id(1) == pl.num_programs(1) - 1)
        def _():
            h2_ref[...] = _rms(o_ref[...], g2_ref[...]).astype(h2_ref.dtype)


def _ffn(x, g, wg, wu, wd, layer, g_next=None, out_rows=None, out_row0=0, dsts=(None, None)):
    T, D = x.shape
    F = wg.shape[2]
    tm = _tile(T, FFN_TOKEN_TILE)
    tf = _tile(F, FFN_FF_TILE)
    out_rows = out_rows or T
    assert out_row0 % tm == 0
    i0 = out_row0 // tm
    vec = pl.BlockSpec((1, D), lambda i, j: (0, 0))
    tok = pl.BlockSpec((tm, D), lambda i, j: (i + i0, 0))
    in_specs = [pl.BlockSpec((tm, D), lambda i, j: (i, 0)), vec,
                pl.BlockSpec((None, D, tf), lambda i, j: (layer, 0, j)),
                pl.BlockSpec((None, D, tf), lambda i, j: (layer, 0, j)),
                pl.BlockSpec((None, tf, D), lambda i, j: (layer, j, 0))]
    args = [x, g.reshape(1, D), wg, wu, wd]
    out_specs, out_shapes = [tok], [jax.ShapeDtypeStruct((out_rows, D), F32)]
    if g_next is not None:
        in_specs.append(vec)
        args.append(g_next.reshape(1, D))
        out_specs.append(tok)
        out_shapes.append(jax.ShapeDtypeStruct((out_rows, D), BF16))
    outs = _call(
        functools.partial(_ffn_body, next_norm=g_next is not None), (T // tm, F // tf),
        in_specs, args, out_specs, out_shapes,
        ("parallel", "arbitrary"), dsts=list(dsts[:len(out_specs)]),
        scratch=[pltpu.VMEM((tm, D), BF16)])
    return outs if g_next is not None else outs[0]


def _norm_body(x_ref, g_ref, o_ref):
    o_ref[...] = _rms(x_ref[...], g_ref[...]).astype(o_ref.dtype)


def _norm(x, g, dtype, row0=0, n_rows=None):
    D = x.shape[1]
    n_rows = n_rows or x.shape[0]
    tm = _tile(n_rows, 512)
    assert row0 % tm == 0
    i0 = row0 // tm
    return _call(
        _norm_body, (n_rows // tm,),
        [pl.BlockSpec((tm, D), lambda i: (i + i0, 0)), pl.BlockSpec((1, D), lambda i: (0, 0))],
        [x, g.reshape(1, D)],
        pl.BlockSpec((tm, D), lambda i: (i, 0)),
        jax.ShapeDtypeStruct((n_rows, D), dtype), ("parallel",))


def _resident(shape):
    return pl.BlockSpec(shape, lambda i: (0,) * len(shape), pipeline_mode=pl.Buffered(1))


def _col_tiles(n):
    tn = _tile(n, PROJ_COL_TILE)
    return [slice(c * tn, (c + 1) * tn) for c in range(n // tn)]


def _proj_body(h_ref, w_ref, *refs, epilogue, n_aux):
    aux, outs = refs[:n_aux], refs[n_aux:]
    h = h_ref[...]
    for cols in _col_tiles(w_ref.shape[1]):
        vals = epilogue(_dot(h, w_ref[:, cols]), *[a[:, cols] for a in aux])
        for o_ref, v in zip(outs, vals):
            o_ref[:, cols] = v.astype(o_ref.dtype)


def _proj(h, w, aux, epilogue, out_dtypes, tm_target):
    T, K = h.shape
    N = w.shape[1]
    tm = _tile(T, tm_target)
    return _call(
        functools.partial(_proj_body, epilogue=epilogue, n_aux=len(aux)),
        (T // tm,),
        [pl.BlockSpec((tm, K), lambda i: (i, 0)), _resident((K, N))] + [_resident(a.shape) for a in aux],
        [h, w, *aux],
        [pl.BlockSpec((tm, N), lambda i: (i, 0)) for _ in out_dtypes],
        [jax.ShapeDtypeStruct((T, N), d) for d in out_dtypes],
        ("parallel",))


def _proj_hgrn_body(h_ref, w_ref, lb_ref, main_ref, k_ref, *, layer):
    W = k_ref.shape[1]
    h = h_ref[...]
    for seg in range(4):
        for kc in _col_tiles(W):
            cols = slice(seg * W + kc.start, seg * W + kc.stop)
            z = _dot(h, w_ref[:, cols])
            if seg == 1:
                g, k = _epi_hgrn_forget(z, lb_ref[:, kc], layer=layer)
                main_ref[:, cols] = g
                k_ref[:, kc] = k
            else:
                main_ref[:, cols] = z if seg == 2 else z * jax.nn.sigmoid(z)


def _proj_hgrn(h, w, lb_all, layer):
    T, K = h.shape
    W = w.shape[1] // 4
    tm = _tile(T, 512)
    return _call(
        functools.partial(_proj_hgrn_body, layer=layer),
        (T // tm,),
        [pl.BlockSpec((tm, K), lambda i: (i, 0)), _resident((K, 4 * W)), _resident(lb_all.shape)],
        [h, w, lb_all],
        [pl.BlockSpec((tm, 4 * W), lambda i: (i, 0)), pl.BlockSpec((tm, W), lambda i: (i, 0))],
        [jax.ShapeDtypeStruct((T, 4 * W), F32), jax.ShapeDtypeStruct((T, W), F32)],
        ("parallel",))


def _qkv_body(h_ref, w_ref, q_ref, k_ref, kb_ref, v_ref, vb_ref, *, nb, ts):
    H = q_ref.shape[1]
    h = h_ref[...]
    targets = ((q_ref,), (k_ref, kb_ref), (v_ref, vb_ref))
    for cols in _col_tiles(w_ref.shape[1]):
        z = _dot(h, w_ref[:, cols])
        for n in range((cols.stop - cols.start) // LANES):
            head = cols.start // LANES + n
            zz = z[:, n * LANES:(n + 1) * LANES].reshape(nb, ts, LANES)
            for o_ref in targets[head // H]:
                o_ref[:, head % H] = zz.astype(o_ref.dtype)


def _proj_qkv(h, w, row0, B, S, layer, n_layers, k_dst, v_dst):
    K = h.shape[1]
    H = w.shape[1] // (3 * LANES)
    if S >= 512:
        ts, nb = _tile(S, 512), 1
    else:
        nb = _tile(B, max(1, 512 // S))
        ts = S
    tm = nb * ts
    spt = S // ts
    assert row0 % tm == 0
    i0 = row0 // tm
    if nb == 1:
        omap = lambda i: (i // spt, 0, i % spt, 0)
    else:
        omap = lambda i: (i, 0, 0, 0)
    plain = pl.BlockSpec((nb, H, ts, LANES), omap)
    slab = pl.BlockSpec((None, nb, H, ts, LANES), lambda i: (layer, *omap(i)))
    bf_shape = jax.ShapeDtypeStruct((B, H, S, LANES), BF16)
    slab_shape = jax.ShapeDtypeStruct((n_layers, B, H, S, LANES), F32)
    return _call(
        functools.partial(_qkv_body, nb=nb, ts=ts),
        (B * S // tm,),
        [pl.BlockSpec((tm, K), lambda i: (i + i0, 0)), _resident(w.shape)],
        [h, w],
        [plain, slab, plain, slab, plain],
        [bf_shape, slab_shape, bf_shape, slab_shape, bf_shape],
        ("parallel",), dsts=[None, k_dst, None, v_dst, None])


def _mm_res_body(*refs, n):
    lhs = [r[...] for r in refs[:n]]
    ws = refs[n:2 * n]
    x_ref, o_ref = refs[2 * n], refs[2 * n + 1]
    for cols in _col_tiles(x_ref.shape[1]):
        z = _dot(lhs[0], ws[0][:, cols])
        for a, w_ref in zip(lhs[1:], ws[1:]):
            z += _dot(a, w_ref[:, cols])
        o_ref[:, cols] = x_ref[:, cols] + z


def _mm_res(lhs, ws, x):
    T, N = x.shape
    n = len(lhs)
    tm = _tile(T, 512)
    tok = pl.BlockSpec((tm, N), lambda i: (i, 0))
    return _call(
        functools.partial(_mm_res_body, n=n), (T // tm,),
        ([pl.BlockSpec((tm, a.shape[1]), lambda i: (i, 0)) for a in lhs]
         + [_resident(w.shape) for w in ws] + [tok]),
        [*lhs, *ws, x],
        tok, jax.ShapeDtypeStruct((T, N), F32), ("parallel",))


def _cumsum_body(lf_ref, base_ref, ltok_ref, lrow_ref, lfrow_ref, carry_ref, *, ts, seg, tiles_per_seq, n_heads):
    lf = lf_ref[...]
    r = lax.broadcasted_iota(jnp.int32, (ts, ts), 0)
    c = lax.broadcasted_iota(jnp.int32, (ts, ts), 1)
    tri = jnp.where(r >= c, 1.0, 0.0).astype(F32)
    if seg < ts:
        tri = jnp.where(r // seg == c // seg, tri, 0.0)
    L = jnp.dot(tri, lf, precision=HIGHEST, preferred_element_type=F32)
    if tiles_per_seq:
        @pl.when(pl.program_id(0) % tiles_per_seq == 0)
        def _():
            carry_ref[...] = jnp.zeros_like(carry_ref)
        L = L + carry_ref[...]
        carry_ref[...] = L[ts - 1:ts, :]
    else:
        L = L + base_ref[...]
    ltok_ref[...] = L
    Lt = L.T[:n_heads]
    lft = lf.T[:n_heads]
    for s in range(ts // seg):
        lrow_ref[s] = Lt[:, s * seg:(s + 1) * seg]
        lfrow_ref[s] = lft[:, s * seg:(s + 1) * seg]


def _cumsum(lf_tok, base, row0, B, S, n_heads, layer, n_layers, dst):
    if S >= LANES:
        ts = _tile(S, 512)
        seg, tps, nseg = ts, S // ts, 1
    else:
        ts, seg, tps = LANES, S, 0
        nseg = ts // seg
    assert row0 % ts == 0 and (B * S) % ts == 0
    i0 = row0 // ts
    if tps:
        omap = lambda i: (i // tps, 0, i % tps)
    else:
        omap = lambda i: (i, 0, 0)
    return _call(
        functools.partial(_cumsum_body, ts=ts, seg=seg, tiles_per_seq=tps, n_heads=n_heads),
        (B * S // ts,),
        [pl.BlockSpec((ts, LANES), lambda i: (i + i0, 0)), pl.BlockSpec((ts, LANES), lambda i: (i, 0))],
        [lf_tok, base],
        [pl.BlockSpec((ts, LANES), lambda i: (i, 0)),
         pl.BlockSpec((nseg, n_heads, seg), omap),
         pl.BlockSpec((None, nseg, n_heads, seg), lambda i: (layer, *omap(i)))],
        [jax.ShapeDtypeStruct((B * S, LANES), F32), jax.ShapeDtypeStruct((B, n_heads, S), F32),
         jax.ShapeDtypeStruct((n_layers, B, n_heads, S), F32)],
        ("arbitrary",), dsts=[None, None, dst], scratch=[pltpu.VMEM((1, LANES), F32)])


def _row_cumsum_body(x_ref, o_ref):
    R, P = x_ref.shape
    r = lax.broadcasted_iota(jnp.int32, (LANES, LANES), 0)
    c = lax.broadcasted_iota(jnp.int32, (LANES, LANES), 1)
    tri = jnp.where(r <= c, 1.0, 0.0).astype(F32)
    carry = jnp.zeros((R, 1), F32)
    for s in range(P // LANES):
        y = jnp.dot(x_ref[:, s * LANES:(s + 1) * LANES], tri, precision=HIGHEST, preferred_element_type=F32) + carry
        o_ref[:, s * LANES:(s + 1) * LANES] = y
        carry = y[:, LANES - 1:LANES]


def _row_cumsum(x, row0, n_rows):
    P = x.shape[1]
    tr = _tile(n_rows, 256)
    assert row0 % tr == 0
    i0 = row0 // tr
    return _call(
        _row_cumsum_body, (n_rows // tr,),
        [pl.BlockSpec((tr, P), lambda i: (i + i0, 0))], [x],
        pl.BlockSpec((tr, P), lambda i: (i, 0)),
        jax.ShapeDtypeStruct((n_rows, P), F32), ("parallel",))


def _head_row(rows, h):
    sublane = lax.broadcasted_iota(jnp.int32, rows.shape, 0)
    return jnp.sum(jnp.where(sublane == h, rows, 0.0), axis=0, keepdims=True)


def _fox_prompt_body(q_ref, k_ref, v_ref, l_ref, o_ref, m_ref, d_ref, acc_ref, s0_ref, s1_ref, *, scale, tq, sub):
    qi = pl.program_id(2)
    c1 = scale * LOG2E
    tk = tq // 2
    n_sub = tq // sub
    every = tuple(range(n_sub))
    m_ref[...] = jnp.full_like(m_ref, MASK_VALUE)
    d_ref[...] = jnp.zeros_like(d_ref)
    acc_ref[...] = jnp.zeros_like(acc_ref)
    head = pl.program_id(1)
    l_here = _head_row(l_ref[0, :, pl.ds(pl.multiple_of(qi * tq, tq), LANES)], head)[:, 0:1]

    def scores(ki, s_ref, subs):
        k0 = pl.multiple_of(ki * tk, tk)
        k = k_ref[0, 0, pl.ds(k0, tk), :]
        bias = (l_here - _head_row(l_ref[0, :, pl.ds(k0, tk)], head)) * LOG2E
        for u in subs:
            rows = slice(u * sub, (u + 1) * sub)
            s_ref[rows] = _dot_nt(q_ref[0, 0, rows, :], k) * c1 + bias

    def update(ki, s_ref, subs, diagonal=None):
        k0 = pl.multiple_of(ki * tk, tk)
        v = v_ref[0, 0, pl.ds(k0, tk), :]
        for n in range(0, len(subs), 2):
            update_pair(s_ref, v, subs[n:n + 2], diagonal)

    def update_pair(s_ref, v, subs, diagonal):
        rows = [slice(u * sub, (u + 1) * sub) for u in subs]
        s = []
        for u, rw in zip(subs, rows):
            su = s_ref[rw]
            if diagonal is not None and u * sub < (diagonal + 1) * tk - 1:
                r = lax.broadcasted_iota(jnp.int32, su.shape, 0) + u * sub
                c = lax.broadcasted_iota(jnp.int32, su.shape, 1) + diagonal * tk
                su = jnp.where(c <= r, su, MASK_VALUE)
            s.append(su)
        m_prev = [m_ref[rw] for rw in rows]
        m_new = [jnp.maximum(mp, jnp.max(su, axis=-1, keepdims=True)) for mp, su in zip(m_prev, s)]
        alpha = [jnp.exp2(mp - mn) for mp, mn in zip(m_prev, m_new)]
        p = [jnp.exp2(su - mn) for su, mn in zip(s, m_new)]
        for n, rw in enumerate(rows):
            d_ref[rw] = alpha[n] * d_ref[rw] + jnp.sum(p[n], axis=-1, keepdims=True)
            m_ref[rw] = m_new[n]
        for n, rw in enumerate(rows):
            acc_ref[rw] = alpha[n] * acc_ref[rw] + _dot(p[n].astype(BF16), v)

    def two_blocks(t, carry):
        scores(2 * t + 1, s1_ref, every)
        update(2 * t, s0_ref, every)
        scores(2 * t + 2, s0_ref, every)
        update(2 * t + 1, s1_ref, every)
        return carry

    scores(0, s0_ref, every)
    lax.fori_loop(0, qi, two_blocks, 0)
    late = tuple(u for u in every if (u + 1) * sub > tk)
    scores(2 * qi + 1, s1_ref, late)
    update(2 * qi, s0_ref, every, diagonal=0)
    update(2 * qi + 1, s1_ref, late, diagonal=1)
    o_ref[...] = (acc_ref[...] / d_ref[...]).astype(o_ref.dtype)


def _fox_prompt(q, k, v, l_row, out_rows):
    B, H, S, D = q.shape
    tq = _tile(S, 1024)
    sub = min(256, tq // 2)
    assert tq % (2 * LANES) == 0 and (tq // 2) % sub == 0
    nq = S // tq
    full = pl.BlockSpec((1, 1, S, D), lambda b, h, qi: (b, h, 0, 0))
    return _call(
        functools.partial(_fox_prompt_body, scale=D ** -0.5, tq=tq, sub=sub),
        (B, H, nq),
        [pl.BlockSpec((1, 1, tq, D), lambda b, h, qi: (b, h, qi, 0)), full, full,
         pl.BlockSpec((1, H, S), lambda b, h, qi: (b, 0, 0))],
        [q, k, v, l_row],
        pl.BlockSpec((tq, D), lambda b, h, qi: (b * nq + qi, h)),
        jax.ShapeDtypeStruct((out_rows, H * D), BF16),
        ("parallel", "parallel", "arbitrary"),
        scratch=[pltpu.VMEM((tq, 1), F32), pltpu.VMEM((tq, 1), F32), pltpu.VMEM((tq, D), F32),
                 pltpu.VMEM((tq, tq // 2), F32), pltpu.VMEM((tq, tq // 2), F32)])


def _head_column(l_tok, h):
    lane = lax.broadcasted_iota(jnp.int32, l_tok.shape, 1)
    return jnp.sum(jnp.where(lane == h, l_tok, 0.0), axis=-1, keepdims=True)


def _fox_sample_body(q_ref, *refs, scale, hps, n_split, head_groups):
    ck_refs, cv_refs = refs[:n_split], refs[n_split:2 * n_split]
    kn_ref, vn_ref, lq_ref, lc_ref, ln_ref, o_ref = refs[2 * n_split:]
    hg = pl.program_id(0) % head_groups
    pn = ck_refs[0].shape[-2]
    D = q_ref.shape[-1]
    for hh in range(hps):
        q = q_ref[0, hh]
        lq = _head_column(lq_ref[...], hg * hps + hh)
        s_c = [_dot_nt(q, ck_refs[n][0, hh].astype(BF16)) * scale + (lq - _head_row(lc_ref[0, :, n * pn:(n + 1) * pn], hg * hps + hh))
               for n in range(n_split)]
        s_n = _dot_nt(q, kn_ref[0, hh]) * scale + (lq - _head_row(ln_ref[0], hg * hps + hh))
        r = lax.broadcasted_iota(jnp.int32, s_n.shape, 0)
        c = lax.broadcasted_iota(jnp.int32, s_n.shape, 1)
        s_n = jnp.where(c <= r, s_n, MASK_VALUE)
        m = jnp.max(s_n, axis=-1, keepdims=True)
        for s in s_c:
            m = jnp.maximum(m, jnp.max(s, axis=-1, keepdims=True))
        p_n = jnp.exp(s_n - m)
        l = jnp.sum(p_n, axis=-1, keepdims=True)
        o = _dot(p_n.astype(BF16), vn_ref[0, hh])
        for n in range(n_split):
            p = jnp.exp(s_c[n] - m)
            l = l + jnp.sum(p, axis=-1, keepdims=True)
            o = o + _dot(p.astype(BF16), cv_refs[n][0, hh].astype(BF16))
        o_ref[:, hh * D:(hh + 1) * D] = (o / l).astype(o_ref.dtype)


def _fox_sample_stage(q, cache_k, cache_v, layer, kn, vn, ln_tok, lc_row, ln_row, out_row0, dst, hps):
    B, H, T, D = q.shape
    P = cache_k.shape[3]
    assert out_row0 % T == 0 and H % hps == 0
    r0 = out_row0 // T
    ng = H // hps
    n_split = FOX_SAMPLE_KEY_SPLITS if P % (FOX_SAMPLE_KEY_SPLITS * LANES) == 0 else 1
    pn = P // n_split
    new = pl.BlockSpec((1, hps, T, D), lambda s: (s // ng, s % ng, 0, 0))
    old = [pl.BlockSpec((None, 1, hps, pn, D), lambda s, n=n: (layer, s // ng, s % ng, n, 0)) for n in range(n_split)]
    return dict(
        steps=B * ng,
        body=functools.partial(_fox_sample_body, scale=D ** -0.5, hps=hps, n_split=n_split, head_groups=ng),
        in_specs=[new, *old, *old, new, new,
                  pl.BlockSpec((T, LANES), lambda s: (s // ng, 0)),
                  pl.BlockSpec((1, H, P), lambda s: (s // ng, 0, 0)),
                  pl.BlockSpec((1, H, T), lambda s: (s // ng, 0, 0))],
        args=[q, *([cache_k] * n_split), *([cache_v] * n_split), kn, vn, ln_tok, lc_row, ln_row],
        out_specs=[pl.BlockSpec((T, hps * D), lambda s: (s // ng + r0, s % ng))],
        out_shapes=[jax.ShapeDtypeStruct(dst.shape, dst.dtype)],
        dsts=[dst], scratch=[])


def _hgrn_body(*refs, C, hpb, has_init, n_chunks):
    if has_init:
        q_ref, k_ref, g_ref, i_ref, gate_ref, gn_ref, s0_ref, o_ref, sf_ref, st_ref = refs
    else:
        q_ref, k_ref, g_ref, i_ref, gate_ref, gn_ref, o_ref, sf_ref, st_ref = refs
        s0_ref = None
    c = pl.program_id(0) % n_chunks

    @pl.when(c == 0)
    def _():
        for hh in range(hpb):
            st_ref[hh] = s0_ref[0, hh].T if has_init else jnp.zeros((LANES, LANES), F32)

    row = lax.broadcasted_iota(jnp.int32, (C, C), 0)
    col = lax.broadcasted_iota(jnp.int32, (C, C), 1)
    tri = jnp.where(row >= col, 1.0, 0.0).astype(F32)
    trow = lax.broadcasted_iota(jnp.int32, (C, LANES), 0)
    level = 31 - lax.clz(jnp.where(row > col, row ^ col, 0))
    level = jnp.where(row == col, C.bit_length() - 1, level)

    def recur(heads):
        sls = {hh: slice(hh * LANES, (hh + 1) * LANES) for hh in heads}
        q = {hh: q_ref[:, sls[hh]] for hh in heads}
        k = {hh: k_ref[:, sls[hh]] for hh in heads}
        v = {hh: i_ref[:, sls[hh]].astype(BF16) for hh in heads}
        g = {hh: g_ref[:, sls[hh]] * LOG2E for hh in heads}
        G = {hh: jnp.dot(tri, g[hh], precision=HIGHEST, preferred_element_type=F32) for hh in heads}
        st = {hh: st_ref[hh] for hh in heads}
        o = {hh: _dot_nt((q[hh] * jnp.exp2(G[hh])).astype(BF16), st[hh].astype(BF16)) for hh in heads}
        A = {hh: jnp.where(level == C.bit_length() - 1, _dot_nt(q[hh].astype(BF16), k[hh].astype(BF16)), 0.0)
             for hh in heads}
        yq = {hh: G[hh] - g[hh] for hh in heads}
        yk = dict(G)
        w, log_w = 1, 0
        while w < C:
            upper = (trow & w) != 0
            for hh in heads:
                qt = (q[hh] * jnp.exp2(G[hh] - yq[hh])).astype(BF16)
                kt = (k[hh] * jnp.exp2(yk[hh] - G[hh])).astype(BF16)
                A[hh] = jnp.where(level == log_w, _dot_nt(qt, kt), A[hh])
                yq[hh] = jnp.where(upper, pltpu.roll(yq[hh], w, 0), yq[hh])
                yk[hh] = jnp.where(upper, yk[hh], pltpu.roll(yk[hh], C - w, 0))
            w, log_w = 2 * w, log_w + 1
        for hh in heads:
            out = o[hh] + _dot(A[hh].astype(BF16), v[hh])
            g_last = G[hh][C - 1:C, :]
            ks = (k[hh] * jnp.exp2(g_last - G[hh])).astype(BF16)
            st_ref[hh] = st[hh] * jnp.exp2(g_last) + _dot_tn(v[hh], ks)
            o_ref[:, sls[hh]] = (_rms(out, gn_ref[...]) * gate_ref[:, sls[hh]]).astype(o_ref.dtype)

    group = math.gcd(hpb, HGRN_HEADS_PER_GROUP)
    for h0 in range(0, hpb, group):
        recur(range(h0, h0 + group))

    @pl.when(c == n_chunks - 1)
    def _():
        for hh in range(hpb):
            sf_ref[0, hh] = st_ref[hh].T


def _hgrn_stage(main, k, gnorm, s0, row0, B, S, H, layer, n_layers, o_dst, s_dst):
    T = k.shape[0]
    C = min(S, LANES)
    nc = S // C
    assert row0 % C == 0
    r0 = row0 // C
    hpb = math.gcd(H, HGRN_HEADS_PER_STEP)
    nhb = H // hpb
    seq = lambda s: s // (nhb * nc)
    hblk = lambda s: (s // nc) % nhb
    row = lambda s: r0 + seq(s) * nc + s % nc
    tok = pl.BlockSpec((C, hpb * LANES), lambda s: (row(s), hblk(s)))
    seg = lambda n: pl.BlockSpec((C, hpb * LANES), lambda s: (row(s), n * nhb + hblk(s)))
    state = pl.BlockSpec((None, 1, hpb, LANES, LANES), lambda s: (layer, seq(s), hblk(s), 0, 0))
    in_specs = [seg(0), tok, seg(1), seg(2), seg(3), pl.BlockSpec((1, LANES), lambda s: (0, 0))]
    args = [main, k, main, main, main, gnorm.reshape(1, LANES)]
    if s0 is not None:
        in_specs.append(state)
        args.append(s0)
    return dict(
        steps=B * nhb * nc,
        body=functools.partial(_hgrn_body, C=C, hpb=hpb, has_init=s0 is not None, n_chunks=nc),
        in_specs=in_specs, args=args, out_specs=[tok, state],
        out_shapes=[jax.ShapeDtypeStruct((T, H * LANES), BF16),
                    jax.ShapeDtypeStruct((n_layers, B, H, LANES, LANES), F32)],
        dsts=[o_dst, s_dst], scratch=[pltpu.VMEM((hpb, LANES, LANES), F32)])


def _cmix_body(u_ref, v_ref, lng_ref, lnb_ref, ws_ref, bs_ref, y_ref, *vout, groups):
    v = v_ref[...]
    mu = jnp.mean(v, axis=-1, keepdims=True)
    xc = v - mu
    var = jnp.mean(xc * xc, axis=-1, keepdims=True)
    vn = xc * lax.rsqrt(var + EPS) * lng_ref[...] + lnb_ref[...]
    if vout:
        vout[0][...] = vn
    vb = vn.astype(BF16)
    Lc, Cd = v.shape
    cg = Cd // groups
    r = lax.broadcasted_iota(jnp.int32, (Lc, Lc), 0)
    c = lax.broadcasted_iota(jnp.int32, (Lc, Lc), 1)
    for gi in range(groups):
        sl = slice(gi * cg, (gi + 1) * cg)
        w = jnp.where(c <= r, ws_ref[gi], 0.0).astype(BF16)
        mixed = _dot(w, vb[:, sl]) + bs_ref[:, sl]
        y_ref[:, sl] = (u_ref[:, sl] * mixed).astype(y_ref.dtype)


def _cmix(uv, ln_g, ln_b, ws, bs, row0, n_rows, Lc, y_dst, v_out):
    T, Cd = uv.shape[0], uv.shape[1] // 2
    G = ws.shape[0]
    assert row0 % Lc == 0
    r0 = row0 // Lc
    tok = pl.BlockSpec((Lc, Cd), lambda i: (i + r0, 0))
    tok_v = pl.BlockSpec((Lc, Cd), lambda i: (i + r0, 1))
    vec = pl.BlockSpec((1, Cd), lambda i: (0, 0))
    out_specs, out_shapes, dsts = [tok], [jax.ShapeDtypeStruct((T, Cd), BF16)], [y_dst]
    if v_out:
        layer, n_layers, dst = v_out
        out_specs.append(pl.BlockSpec((None, Lc, Cd), lambda i: (layer, i, 0)))
        out_shapes.append(jax.ShapeDtypeStruct((n_layers, n_rows, Cd), F32))
        dsts.append(dst)
    return _call(
        functools.partial(_cmix_body, groups=G), (n_rows // Lc,),
        [tok, tok_v, vec, vec, pl.BlockSpec((G, Lc, Lc), lambda i: (0, 0, 0)),
         pl.BlockSpec((Lc, Cd), lambda i: (0, 0))],
        [uv, uv, ln_g.reshape(1, Cd), ln_b.reshape(1, Cd), ws, bs],
        out_specs, out_shapes, ("parallel",), dsts=dsts)


def _epi_gelu(z):
    return (jax.nn.gelu(z),)


def _epi_logsig(z, b):
    return (jax.nn.log_sigmoid(z + b),)


def _epi_hgrn_forget(z, lb_all, *, layer):
    rows = [lb_all[n:n + 1] for n in range(lb_all.shape[0])]
    top = functools.reduce(jnp.maximum, rows)
    e = [jnp.exp(r - top) for r in rows]
    total = functools.reduce(jnp.add, e)
    sm = [a / total for a in e]
    lb = functools.reduce(jnp.add, sm[:layer + 1]) - sm[0]
    f = lb + (1.0 - lb) * jax.nn.sigmoid(z)
    g = jnp.log(jnp.maximum(f, TINY))
    k = (1.0 - lb) * jax.nn.sigmoid(-z)
    return g, k


def kernel(x_prompt, x_sample, cache_k, cache_v, cache_logf, state_hgrn, norm_ffn1, ffn1_gate, ffn1_up, ffn1_down, norm_mix, ab_w_in, ab_b_f, hgrn_lb, hgrn_gnorm, ab_w_out, c_w_in, c_ln_g, c_ln_b, c_w_s, c_b_s, c_w_out, norm_ffn2, ffn2_gate, ffn2_up, ffn2_down, norm_final):
    B, S, D = x_prompt.shape
    DB, DT, _ = x_sample.shape
    depth = norm_ffn1.shape[0]
    NAB, NC = ab_w_in.shape[0], c_w_in.shape[0]
    HA, P, HD = cache_k.shape[2], cache_k.shape[3], cache_k.shape[4]
    HB, DK, DV = state_hgrn.shape[2], state_hgrn.shape[3], state_hgrn.shape[4]
    assert HD == LANES and DK == LANES and DV == LANES and HA <= 8
    WA, WBK, WBV = HA * HD, HB * DK, HB * DV
    G, CL = c_w_s.shape[1], c_w_s.shape[2]
    CD = c_w_out.shape[1]
    Tp, Ts = B * S, DB * DT
    T = Tp + Ts
    bf = lambda a: a.astype(BF16)

    pk = pv = plf = ps = sk = sv = slf = ss = scv = None
    x = None
    for l in range(depth):
        j = l // 2
        w1 = (norm_ffn1[l], ffn1_gate, ffn1_up, ffn1_down, l, norm_mix[l])
        if l == 0:
            x, h = _ffn(x_prompt.reshape(Tp, D), *w1, out_rows=T)
            x, h = _ffn(x_sample.reshape(Ts, D), *w1, out_rows=T, out_row0=Tp, dsts=(x, h))
        else:
            x, h = _ffn(x, *w1)
        if l % 2 == 0:
            w_in = ab_w_in[j]
            o1 = 3 * WA + HA
            w_qkv = bf(w_in[:, :3 * WA])
            w_f = bf(jnp.pad(w_in[:, 3 * WA:o1], ((0, 0), (0, LANES - HA))))
            w_b = bf(w_in[:, o1:])
            b_f = jnp.pad(ab_b_f[j], (0, LANES - HA)).reshape(1, LANES)

            (lf_tok,) = _proj(h, w_f, [b_f], _epi_logsig, [F32], 1024)
            _, lp_row, plf = _cumsum(lf_tok, jnp.zeros((Tp, LANES), F32), 0, B, S, HA, j, NAB, plf)
            lc_row = _row_cumsum(cache_logf.reshape(NAB * DB * HA, P), j * DB * HA, DB * HA).reshape(DB, HA, P)
            base = jnp.pad(lc_row[:, :, P - 1], ((0, 0), (0, LANES - HA)))
            base = jnp.broadcast_to(base[:, None, :], (DB, DT, LANES)).reshape(Ts, LANES)
            ls_tok, ls_row, slf = _cumsum(lf_tok, base, Tp, DB, DT, HA, j, NAB, slf)

            qp, pk, kpb, pv, vpb = _proj_qkv(h, w_qkv, 0, B, S, j, NAB, pk, pv)
            qs, sk, ksb, sv, vsb = _proj_qkv(h, w_qkv, Tp, DB, DT, j, NAB, sk, sv)
            oa = _fox_prompt(qp, kpb, vpb, lp_row, T)

            assert WBK == WBV
            zb, kb = _proj_hgrn(h, w_b, hgrn_lb, j)
            rec_p = _hgrn_stage(zb, kb, hgrn_gnorm[j], None, 0, B, S, HB, j, NAB, None, ps)
            fox_s = functools.partial(_fox_sample_stage, qs, cache_k, cache_v, j, ksb, vsb, ls_tok,
                                      lc_row, ls_row, Tp, oa)
            hps = DB * HA // rec_p["steps"]
            if 1 <= hps <= FOX_SAMPLE_MAX_HEADS_PER_STEP and HA % hps == 0 and DB * (HA // hps) == rec_p["steps"]:
                ob, ps, oa = _run_stages([rec_p, fox_s(hps)])
            else:
                ob, ps = _run_stages([rec_p])
                (oa,) = _run_stages([fox_s(math.gcd(HA, FOX_SAMPLE_HEADS_PER_STEP))])
            ob, ss = _run_stages([_hgrn_stage(zb, kb, hgrn_gnorm[j], state_hgrn, Tp, DB, DT, HB, j, NAB, ob, ss)])

            w_out = bf(ab_w_out[j])
            x = _mm_res([oa, ob], [w_out[:WA], w_out[WA:]], x)
        else:
            (uv,) = _proj(h, bf(c_w_in[j]), [], _epi_gelu, [F32], 512)
            y = None
            for row0, n_rows, seq in ((0, Tp, S), (Tp, Ts, DT)):
                Lc = min(seq, CL)
                ws = c_w_s[j][:, :Lc, :Lc]
                bs = jnp.repeat(c_b_s[j][:, :Lc].T, CD // G, axis=1)
                if row0 == 0:
                    (y,) = _cmix(uv, c_ln_g[j], c_ln_b[j], ws, bs, row0, n_rows, Lc, y, None)
                else:
                    y, scv = _cmix(uv, c_ln_g[j], c_ln_b[j], ws, bs, row0, n_rows, Lc, y, (j, NC, scv))
            x = _mm_res([y], [bf(c_w_out[j])], x)
        x = _ffn(x, norm_ffn2[l], ffn2_gate, ffn2_up, ffn2_down, l)

    y_p = _norm(x, norm_final, F32, 0, Tp)
    y_s = _norm(x, norm_final, F32, Tp, Ts)
    return (y_p.reshape(B, S, D), y_s.reshape(DB, DT, D), pk, pv, plf, ps,
            sk, sv, slf, ss, scv.reshape(NC, DB, DT, CD))
```

```python
import functools
import math

import jax
import jax.numpy as jnp
from jax import lax
from jax.experimental import pallas as pl
from jax.experimental.pallas import tpu as pltpu

F32 = jnp.float32
BF16 = jnp.bfloat16
EPS = 1e-6
TINY = 1e-30
MASK_VALUE = -1e30
LOG2E = math.log2(math.e)
HIGHEST = lax.Precision.HIGHEST

LANES = 128
MIB = 1 << 20
VMEM_LIMIT = 60 * MIB
FFN_TOKEN_TILE = 1024
FFN_FF_TILE = 256
PROJ_COL_TILE = 512
HGRN_HEADS_PER_STEP = 8
HGRN_HEADS_PER_GROUP = 8
CMIX_CHUNKS_PER_STEP = 4
FOX_SAMPLE_HEADS_PER_STEP = 2
FOX_SAMPLE_MAX_HEADS_PER_STEP = 4
FOX_SAMPLE_KEY_SPLITS = 2


def _tile(n, target):
    if n <= target:
        return n
    for t in range(target, 7, -1):
        if n % t == 0 and t % 8 == 0:
            return t
    return n


def _call(body, grid, in_specs, args, out_specs, out_shapes, sem, dsts=None, scratch=()):
    in_specs, args = list(in_specs), list(args)
    n_in = len(args)
    aliases = {}
    for k, d in enumerate(dsts or ()):
        if d is not None:
            aliases[len(args)] = k
            in_specs.append(pl.BlockSpec(memory_space=pl.ANY))
            args.append(d)
    n_all = len(args)

    def wrapped(*refs):
        body(*refs[:n_in], *refs[n_all:])

    return pl.pallas_call(
        wrapped, grid=grid, in_specs=in_specs, out_specs=out_specs, out_shape=out_shapes,
        input_output_aliases=aliases, scratch_shapes=list(scratch),
        compiler_params=pltpu.CompilerParams(dimension_semantics=sem, vmem_limit_bytes=VMEM_LIMIT),
    )(*args)


def _run_stages(stages):
    steps = stages[0]["steps"]
    assert all(st["steps"] == steps for st in stages)
    n_in = [len(st["args"]) for st in stages]
    n_out = [len(st["out_specs"]) for st in stages]
    n_scr = [len(st["scratch"]) for st in stages]

    def body(*refs):
        ins, outs, scr = refs[:sum(n_in)], refs[sum(n_in):sum(n_in) + sum(n_out)], refs[sum(n_in) + sum(n_out):]
        a = b = c = 0
        for st, ni, no, ns in zip(stages, n_in, n_out, n_scr):
            st["body"](*ins[a:a + ni], *outs[b:b + no], *scr[c:c + ns])
            a, b, c = a + ni, b + no, c + ns

    cat = lambda key: [x for st in stages for x in st[key]]
    return _call(body, (steps,), cat("in_specs"), cat("args"), cat("out_specs"), cat("out_shapes"),
                 ("arbitrary",), dsts=cat("dsts"), scratch=cat("scratch"))


def _dot(a, b):
    return jnp.dot(a, b, preferred_element_type=F32)


def _dot_nt(a, b):
    return lax.dot_general(a, b, (((1,), (1,)), ((), ())), preferred_element_type=F32)


def _dot_tn(a, b):
    return lax.dot_general(a, b, (((0,), (0,)), ((), ())), preferred_element_type=F32)


def _rms(x, g):
    return x * lax.rsqrt(jnp.mean(x * x, axis=-1, keepdims=True) + EPS) * g


def _ffn_body(x_ref, g_ref, wg_ref, wu_ref, wd_ref, *refs, next_norm):
    if next_norm:
        g2_ref, o_ref, h2_ref, h_ref = refs
    else:
        o_ref, h_ref = refs

    @pl.when(pl.program_id(1) == 0)
    def _():
        x = x_ref[...]
        h_ref[...] = _rms(x, g_ref[...]).astype(BF16)
        o_ref[...] = x

    h = h_ref[...]
    a = _dot(h, wg_ref[...].astype(BF16))
    b = _dot(h, wu_ref[...].astype(BF16))
    act = (0.5 * a * jax.nn.sigmoid(a) * b).astype(BF16)
    o_ref[...] += _dot(act, wd_ref[...].astype(BF16))

    if next_norm:
        @pl.when(pl.program_id(1) == pl.num_programs(1) - 1)
        def _():
            h2_ref[...] = _rms(o_ref[...], g2_ref[...]).astype(h2_ref.dtype)


def _ffn(x, g, wg, wu, wd, layer, g_next=None, out_rows=None, out_row0=0, dsts=(None, None)):
    T, D = x.shape
    F = wg.shape[2]
    tm = _tile(T, FFN_TOKEN_TILE)
    tf = _tile(F, FFN_FF_TILE)
    out_rows = out_rows or T
    assert out_row0 % tm == 0
    i0 = out_row0 // tm
    vec = pl.BlockSpec((1, D), lambda i, j: (0, 0))
    tok = pl.BlockSpec((tm, D), lambda i, j: (i + i0, 0))
    in_specs = [pl.BlockSpec((tm, D), lambda i, j: (i, 0)), vec,
                pl.BlockSpec((None, D, tf), lambda i, j: (layer, 0, j)),
                pl.BlockSpec((None, D, tf), lambda i, j: (layer, 0, j)),
                pl.BlockSpec((None, tf, D), lambda i, j: (layer, j, 0))]
    args = [x, g.reshape(1, D), wg, wu, wd]
    out_specs, out_shapes = [tok], [jax.ShapeDtypeStruct((out_rows, D), F32)]
    if g_next is not None:
        in_specs.append(vec)
        args.append(g_next.reshape(1, D))
        out_specs.append(tok)
        out_shapes.append(jax.ShapeDtypeStruct((out_rows, D), BF16))
    outs = _call(
        functools.partial(_ffn_body, next_norm=g_next is not None), (T // tm, F // tf),
        in_specs, args, out_specs, out_shapes,
        ("parallel", "arbitrary"), dsts=list(dsts[:len(out_specs)]),
        scratch=[pltpu.VMEM((tm, D), BF16)])
    return outs if g_next is not None else outs[0]


def _norm_body(x_ref, g_ref, o_ref):
    o_ref[...] = _rms(x_ref[...], g_ref[...]).astype(o_ref.dtype)


def _norm(x, g, dtype, row0=0, n_rows=None):
    D = x.shape[1]
    n_rows = n_rows or x.shape[0]
    tm = _tile(n_rows, 512)
    assert row0 % tm == 0
    i0 = row0 // tm
    return _call(
        _norm_body, (n_rows // tm,),
        [pl.BlockSpec((tm, D), lambda i: (i + i0, 0)), pl.BlockSpec((1, D), lambda i: (0, 0))],
        [x, g.reshape(1, D)],
        pl.BlockSpec((tm, D), lambda i: (i, 0)),
        jax.ShapeDtypeStruct((n_rows, D), dtype), ("parallel",))


def _resident(shape):
    return pl.BlockSpec(shape, lambda i: (0,) * len(shape), pipeline_mode=pl.Buffered(1))


def _col_tiles(n):
    tn = _tile(n, PROJ_COL_TILE)
    return [slice(c * tn, (c + 1) * tn) for c in range(n // tn)]


def _proj_body(h_ref, w_ref, *refs, epilogue, n_aux):
    aux, outs = refs[:n_aux], refs[n_aux:]
    h = h_ref[...]
    for cols in _col_tiles(w_ref.shape[1]):
        vals = epilogue(_dot(h, w_ref[:, cols]), *[a[:, cols] for a in aux])
        for o_ref, v in zip(outs, vals):
            o_ref[:, cols] = v.astype(o_ref.dtype)


def _proj(h, w, aux, epilogue, out_dtypes, tm_target):
    T, K = h.shape
    N = w.shape[1]
    tm = _tile(T, tm_target)
    return _call(
        functools.partial(_proj_body, epilogue=epilogue, n_aux=len(aux)),
        (T // tm,),
        [pl.BlockSpec((tm, K), lambda i: (i, 0)), _resident((K, N))] + [_resident(a.shape) for a in aux],
        [h, w, *aux],
        [pl.BlockSpec((tm, N), lambda i: (i, 0)) for _ in out_dtypes],
        [jax.ShapeDtypeStruct((T, N), d) for d in out_dtypes],
        ("parallel",))


def _proj_hgrn_body(h_ref, w_ref, lb_ref, main_ref, k_ref, *, layer):
    W = k_ref.shape[1]
    h = h_ref[...]
    for seg in range(4):
        for kc in _col_tiles(W):
            cols = slice(seg * W + kc.start, seg * W + kc.stop)
            z = _dot(h, w_ref[:, cols])
            if seg == 1:
                g, k = _epi_hgrn_forget(z, lb_ref[:, kc], layer=layer)
                main_ref[:, cols] = g
                k_ref[:, kc] = k
            else:
                main_ref[:, cols] = z if seg == 2 else z * jax.nn.sigmoid(z)


def _proj_hgrn(h, w, lb_all, layer):
    T, K = h.shape
    W = w.shape[1] // 4
    tm = _tile(T, 512)
    return _call(
        functools.partial(_proj_hgrn_body, layer=layer),
        (T // tm,),
        [pl.BlockSpec((tm, K), lambda i: (i, 0)), _resident((K, 4 * W)), _resident(lb_all.shape)],
        [h, w, lb_all],
        [pl.BlockSpec((tm, 4 * W), lambda i: (i, 0)), pl.BlockSpec((tm, W), lambda i: (i, 0))],
        [jax.ShapeDtypeStruct((T, 4 * W), F32), jax.ShapeDtypeStruct((T, W), F32)],
        ("parallel",))


def _qkv_body(h_ref, w_ref, bf_ref, q_ref, k_ref, kb_ref, v_ref, vb_ref, lf_ref, *, nb, ts):
    H = q_ref.shape[1]
    h = h_ref[...]
    gate_cols = slice(3 * H * LANES, 3 * H * LANES + LANES)
    lf_ref[...] = jax.nn.log_sigmoid(_dot(h, w_ref[:, gate_cols]) + bf_ref[...])
    targets = ((q_ref,), (k_ref, kb_ref), (v_ref, vb_ref))
    for cols in _col_tiles(3 * H * LANES):
        z = _dot(h, w_ref[:, cols])
        for n in range((cols.stop - cols.start) // LANES):
            head = cols.start // LANES + n
            zz = z[:, n * LANES:(n + 1) * LANES].reshape(nb, ts, LANES)
            for o_ref in targets[head // H]:
                o_ref[:, head % H] = zz.astype(o_ref.dtype)


def _proj_qkv_stage(h, w, b_f, row0, B, S, layer, n_layers, k_dst, v_dst, lf_dst, rows_per_step=512):
    T, K = h.shape
    H = (w.shape[1] - LANES) // (3 * LANES)
    if S >= rows_per_step:
        ts, nb = _tile(S, rows_per_step), 1
    else:
        nb = _tile(B, max(1, rows_per_step // S))
        ts = S
    tm = nb * ts
    spt = S // ts
    assert row0 % tm == 0
    i0 = row0 // tm
    if nb == 1:
        omap = lambda i: (i // spt, 0, i % spt, 0)
    else:
        omap = lambda i: (i, 0, 0, 0)
    plain = pl.BlockSpec((nb, H, ts, LANES), omap)
    slab = pl.BlockSpec((None, nb, H, ts, LANES), lambda i: (layer, *omap(i)))
    bf_shape = jax.ShapeDtypeStruct((B, H, S, LANES), BF16)
    slab_shape = jax.ShapeDtypeStruct((n_layers, B, H, S, LANES), F32)
    return dict(
        steps=B * S // tm, body=functools.partial(_qkv_body, nb=nb, ts=ts),
        in_specs=[pl.BlockSpec((tm, K), lambda i: (i + i0, 0)), _resident(w.shape), _resident(b_f.shape)],
        args=[h, w, b_f],
        out_specs=[plain, slab, plain, slab, plain, pl.BlockSpec((tm, LANES), lambda i: (i + i0, 0))],
        out_shapes=[bf_shape, slab_shape, bf_shape, slab_shape, bf_shape, jax.ShapeDtypeStruct((T, LANES), F32)],
        dsts=[None, k_dst, None, v_dst, None, lf_dst], scratch=[])


def _mm_res_body(*refs, n):
    lhs = [r[...] for r in refs[:n]]
    ws = refs[n:2 * n]
    x_ref, o_ref = refs[2 * n], refs[2 * n + 1]
    for cols in _col_tiles(x_ref.shape[1]):
        z = _dot(lhs[0], ws[0][:, cols])
        for a, w_ref in zip(lhs[1:], ws[1:]):
            z += _dot(a, w_ref[:, cols])
        o_ref[:, cols] = x_ref[:, cols] + z


def _mm_res(lhs, ws, x):
    T, N = x.shape
    n = len(lhs)
    tm = _tile(T, 512)
    tok = pl.BlockSpec((tm, N), lambda i: (i, 0))
    return _call(
        functools.partial(_mm_res_body, n=n), (T // tm,),
        ([pl.BlockSpec((tm, a.shape[1]), lambda i: (i, 0)) for a in lhs]
         + [_resident(w.shape) for w in ws] + [tok]),
        [*lhs, *ws, x],
        tok, jax.ShapeDtypeStruct((T, N), F32), ("parallel",))


def _cumsum_body(lf_ref, base_ref, ltok_ref, lrow_ref, lfrow_ref, carry_ref, *, ts, seg, tiles_per_seq, n_heads):
    lf = lf_ref[...]
    r = lax.broadcasted_iota(jnp.int32, (ts, ts), 0)
    c = lax.broadcasted_iota(jnp.int32, (ts, ts), 1)
    tri = jnp.where(r >= c, 1.0, 0.0).astype(F32)
    if seg < ts:
        tri = jnp.where(r // seg == c // seg, tri, 0.0)
    L = jnp.dot(tri, lf, precision=HIGHEST, preferred_element_type=F32)
    if tiles_per_seq:
        @pl.when(pl.program_id(0) % tiles_per_seq == 0)
        def _():
            carry_ref[...] = jnp.zeros_like(carry_ref)
        L = L + carry_ref[...]
        carry_ref[...] = L[ts - 1:ts, :]
    else:
        L = L + base_ref[...]
    ltok_ref[...] = L
    Lt = L.T[:n_heads]
    lft = lf.T[:n_heads]
    for s in range(ts // seg):
        lrow_ref[s] = Lt[:, s * seg:(s + 1) * seg]
        lfrow_ref[s] = lft[:, s * seg:(s + 1) * seg]


def _cumsum(lf_tok, base, row0, B, S, n_heads, layer, n_layers, dst):
    if S >= LANES:
        ts = _tile(S, 512)
        seg, tps, nseg = ts, S // ts, 1
    else:
        ts, seg, tps = LANES, S, 0
        nseg = ts // seg
    assert row0 % ts == 0 and (B * S) % ts == 0
    i0 = row0 // ts
    if tps:
        omap = lambda i: (i // tps, 0, i % tps)
    else:
        omap = lambda i: (i, 0, 0)
    return _call(
        functools.partial(_cumsum_body, ts=ts, seg=seg, tiles_per_seq=tps, n_heads=n_heads),
        (B * S // ts,),
        [pl.BlockSpec((ts, LANES), lambda i: (i + i0, 0)), pl.BlockSpec((ts, LANES), lambda i: (i, 0))],
        [lf_tok, base],
        [pl.BlockSpec((ts, LANES), lambda i: (i, 0)),
         pl.BlockSpec((nseg, n_heads, seg), omap),
         pl.BlockSpec((None, nseg, n_heads, seg), lambda i: (layer, *omap(i)))],
        [jax.ShapeDtypeStruct((B * S, LANES), F32), jax.ShapeDtypeStruct((B, n_heads, S), F32),
         jax.ShapeDtypeStruct((n_layers, B, n_heads, S), F32)],
        ("arbitrary",), dsts=[None, None, dst], scratch=[pltpu.VMEM((1, LANES), F32)])


def _row_cumsum_body(x_ref, o_ref):
    R, P = x_ref.shape
    r = lax.broadcasted_iota(jnp.int32, (LANES, LANES), 0)
    c = lax.broadcasted_iota(jnp.int32, (LANES, LANES), 1)
    tri = jnp.where(r <= c, 1.0, 0.0).astype(F32)
    carry = jnp.zeros((R, 1), F32)
    for s in range(P // LANES):
        y = jnp.dot(x_ref[:, s * LANES:(s + 1) * LANES], tri, precision=HIGHEST, preferred_element_type=F32) + carry
        o_ref[:, s * LANES:(s + 1) * LANES] = y
        carry = y[:, LANES - 1:LANES]


def _row_cumsum(x, row0, n_rows):
    P = x.shape[1]
    tr = _tile(n_rows, 256)
    assert row0 % tr == 0
    i0 = row0 // tr
    return _call(
        _row_cumsum_body, (n_rows // tr,),
        [pl.BlockSpec((tr, P), lambda i: (i + i0, 0))], [x],
        pl.BlockSpec((tr, P), lambda i: (i, 0)),
        jax.ShapeDtypeStruct((n_rows, P), F32), ("parallel",))


def _head_row(rows, h):
    sublane = lax.broadcasted_iota(jnp.int32, rows.shape, 0)
    return jnp.sum(jnp.where(sublane == h, rows, 0.0), axis=0, keepdims=True)


def _fox_prompt_body(q_ref, k_ref, v_ref, l_ref, o_ref, m_ref, d_ref, acc_ref, s0_ref, s1_ref, *, scale, tq, sub):
    qi = pl.program_id(2)
    c1 = scale * LOG2E
    tk = tq // 2
    n_sub = tq // sub
    every = tuple(range(n_sub))
    m_ref[...] = jnp.full_like(m_ref, MASK_VALUE)
    d_ref[...] = jnp.zeros_like(d_ref)
    acc_ref[...] = jnp.zeros_like(acc_ref)
    head = pl.program_id(1)
    l_here = _head_row(l_ref[0, :, pl.ds(pl.multiple_of(qi * tq, tq), LANES)], head)[:, 0:1]

    def scores(ki, s_ref, subs):
        k0 = pl.multiple_of(ki * tk, tk)
        k = k_ref[0, 0, pl.ds(k0, tk), :]
        bias = (l_here - _head_row(l_ref[0, :, pl.ds(k0, tk)], head)) * LOG2E
        for u in subs:
            rows = slice(u * sub, (u + 1) * sub)
            s_ref[rows] = _dot_nt(q_ref[0, 0, rows, :], k) * c1 + bias

    def update(ki, s_ref, subs, diagonal=None):
        k0 = pl.multiple_of(ki * tk, tk)
        v = v_ref[0, 0, pl.ds(k0, tk), :]
        for n in range(0, len(subs), 2):
            update_pair(s_ref, v, subs[n:n + 2], diagonal)

    def update_pair(s_ref, v, subs, diagonal):
        rows = [slice(u * sub, (u + 1) * sub) for u in subs]
        s = []
        for u, rw in zip(subs, rows):
            su = s_ref[rw]
            if diagonal is not None and u * sub < (diagonal + 1) * tk - 1:
                r = lax.broadcasted_iota(jnp.int32, su.shape, 0) + u * sub
                c = lax.broadcasted_iota(jnp.int32, su.shape, 1) + diagonal * tk
                su = jnp.where(c <= r, su, MASK_VALUE)
            s.append(su)
        m_prev = [m_ref[rw] for rw in rows]
        m_new = [jnp.maximum(mp, jnp.max(su, axis=-1, keepdims=True)) for mp, su in zip(m_prev, s)]
        alpha = [jnp.exp2(mp - mn) for mp, mn in zip(m_prev, m_new)]
        p = [jnp.exp2(su - mn) for su, mn in zip(s, m_new)]
        for n, rw in enumerate(rows):
            d_ref[rw] = alpha[n] * d_ref[rw] + jnp.sum(p[n], axis=-1, keepdims=True)
            m_ref[rw] = m_new[n]
        for n, rw in enumerate(rows):
            acc_ref[rw] = alpha[n] * acc_ref[rw] + _dot(p[n].astype(BF16), v)

    def two_blocks(t, carry):
        scores(2 * t + 1, s1_ref, every)
        update(2 * t, s0_ref, every)
        scores(2 * t + 2, s0_ref, every)
        update(2 * t + 1, s1_ref, every)
        return carry

    scores(0, s0_ref, every)
    lax.fori_loop(0, qi, two_blocks, 0)
    late = tuple(u for u in every if (u + 1) * sub > tk)
    scores(2 * qi + 1, s1_ref, late)
    update(2 * qi, s0_ref, every, diagonal=0)
    update(2 * qi + 1, s1_ref, late, diagonal=1)
    o_ref[...] = (acc_ref[...] / d_ref[...]).astype(o_ref.dtype)


def _fox_prompt(q, k, v, l_row, out_rows, dst):
    B, H, S, D = q.shape
    tq = _tile(S, 1024)
    sub = min(256, tq // 2)
    assert tq % (2 * LANES) == 0 and (tq // 2) % sub == 0
    nq = S // tq
    full = pl.BlockSpec((1, 1, S, D), lambda b, h, qi: (b, h, 0, 0))
    return _call(
        functools.partial(_fox_prompt_body, scale=D ** -0.5, tq=tq, sub=sub),
        (B, H, nq),
        [pl.BlockSpec((1, 1, tq, D), lambda b, h, qi: (b, h, qi, 0)), full, full,
         pl.BlockSpec((1, H, S), lambda b, h, qi: (b, 0, 0))],
        [q, k, v, l_row],
        pl.BlockSpec((tq, D), lambda b, h, qi: (b * nq + qi, h)),
        jax.ShapeDtypeStruct((out_rows, H * D), BF16),
        ("parallel", "parallel", "arbitrary"), dsts=[dst],
        scratch=[pltpu.VMEM((tq, 1), F32), pltpu.VMEM((tq, 1), F32), pltpu.VMEM((tq, D), F32),
                 pltpu.VMEM((tq, tq // 2), F32), pltpu.VMEM((tq, tq // 2), F32)])


def _head_column(l_tok, h):
    lane = lax.broadcasted_iota(jnp.int32, l_tok.shape, 1)
    return jnp.sum(jnp.where(lane == h, l_tok, 0.0), axis=-1, keepdims=True)


def _fox_sample_body(q_ref, *refs, scale, hps, n_split, head_groups):
    ck_refs, cv_refs = refs[:n_split], refs[n_split:2 * n_split]
    kn_ref, vn_ref, lq_ref, lc_ref, ln_ref, o_ref = refs[2 * n_split:]
    hg = pl.program_id(0) % head_groups
    pn = ck_refs[0].shape[-2]
    D = q_ref.shape[-1]
    for hh in range(hps):
        q = q_ref[0, hh]
        lq = _head_column(lq_ref[...], hg * hps + hh)
        s_c = [_dot_nt(q, ck_refs[n][0, hh].astype(BF16)) * scale + (lq - _head_row(lc_ref[0, :, n * pn:(n + 1) * pn], hg * hps + hh))
               for n in range(n_split)]
        s_n = _dot_nt(q, kn_ref[0, hh]) * scale + (lq - _head_row(ln_ref[0], hg * hps + hh))
        r = lax.broadcasted_iota(jnp.int32, s_n.shape, 0)
        c = lax.broadcasted_iota(jnp.int32, s_n.shape, 1)
        s_n = jnp.where(c <= r, s_n, MASK_VALUE)
        m = jnp.max(s_n, axis=-1, keepdims=True)
        for s in s_c:
            m = jnp.maximum(m, jnp.max(s, axis=-1, keepdims=True))
        p_n = jnp.exp(s_n - m)
        l = jnp.sum(p_n, axis=-1, keepdims=True)
        o = _dot(p_n.astype(BF16), vn_ref[0, hh])
        for n in range(n_split):
            p = jnp.exp(s_c[n] - m)
            l = l + jnp.sum(p, axis=-1, keepdims=True)
            o = o + _dot(p.astype(BF16), cv_refs[n][0, hh].astype(BF16))
        o_ref[:, hh * D:(hh + 1) * D] = (o / l).astype(o_ref.dtype)


def _fox_sample_stage(q, cache_k, cache_v, layer, kn, vn, ln_tok, lc_row, ln_row, out_row0, out_rows, dst, hps):
    B, H, T, D = q.shape
    P = cache_k.shape[3]
    assert out_row0 % T == 0 and H % hps == 0
    r0 = out_row0 // T
    ng = H // hps
    n_split = FOX_SAMPLE_KEY_SPLITS if P % (FOX_SAMPLE_KEY_SPLITS * LANES) == 0 else 1
    pn = P // n_split
    new = pl.BlockSpec((1, hps, T, D), lambda s: (s // ng, s % ng, 0, 0))
    old = [pl.BlockSpec((None, 1, hps, pn, D), lambda s, n=n: (layer, s // ng, s % ng, n, 0)) for n in range(n_split)]
    return dict(
        steps=B * ng,
        body=functools.partial(_fox_sample_body, scale=D ** -0.5, hps=hps, n_split=n_split, head_groups=ng),
        in_specs=[new, *old, *old, new, new,
                  pl.BlockSpec((T, LANES), lambda s: (s // ng, 0)),
                  pl.BlockSpec((1, H, P), lambda s: (s // ng, 0, 0)),
                  pl.BlockSpec((1, H, T), lambda s: (s // ng, 0, 0))],
        args=[q, *([cache_k] * n_split), *([cache_v] * n_split), kn, vn, ln_tok, lc_row, ln_row],
        out_specs=[pl.BlockSpec((T, hps * D), lambda s: (s // ng + r0, s % ng))],
        out_shapes=[jax.ShapeDtypeStruct((out_rows, H * D), BF16)],
        dsts=[dst], scratch=[])


def _hgrn_body(*refs, C, hpb, has_init, n_chunks):
    if has_init:
        q_ref, k_ref, g_ref, i_ref, gate_ref, gn_ref, s0_ref, o_ref, sf_ref, st_ref = refs
    else:
        q_ref, k_ref, g_ref, i_ref, gate_ref, gn_ref, o_ref, sf_ref, st_ref = refs
        s0_ref = None
    c = pl.program_id(0) % n_chunks

    @pl.when(c == 0)
    def _():
        for hh in range(hpb):
            st_ref[hh] = s0_ref[0, hh].T if has_init else jnp.zeros((LANES, LANES), F32)

    row = lax.broadcasted_iota(jnp.int32, (C, C), 0)
    col = lax.broadcasted_iota(jnp.int32, (C, C), 1)
    tri = jnp.where(row >= col, 1.0, 0.0).astype(F32)
    trow = lax.broadcasted_iota(jnp.int32, (C, LANES), 0)
    level = 31 - lax.clz(jnp.where(row > col, row ^ col, 0))
    level = jnp.where(row == col, C.bit_length() - 1, level)

    def recur(heads):
        sls = {hh: slice(hh * LANES, (hh + 1) * LANES) for hh in heads}
        q = {hh: q_ref[:, sls[hh]] for hh in heads}
        k = {hh: k_ref[:, sls[hh]] for hh in heads}
        v = {hh: i_ref[:, sls[hh]].astype(BF16) for hh in heads}
        g = {hh: g_ref[:, sls[hh]] * LOG2E for hh in heads}
        G = {hh: jnp.dot(tri, g[hh], precision=HIGHEST, preferred_element_type=F32) for hh in heads}
        st = {hh: st_ref[hh] for hh in heads}
        o = {hh: _dot_nt((q[hh] * jnp.exp2(G[hh])).astype(BF16), st[hh].astype(BF16)) for hh in heads}
        A = {hh: jnp.where(level == C.bit_length() - 1, _dot_nt(q[hh].astype(BF16), k[hh].astype(BF16)), 0.0)
             for hh in heads}
        yq = {hh: G[hh] - g[hh] for hh in heads}
        yk = dict(G)
        w, log_w = 1, 0
        while w < C:
            upper = (trow & w) != 0
            for hh in heads:
                qt = (q[hh] * jnp.exp2(G[hh] - yq[hh])).astype(BF16)
                kt = (k[hh] * jnp.exp2(yk[hh] - G[hh])).astype(BF16)
                A[hh] = jnp.where(level == log_w, _dot_nt(qt, kt), A[hh])
                yq[hh] = jnp.where(upper, pltpu.roll(yq[hh], w, 0), yq[hh])
                yk[hh] = jnp.where(upper, yk[hh], pltpu.roll(yk[hh], C - w, 0))
            w, log_w = 2 * w, log_w + 1
        for hh in heads:
            out = o[hh] + _dot(A[hh].astype(BF16), v[hh])
            g_last = G[hh][C - 1:C, :]
            ks = (k[hh] * jnp.exp2(g_last - G[hh])).astype(BF16)
            st_ref[hh] = st[hh] * jnp.exp2(g_last) + _dot_tn(v[hh], ks)
            o_ref[:, sls[hh]] = (_rms(out, gn_ref[...]) * gate_ref[:, sls[hh]]).astype(o_ref.dtype)

    group = math.gcd(hpb, HGRN_HEADS_PER_GROUP)
    for h0 in range(0, hpb, group):
        recur(range(h0, h0 + group))

    @pl.when(c == n_chunks - 1)
    def _():
        for hh in range(hpb):
            sf_ref[0, hh] = st_ref[hh].T


def _hgrn_stage(main, k, gnorm, s0, row0, B, S, H, layer, n_layers, o_dst, s_dst):
    T = k.shape[0]
    C = min(S, LANES)
    nc = S // C
    assert row0 % C == 0
    r0 = row0 // C
    hpb = math.gcd(H, HGRN_HEADS_PER_STEP)
    nhb = H // hpb
    seq = lambda s: s // (nhb * nc)
    hblk = lambda s: (s // nc) % nhb
    row = lambda s: r0 + seq(s) * nc + s % nc
    tok = pl.BlockSpec((C, hpb * LANES), lambda s: (row(s), hblk(s)))
    seg = lambda n: pl.BlockSpec((C, hpb * LANES), lambda s: (row(s), n * nhb + hblk(s)))
    state = pl.BlockSpec((None, 1, hpb, LANES, LANES), lambda s: (layer, seq(s), hblk(s), 0, 0))
    in_specs = [seg(0), tok, seg(1), seg(2), seg(3), pl.BlockSpec((1, LANES), lambda s: (0, 0))]
    args = [main, k, main, main, main, gnorm.reshape(1, LANES)]
    if s0 is not None:
        in_specs.append(state)
        args.append(s0)
    return dict(
        steps=B * nhb * nc,
        body=functools.partial(_hgrn_body, C=C, hpb=hpb, has_init=s0 is not None, n_chunks=nc),
        in_specs=in_specs, args=args, out_specs=[tok, state],
        out_shapes=[jax.ShapeDtypeStruct((T, H * LANES), BF16),
                    jax.ShapeDtypeStruct((n_layers, B, H, LANES, LANES), F32)],
        dsts=[o_dst, s_dst], scratch=[pltpu.VMEM((hpb, LANES, LANES), F32)])


def _cmix_body(u_ref, v_ref, lng_ref, lnb_ref, ws_ref, bs_ref, y_ref, *vout, groups):
    Lc, Cd = bs_ref.shape
    cg = Cd // groups
    r = lax.broadcasted_iota(jnp.int32, (Lc, Lc), 0)
    c = lax.broadcasted_iota(jnp.int32, (Lc, Lc), 1)
    w = [jnp.where(c <= r, ws_ref[gi], 0.0).astype(BF16) for gi in range(groups)]
    for n in range(v_ref.shape[0] // Lc):
        rows = slice(n * Lc, (n + 1) * Lc)
        v = v_ref[rows]
        mu = jnp.mean(v, axis=-1, keepdims=True)
        xc = v - mu
        var = jnp.mean(xc * xc, axis=-1, keepdims=True)
        vn = xc * lax.rsqrt(var + EPS) * lng_ref[...] + lnb_ref[...]
        if vout:
            vout[0][rows] = vn
        vb = vn.astype(BF16)
        for gi in range(groups):
            sl = slice(gi * cg, (gi + 1) * cg)
            mixed = _dot(w[gi], vb[:, sl]) + bs_ref[:, sl]
            y_ref[rows, sl] = (u_ref[rows, sl] * mixed).astype(y_ref.dtype)


def _cmix(uv, ln_g, ln_b, ws, bs, row0, n_rows, Lc, y_dst, v_out):
    T, Cd = uv.shape[0], uv.shape[1] // 2
    G = ws.shape[0]
    tm = Lc * math.gcd(n_rows // Lc, CMIX_CHUNKS_PER_STEP)
    assert row0 % tm == 0
    r0 = row0 // tm
    tok = pl.BlockSpec((tm, Cd), lambda i: (i + r0, 0))
    tok_v = pl.BlockSpec((tm, Cd), lambda i: (i + r0, 1))
    vec = pl.BlockSpec((1, Cd), lambda i: (0, 0))
    out_specs, out_shapes, dsts = [tok], [jax.ShapeDtypeStruct((T, Cd), BF16)], [y_dst]
    if v_out:
        layer, n_layers, dst = v_out
        out_specs.append(pl.BlockSpec((None, tm, Cd), lambda i: (layer, i, 0)))
        out_shapes.append(jax.ShapeDtypeStruct((n_layers, n_rows, Cd), F32))
        dsts.append(dst)
    return _call(
        functools.partial(_cmix_body, groups=G), (n_rows // tm,),
        [tok, tok_v, vec, vec, pl.BlockSpec((G, Lc, Lc), lambda i: (0, 0, 0)),
         pl.BlockSpec((Lc, Cd), lambda i: (0, 0))],
        [uv, uv, ln_g.reshape(1, Cd), ln_b.reshape(1, Cd), ws, bs],
        out_specs, out_shapes, ("parallel",), dsts=dsts)


def _epi_gelu(z):
    return (jax.nn.gelu(z),)


def _epi_hgrn_forget(z, lb_all, *, layer):
    rows = [lb_all[n:n + 1] for n in range(lb_all.shape[0])]
    top = functools.reduce(jnp.maximum, rows)
    e = [jnp.exp(r - top) for r in rows]
    total = functools.reduce(jnp.add, e)
    sm = [a / total for a in e]
    lb = functools.reduce(jnp.add, sm[:layer + 1]) - sm[0]
    f = lb + (1.0 - lb) * jax.nn.sigmoid(z)
    g = jnp.log(jnp.maximum(f, TINY))
    k = (1.0 - lb) * jax.nn.sigmoid(-z)
    return g, k


def kernel(x_prompt, x_sample, cache_k, cache_v, cache_logf, state_hgrn, norm_ffn1, ffn1_gate, ffn1_up, ffn1_down, norm_mix, ab_w_in, ab_b_f, hgrn_lb, hgrn_gnorm, ab_w_out, c_w_in, c_ln_g, c_ln_b, c_w_s, c_b_s, c_w_out, norm_ffn2, ffn2_gate, ffn2_up, ffn2_down, norm_final):
    B, S, D = x_prompt.shape
    DB, DT, _ = x_sample.shape
    depth = norm_ffn1.shape[0]
    NAB, NC = ab_w_in.shape[0], c_w_in.shape[0]
    HA, P, HD = cache_k.shape[2], cache_k.shape[3], cache_k.shape[4]
    HB, DK, DV = state_hgrn.shape[2], state_hgrn.shape[3], state_hgrn.shape[4]
    assert HD == LANES and DK == LANES and DV == LANES and HA <= 8
    WA, WBK, WBV = HA * HD, HB * DK, HB * DV
    G, CL = c_w_s.shape[1], c_w_s.shape[2]
    CD = c_w_out.shape[1]
    Tp, Ts = B * S, DB * DT
    T = Tp + Ts
    bf = lambda a: a.astype(BF16)

    pk = pv = plf = ps = sk = sv = slf = ss = scv = None
    x = None
    for l in range(depth):
        j = l // 2
        w1 = (norm_ffn1[l], ffn1_gate, ffn1_up, ffn1_down, l, norm_mix[l])
        if l == 0:
            x, h = _ffn(x_prompt.reshape(Tp, D), *w1, out_rows=T)
            x, h = _ffn(x_sample.reshape(Ts, D), *w1, out_rows=T, out_row0=Tp, dsts=(x, h))
        else:
            x, h = _ffn(x, *w1)
        if l % 2 == 0:
            w_in = ab_w_in[j]
            o1 = 3 * WA + HA
            w_qkvf = bf(jnp.pad(w_in[:, :o1], ((0, 0), (0, LANES - HA))))
            w_b = bf(w_in[:, o1:])
            b_f = jnp.pad(ab_b_f[j], (0, LANES - HA)).reshape(1, LANES)
            qs, sk, ksb, sv, vsb, lf_tok = _run_stages(
                [_proj_qkv_stage(h, w_qkvf, b_f, Tp, DB, DT, j, NAB, sk, sv, None)])
            qp, pk, kpb, pv, vpb, lf_tok = _run_stages(
                [_proj_qkv_stage(h, w_qkvf, b_f, 0, B, S, j, NAB, pk, pv, lf_tok)])
            assert WBK == WBV
            zb, kb = _proj_hgrn(h, w_b, hgrn_lb, j)

            _, lp_row, plf = _cumsum(lf_tok, jnp.zeros((Tp, LANES), F32), 0, B, S, HA, j, NAB, plf)
            lc_row = _row_cumsum(cache_logf.reshape(NAB * DB * HA, P), j * DB * HA, DB * HA).reshape(DB, HA, P)
            base = jnp.pad(lc_row[:, :, P - 1], ((0, 0), (0, LANES - HA)))
            base = jnp.broadcast_to(base[:, None, :], (DB, DT, LANES)).reshape(Ts, LANES)
            ls_tok, ls_row, slf = _cumsum(lf_tok, base, Tp, DB, DT, HA, j, NAB, slf)

            rec_p = _hgrn_stage(zb, kb, hgrn_gnorm[j], None, 0, B, S, HB, j, NAB, None, ps)
            fox_s = functools.partial(_fox_sample_stage, qs, cache_k, cache_v, j, ksb, vsb, ls_tok,
                                      lc_row, ls_row, Tp, T, None)
            hps = DB * HA // rec_p["steps"]
            if 1 <= hps <= FOX_SAMPLE_MAX_HEADS_PER_STEP and HA % hps == 0 and DB * (HA // hps) == rec_p["steps"]:
                ob, ps, oa = _run_stages([rec_p, fox_s(hps)])
            else:
                ob, ps = _run_stages([rec_p])
                (oa,) = _run_stages([fox_s(math.gcd(HA, FOX_SAMPLE_HEADS_PER_STEP))])
            oa = _fox_prompt(qp, kpb, vpb, lp_row, T, oa)
            ob, ss = _run_stages([_hgrn_stage(zb, kb, hgrn_gnorm[j], state_hgrn, Tp, DB, DT, HB, j, NAB, ob, ss)])

            w_out = bf(ab_w_out[j])
            x = _mm_res([oa, ob], [w_out[:WA], w_out[WA:]], x)
        else:
            (uv,) = _proj(h, bf(c_w_in[j]), [], _epi_gelu, [F32], 512)
            y = None
            for row0, n_rows, seq in ((0, Tp, S), (Tp, Ts, DT)):
                Lc = min(seq, CL)
                ws = c_w_s[j][:, :Lc, :Lc]
                bs = jnp.repeat(c_b_s[j][:, :Lc].T, CD // G, axis=1)
                if row0 == 0:
                    (y,) = _cmix(uv, c_ln_g[j], c_ln_b[j], ws, bs, row0, n_rows, Lc, y, None)
                else:
                    y, scv = _cmix(uv, c_ln_g[j], c_ln_b[j], ws, bs, row0, n_rows, Lc, y, (j, NC, scv))
            x = _mm_res([y], [bf(c_w_out[j])], x)
        x = _ffn(x, norm_ffn2[l], ffn2_gate, ffn2_up, ffn2_down, l)

    y_p = _norm(x, norm_final, F32, 0, Tp)
    y_s = _norm(x, norm_final, F32, Tp, Ts)
    return (y_p.reshape(B, S, D), y_s.reshape(DB, DT, D), pk, pv, plf, ps,
            sk, sv, slf, ss, scv.reshape(NC, DB, DT, CD))
```

```python
import functools
import math

import jax
import jax.numpy as jnp
from jax import lax
from jax.experimental import pallas as pl
from jax.experimental.pallas import tpu as pltpu

F32 = jnp.float32
BF16 = jnp.bfloat16
EPS = 1e-6
TINY = 1e-30
MASK_VALUE = -1e30
LOG2E = math.log2(math.e)
HIGHEST = lax.Precision.HIGHEST

LANES = 128
MIB = 1 << 20
VMEM_LIMIT = 60 * MIB
FFN_TOKEN_TILE = 1024
FFN_FF_TILE = 256
PROJ_COL_TILE = 512
HGRN_HEADS_PER_STEP = 8
HGRN_HEADS_PER_GROUP = 8
CMIX_CHUNKS_PER_STEP = 4
FOX_SAMPLE_HEADS_PER_STEP = 2
FOX_SAMPLE_MAX_HEADS_PER_STEP = 4
FOX_SAMPLE_KEY_SPLITS = 2


def _tile(n, target):
    if n <= target:
        return n
    for t in range(target, 7, -1):
        if n % t == 0 and t % 8 == 0:
            return t
    return n


def _call(body, grid, in_specs, args, out_specs, out_shapes, sem, dsts=None, scratch=()):
    in_specs, args = list(in_specs), list(args)
    n_in = len(args)
    aliases = {}
    for k, d in enumerate(dsts or ()):
        if d is not None:
            aliases[len(args)] = k
            in_specs.append(pl.BlockSpec(memory_space=pl.ANY))
            args.append(d)
    n_all = len(args)

    def wrapped(*refs):
        body(*refs[:n_in], *refs[n_all:])

    return pl.pallas_call(
        wrapped, grid=grid, in_specs=in_specs, out_specs=out_specs, out_shape=out_shapes,
        input_output_aliases=aliases, scratch_shapes=list(scratch),
        compiler_params=pltpu.CompilerParams(dimension_semantics=sem, vmem_limit_bytes=VMEM_LIMIT),
    )(*args)


def _run_stages(stages):
    steps = stages[0]["steps"]
    assert all(st["steps"] == steps for st in stages)
    n_in = [len(st["args"]) for st in stages]
    n_out = [len(st["out_specs"]) for st in stages]
    n_scr = [len(st["scratch"]) for st in stages]

    def body(*refs):
        ins, outs, scr = refs[:sum(n_in)], refs[sum(n_in):sum(n_in) + sum(n_out)], refs[sum(n_in) + sum(n_out):]
        a = b = c = 0
        for st, ni, no, ns in zip(stages, n_in, n_out, n_scr):
            st["body"](*ins[a:a + ni], *outs[b:b + no], *scr[c:c + ns])
            a, b, c = a + ni, b + no, c + ns

    cat = lambda key: [x for st in stages for x in st[key]]
    return _call(body, (steps,), cat("in_specs"), cat("args"), cat("out_specs"), cat("out_shapes"),
                 ("arbitrary",), dsts=cat("dsts"), scratch=cat("scratch"))


def _dot(a, b):
    return jnp.dot(a, b, preferred_element_type=F32)


def _dot_nt(a, b):
    return lax.dot_general(a, b, (((1,), (1,)), ((), ())), preferred_element_type=F32)


def _dot_tn(a, b):
    return lax.dot_general(a, b, (((0,), (0,)), ((), ())), preferred_element_type=F32)


def _rms(x, g):
    return x * lax.rsqrt(jnp.mean(x * x, axis=-1, keepdims=True) + EPS) * g


def _ffn_body(x_ref, g_ref, wg_ref, wu_ref, wd_ref, *refs, next_norm):
    if next_norm:
        g2_ref, o_ref, h2_ref, h_ref = refs
    else:
        o_ref, h_ref = refs

    @pl.when(pl.program_id(1) == 0)
    def _():
        x = x_ref[...]
        h_ref[...] = _rms(x, g_ref[...]).astype(BF16)
        o_ref[...] = x

    h = h_ref[...]
    a = _dot(h, wg_ref[...].astype(BF16))
    b = _dot(h, wu_ref[...].astype(BF16))
    act = (0.5 * a * jax.nn.sigmoid(a) * b).astype(BF16)
    o_ref[...] += _dot(act, wd_ref[...].astype(BF16))

    if next_norm:
        @pl.when(pl.program_id(1) == pl.num_programs(1) - 1)
        def _():
            h2_ref[...] = _rms(o_ref[...], g2_ref[...]).astype(h2_ref.dtype)


def _ffn(x, g, wg, wu, wd, layer, g_next=None, out_rows=None, out_row0=0, dsts=(None, None)):
    T, D = x.shape
    F = wg.shape[2]
    tm = _tile(T, FFN_TOKEN_TILE)
    tf = _tile(F, FFN_FF_TILE)
    out_rows = out_rows or T
    assert out_row0 % tm == 0
    i0 = out_row0 // tm
    vec = pl.BlockSpec((1, D), lambda i, j: (0, 0))
    tok = pl.BlockSpec((tm, D), lambda i, j: (i + i0, 0))
    in_specs = [pl.BlockSpec((tm, D), lambda i, j: (i, 0)), vec,
                pl.BlockSpec((None, D, tf), lambda i, j: (layer, 0, j)),
                pl.BlockSpec((None, D, tf), lambda i, j: (layer, 0, j)),
                pl.BlockSpec((None, tf, D), lambda i, j: (layer, j, 0))]
    args = [x, g.reshape(1, D), wg, wu, wd]
    out_specs, out_shapes = [tok], [jax.ShapeDtypeStruct((out_rows, D), F32)]
    if g_next is not None:
        in_specs.append(vec)
        args.append(g_next.reshape(1, D))
        out_specs.append(tok)
        out_shapes.append(jax.ShapeDtypeStruct((out_rows, D), BF16))
    outs = _call(
        functools.partial(_ffn_body, next_norm=g_next is not None), (T // tm, F // tf),
        in_specs, args, out_specs, out_shapes,
        ("parallel", "arbitrary"), dsts=list(dsts[:len(out_specs)]),
        scratch=[pltpu.VMEM((tm, D), BF16)])
    return outs if g_next is not None else outs[0]


def _norm_body(x_ref, g_ref, o_ref):
    o_ref[...] = _rms(x_ref[...], g_ref[...]).astype(o_ref.dtype)


def _norm(x, g, dtype, row0=0, n_rows=None):
    D = x.shape[1]
    n_rows = n_rows or x.shape[0]
    tm = _tile(n_rows, 512)
    assert row0 % tm == 0
    i0 = row0 // tm
    return _call(
        _norm_body, (n_rows // tm,),
        [pl.BlockSpec((tm, D), lambda i: (i + i0, 0)), pl.BlockSpec((1, D), lambda i: (0, 0))],
        [x, g.reshape(1, D)],
        pl.BlockSpec((tm, D), lambda i: (i, 0)),
        jax.ShapeDtypeStruct((n_rows, D), dtype), ("parallel",))


def _resident(shape):
    return pl.BlockSpec(shape, lambda i: (0,) * len(shape), pipeline_mode=pl.Buffered(1))


def _col_tiles(n):
    tn = _tile(n, PROJ_COL_TILE)
    return [slice(c * tn, (c + 1) * tn) for c in range(n // tn)]


def _proj_body(h_ref, w_ref, *refs, epilogue, n_aux):
    aux, outs = refs[:n_aux], refs[n_aux:]
    h = h_ref[...]
    for cols in _col_tiles(w_ref.shape[1]):
        vals = epilogue(_dot(h, w_ref[:, cols]), *[a[:, cols] for a in aux])
        for o_ref, v in zip(outs, vals):
            o_ref[:, cols] = v.astype(o_ref.dtype)


def _proj(h, w, aux, epilogue, out_dtypes, tm_target):
    T, K = h.shape
    N = w.shape[1]
    tm = _tile(T, tm_target)
    return _call(
        functools.partial(_proj_body, epilogue=epilogue, n_aux=len(aux)),
        (T // tm,),
        [pl.BlockSpec((tm, K), lambda i: (i, 0)), _resident((K, N))] + [_resident(a.shape) for a in aux],
        [h, w, *aux],
        [pl.BlockSpec((tm, N), lambda i: (i, 0)) for _ in out_dtypes],
        [jax.ShapeDtypeStruct((T, N), d) for d in out_dtypes],
        ("parallel",))


def _proj_hgrn_body(h_ref, w_ref, lb_ref, main_ref, k_ref, *, layer):
    W = k_ref.shape[1]
    h = h_ref[...]
    for seg in range(4):
        for kc in _col_tiles(W):
            cols = slice(seg * W + kc.start, seg * W + kc.stop)
            z = _dot(h, w_ref[:, cols])
            if seg == 1:
                g, k = _epi_hgrn_forget(z, lb_ref[:, kc], layer=layer)
                main_ref[:, cols] = g
                k_ref[:, kc] = k
            else:
                main_ref[:, cols] = z if seg == 2 else z * jax.nn.sigmoid(z)


def _proj_hgrn(h, w, lb_all, layer):
    T, K = h.shape
    W = w.shape[1] // 4
    tm = _tile(T, 512)
    return _call(
        functools.partial(_proj_hgrn_body, layer=layer),
        (T // tm,),
        [pl.BlockSpec((tm, K), lambda i: (i, 0)), _resident((K, 4 * W)), _resident(lb_all.shape)],
        [h, w, lb_all],
        [pl.BlockSpec((tm, 4 * W), lambda i: (i, 0)), pl.BlockSpec((tm, W), lambda i: (i, 0))],
        [jax.ShapeDtypeStruct((T, 4 * W), F32), jax.ShapeDtypeStruct((T, W), F32)],
        ("parallel",))


def _qkv_body(h_ref, w_ref, bf_ref, q_ref, k_ref, kb_ref, v_ref, vb_ref, lf_ref, *, nb, ts):
    H = q_ref.shape[1]
    h = h_ref[...]
    gate_cols = slice(3 * H * LANES, 3 * H * LANES + LANES)
    lf_ref[...] = jax.nn.log_sigmoid(_dot(h, w_ref[:, gate_cols]) + bf_ref[...])
    targets = ((q_ref,), (k_ref, kb_ref), (v_ref, vb_ref))
    for cols in _col_tiles(3 * H * LANES):
        z = _dot(h, w_ref[:, cols])
        for n in range((cols.stop - cols.start) // LANES):
            head = cols.start // LANES + n
            zz = z[:, n * LANES:(n + 1) * LANES].reshape(nb, ts, LANES)
            for o_ref in targets[head // H]:
                o_ref[:, head % H] = zz.astype(o_ref.dtype)


def _proj_qkv_stage(h, w, b_f, row0, B, S, layer, n_layers, k_dst, v_dst, lf_dst, rows_per_step=512):
    T, K = h.shape
    H = (w.shape[1] - LANES) // (3 * LANES)
    if S >= rows_per_step:
        ts, nb = _tile(S, rows_per_step), 1
    else:
        nb = _tile(B, max(1, rows_per_step // S))
        ts = S
    tm = nb * ts
    spt = S // ts
    assert row0 % tm == 0
    i0 = row0 // tm
    if nb == 1:
        omap = lambda i: (i // spt, 0, i % spt, 0)
    else:
        omap = lambda i: (i, 0, 0, 0)
    plain = pl.BlockSpec((nb, H, ts, LANES), omap)
    slab = pl.BlockSpec((None, nb, H, ts, LANES), lambda i: (layer, *omap(i)))
    bf_shape = jax.ShapeDtypeStruct((B, H, S, LANES), BF16)
    slab_shape = jax.ShapeDtypeStruct((n_layers, B, H, S, LANES), F32)
    return dict(
        steps=B * S // tm, body=functools.partial(_qkv_body, nb=nb, ts=ts),
        in_specs=[pl.BlockSpec((tm, K), lambda i: (i + i0, 0)), _resident(w.shape), _resident(b_f.shape)],
        args=[h, w, b_f],
        out_specs=[plain, slab, plain, slab, plain, pl.BlockSpec((tm, LANES), lambda i: (i + i0, 0))],
        out_shapes=[bf_shape, slab_shape, bf_shape, slab_shape, bf_shape, jax.ShapeDtypeStruct((T, LANES), F32)],
        dsts=[None, k_dst, None, v_dst, None, lf_dst], scratch=[])


def _mm_res_body(*refs, n):
    lhs = [r[...] for r in refs[:n]]
    w_ref, x_ref, o_ref, wb_ref = refs[n:]

    @pl.when(pl.program_id(0) == 0)
    def _():
        wb_ref[...] = w_ref[...].astype(BF16)

    for cols in _col_tiles(x_ref.shape[1]):
        z, k0 = None, 0
        for a in lhs:
            part = _dot(a, wb_ref[k0:k0 + a.shape[1], cols])
            z = part if z is None else z + part
            k0 += a.shape[1]
        o_ref[:, cols] = x_ref[:, cols] + z


def _mm_res(lhs, w, layer, x):
    T, N = x.shape
    K = w.shape[1]
    assert sum(a.shape[1] for a in lhs) == K
    tm = _tile(T, 512)
    tok = pl.BlockSpec((tm, N), lambda i: (i, 0))
    return _call(
        functools.partial(_mm_res_body, n=len(lhs)), (T // tm,),
        ([pl.BlockSpec((tm, a.shape[1]), lambda i: (i, 0)) for a in lhs]
         + [pl.BlockSpec((None, K, N), lambda i: (layer, 0, 0), pipeline_mode=pl.Buffered(1)), tok]),
        [*lhs, w, x],
        tok, jax.ShapeDtypeStruct((T, N), F32), ("arbitrary",),
        scratch=[pltpu.VMEM((K, N), BF16)])


def _cumsum_body(lf_ref, base_ref, ltok_ref, lrow_ref, lfrow_ref, carry_ref, *, ts, seg, tiles_per_seq, n_heads):
    lf = lf_ref[...]
    r = lax.broadcasted_iota(jnp.int32, (ts, ts), 0)
    c = lax.broadcasted_iota(jnp.int32, (ts, ts), 1)
    tri = jnp.where(r >= c, 1.0, 0.0).astype(F32)
    if seg < ts:
        tri = jnp.where(r // seg == c // seg, tri, 0.0)
    L = jnp.dot(tri, lf, precision=HIGHEST, preferred_element_type=F32)
    if tiles_per_seq:
        @pl.when(pl.program_id(0) % tiles_per_seq == 0)
        def _():
            carry_ref[...] = jnp.zeros_like(carry_ref)
        L = L + carry_ref[...]
        carry_ref[...] = L[ts - 1:ts, :]
    else:
        L = L + base_ref[...]
    ltok_ref[...] = L
    Lt = L.T[:n_heads]
    lft = lf.T[:n_heads]
    for s in range(ts // seg):
        lrow_ref[s] = Lt[:, s * seg:(s + 1) * seg]
        lfrow_ref[s] = lft[:, s * seg:(s + 1) * seg]


def _cumsum(lf_tok, base, row0, B, S, n_heads, layer, n_layers, dst):
    if S >= LANES:
        ts = _tile(S, 512)
        seg, tps, nseg = ts, S // ts, 1
    else:
        ts, seg, tps = LANES, S, 0
        nseg = ts // seg
    assert row0 % ts == 0 and (B * S) % ts == 0
    i0 = row0 // ts
    if tps:
        omap = lambda i: (i // tps, 0, i % tps)
    else:
        omap = lambda i: (i, 0, 0)
    return _call(
        functools.partial(_cumsum_body, ts=ts, seg=seg, tiles_per_seq=tps, n_heads=n_heads),
        (B * S // ts,),
        [pl.BlockSpec((ts, LANES), lambda i: (i + i0, 0)), pl.BlockSpec((ts, LANES), lambda i: (i, 0))],
        [lf_tok, base],
        [pl.BlockSpec((ts, LANES), lambda i: (i, 0)),
         pl.BlockSpec((nseg, n_heads, seg), omap),
         pl.BlockSpec((None, nseg, n_heads, seg), lambda i: (layer, *omap(i)))],
        [jax.ShapeDtypeStruct((B * S, LANES), F32), jax.ShapeDtypeStruct((B, n_heads, S), F32),
         jax.ShapeDtypeStruct((n_layers, B, n_heads, S), F32)],
        ("arbitrary",), dsts=[None, None, dst], scratch=[pltpu.VMEM((1, LANES), F32)])


def _row_cumsum_body(x_ref, o_ref):
    R, P = x_ref.shape
    r = lax.broadcasted_iota(jnp.int32, (LANES, LANES), 0)
    c = lax.broadcasted_iota(jnp.int32, (LANES, LANES), 1)
    tri = jnp.where(r <= c, 1.0, 0.0).astype(F32)
    carry = jnp.zeros((R, 1), F32)
    for s in range(P // LANES):
        y = jnp.dot(x_ref[:, s * LANES:(s + 1) * LANES], tri, precision=HIGHEST, preferred_element_type=F32) + carry
        o_ref[:, s * LANES:(s + 1) * LANES] = y
        carry = y[:, LANES - 1:LANES]


def _row_cumsum(x, row0, n_rows):
    P = x.shape[1]
    tr = _tile(n_rows, 256)
    assert row0 % tr == 0
    i0 = row0 // tr
    return _call(
        _row_cumsum_body, (n_rows // tr,),
        [pl.BlockSpec((tr, P), lambda i: (i + i0, 0))], [x],
        pl.BlockSpec((tr, P), lambda i: (i, 0)),
        jax.ShapeDtypeStruct((n_rows, P), F32), ("parallel",))


def _head_row(rows, h):
    sublane = lax.broadcasted_iota(jnp.int32, rows.shape, 0)
    return jnp.sum(jnp.where(sublane == h, rows, 0.0), axis=0, keepdims=True)


def _fox_prompt_body(q_ref, k_ref, v_ref, l_ref, o_ref, m_ref, d_ref, acc_ref, s0_ref, s1_ref, *, scale, tq, sub):
    qi = pl.program_id(2)
    c1 = scale * LOG2E
    tk = tq // 2
    n_sub = tq // sub
    every = tuple(range(n_sub))
    m_ref[...] = jnp.full_like(m_ref, MASK_VALUE)
    d_ref[...] = jnp.zeros_like(d_ref)
    acc_ref[...] = jnp.zeros_like(acc_ref)
    head = pl.program_id(1)
    l_here = _head_row(l_ref[0, :, pl.ds(pl.multiple_of(qi * tq, tq), LANES)], head)[:, 0:1]

    def scores(ki, s_ref, subs):
        k0 = pl.multiple_of(ki * tk, tk)
        k = k_ref[0, 0, pl.ds(k0, tk), :]
        bias = (l_here - _head_row(l_ref[0, :, pl.ds(k0, tk)], head)) * LOG2E
        for u in subs:
            rows = slice(u * sub, (u + 1) * sub)
            s_ref[rows] = _dot_nt(q_ref[0, 0, rows, :], k) * c1 + bias

    def update(ki, s_ref, subs, diagonal=None):
        k0 = pl.multiple_of(ki * tk, tk)
        v = v_ref[0, 0, pl.ds(k0, tk), :]
        for n in range(0, len(subs), 2):
            update_pair(s_ref, v, subs[n:n + 2], diagonal)

    def update_pair(s_ref, v, subs, diagonal):
        rows = [slice(u * sub, (u + 1) * sub) for u in subs]
        s = []
        for u, rw in zip(subs, rows):
            su = s_ref[rw]
            if diagonal is not None and u * sub < (diagonal + 1) * tk - 1:
                r = lax.broadcasted_iota(jnp.int32, su.shape, 0) + u * sub
                c = lax.broadcasted_iota(jnp.int32, su.shape, 1) + diagonal * tk
                su = jnp.where(c <= r, su, MASK_VALUE)
            s.append(su)
        m_prev = [m_ref[rw] for rw in rows]
        m_new = [jnp.maximum(mp, jnp.max(su, axis=-1, keepdims=True)) for mp, su in zip(m_prev, s)]
        alpha = [jnp.exp2(mp - mn) for mp, mn in zip(m_prev, m_new)]
        p = [jnp.exp2(su - mn) for su, mn in zip(s, m_new)]
        for n, rw in enumerate(rows):
            d_ref[rw] = alpha[n] * d_ref[rw] + jnp.sum(p[n], axis=-1, keepdims=True)
            m_ref[rw] = m_new[n]
        for n, rw in enumerate(rows):
            acc_ref[rw] = alpha[n] * acc_ref[rw] + _dot(p[n].astype(BF16), v)

    def two_blocks(t, carry):
        scores(2 * t + 1, s1_ref, every)
        update(2 * t, s0_ref, every)
        scores(2 * t + 2, s0_ref, every)
        update(2 * t + 1, s1_ref, every)
        return carry

    scores(0, s0_ref, every)
    lax.fori_loop(0, qi, two_blocks, 0)
    late = tuple(u for u in every if (u + 1) * sub > tk)
    scores(2 * qi + 1, s1_ref, late)
    update(2 * qi, s0_ref, every, diagonal=0)
    update(2 * qi + 1, s1_ref, late, diagonal=1)
    o_ref[...] = (acc_ref[...] / d_ref[...]).astype(o_ref.dtype)


def _fox_prompt(q, k, v, l_row, out_rows, dst):
    B, H, S, D = q.shape
    tq = _tile(S, 1024)
    sub = min(256, tq // 2)
    assert tq % (2 * LANES) == 0 and (tq // 2) % sub == 0
    nq = S // tq
    full = pl.BlockSpec((1, 1, S, D), lambda b, h, qi: (b, h, 0, 0))
    return _call(
        functools.partial(_fox_prompt_body, scale=D ** -0.5, tq=tq, sub=sub),
        (B, H, nq),
        [pl.BlockSpec((1, 1, tq, D), lambda b, h, qi: (b, h, qi, 0)), full, full,
         pl.BlockSpec((1, H, S), lambda b, h, qi: (b, 0, 0))],
        [q, k, v, l_row],
        pl.BlockSpec((tq, D), lambda b, h, qi: (b * nq + qi, h)),
        jax.ShapeDtypeStruct((out_rows, H * D), BF16),
        ("parallel", "parallel", "arbitrary"), dsts=[dst],
        scratch=[pltpu.VMEM((tq, 1), F32), pltpu.VMEM((tq, 1), F32), pltpu.VMEM((tq, D), F32),
                 pltpu.VMEM((tq, tq // 2), F32), pltpu.VMEM((tq, tq // 2), F32)])


def _head_column(l_tok, h):
    lane = lax.broadcasted_iota(jnp.int32, l_tok.shape, 1)
    return jnp.sum(jnp.where(lane == h, l_tok, 0.0), axis=-1, keepdims=True)


def _fox_sample_body(q_ref, *refs, scale, hps, n_split, head_groups):
    ck_refs, cv_refs = refs[:n_split], refs[n_split:2 * n_split]
    kn_ref, vn_ref, lq_ref, lc_ref, ln_ref, o_ref = refs[2 * n_split:]
    hg = pl.program_id(0) % head_groups
    pn = ck_refs[0].shape[-2]
    D = q_ref.shape[-1]
    for hh in range(hps):
        q = q_ref[0, hh]
        lq = _head_column(lq_ref[...], hg * hps + hh)
        s_c = [_dot_nt(q, ck_refs[n][0, hh].astype(BF16)) * scale + (lq - _head_row(lc_ref[0, :, n * pn:(n + 1) * pn], hg * hps + hh))
               for n in range(n_split)]
        s_n = _dot_nt(q, kn_ref[0, hh]) * scale + (lq - _head_row(ln_ref[0], hg * hps + hh))
        r = lax.broadcasted_iota(jnp.int32, s_n.shape, 0)
        c = lax.broadcasted_iota(jnp.int32, s_n.shape, 1)
        s_n = jnp.where(c <= r, s_n, MASK_VALUE)
        m = jnp.max(s_n, axis=-1, keepdims=True)
        for s in s_c:
            m = jnp.maximum(m, jnp.max(s, axis=-1, keepdims=True))
        p_n = jnp.exp(s_n - m)
        l = jnp.sum(p_n, axis=-1, keepdims=True)
        o = _dot(p_n.astype(BF16), vn_ref[0, hh])
        for n in range(n_split):
            p = jnp.exp(s_c[n] - m)
            l = l + jnp.sum(p, axis=-1, keepdims=True)
            o = o + _dot(p.astype(BF16), cv_refs[n][0, hh].astype(BF16))
        o_ref[:, hh * D:(hh + 1) * D] = (o / l).astype(o_ref.dtype)


def _fox_sample_stage(q, cache_k, cache_v, layer, kn, vn, ln_tok, lc_row, ln_row, out_row0, out_rows, dst, hps):
    B, H, T, D = q.shape
    P = cache_k.shape[3]
    assert out_row0 % T == 0 and H % hps == 0
    r0 = out_row0 // T
    ng = H // hps
    n_split = FOX_SAMPLE_KEY_SPLITS if P % (FOX_SAMPLE_KEY_SPLITS * LANES) == 0 else 1
    pn = P // n_split
    new = pl.BlockSpec((1, hps, T, D), lambda s: (s // ng, s % ng, 0, 0))
    old = [pl.BlockSpec((None, 1, hps, pn, D), lambda s, n=n: (layer, s // ng, s % ng, n, 0)) for n in range(n_split)]
    return dict(
        steps=B * ng,
        body=functools.partial(_fox_sample_body, scale=D ** -0.5, hps=hps, n_split=n_split, head_groups=ng),
        in_specs=[new, *old, *old, new, new,
                  pl.BlockSpec((T, LANES), lambda s: (s // ng, 0)),
                  pl.BlockSpec((1, H, P), lambda s: (s // ng, 0, 0)),
                  pl.BlockSpec((1, H, T), lambda s: (s // ng, 0, 0))],
        args=[q, *([cache_k] * n_split), *([cache_v] * n_split), kn, vn, ln_tok, lc_row, ln_row],
        out_specs=[pl.BlockSpec((T, hps * D), lambda s: (s // ng + r0, s % ng))],
        out_shapes=[jax.ShapeDtypeStruct((out_rows, H * D), BF16)],
        dsts=[dst], scratch=[])


def _hgrn_body(*refs, C, hpb, has_init, n_chunks):
    if has_init:
        q_ref, k_ref, g_ref, i_ref, gate_ref, gn_ref, s0_ref, o_ref, sf_ref, st_ref = refs
    else:
        q_ref, k_ref, g_ref, i_ref, gate_ref, gn_ref, o_ref, sf_ref, st_ref = refs
        s0_ref = None
    c = pl.program_id(0) % n_chunks

    @pl.when(c == 0)
    def _():
        for hh in range(hpb):
            st_ref[hh] = s0_ref[0, hh].T if has_init else jnp.zeros((LANES, LANES), F32)

    row = lax.broadcasted_iota(jnp.int32, (C, C), 0)
    col = lax.broadcasted_iota(jnp.int32, (C, C), 1)
    tri = jnp.where(row >= col, 1.0, 0.0).astype(F32)
    trow = lax.broadcasted_iota(jnp.int32, (C, LANES), 0)
    level = 31 - lax.clz(jnp.where(row > col, row ^ col, 0))
    level = jnp.where(row == col, C.bit_length() - 1, level)

    def recur(heads):
        sls = {hh: slice(hh * LANES, (hh + 1) * LANES) for hh in heads}
        q = {hh: q_ref[:, sls[hh]] for hh in heads}
        k = {hh: k_ref[:, sls[hh]] for hh in heads}
        v = {hh: i_ref[:, sls[hh]].astype(BF16) for hh in heads}
        g = {hh: g_ref[:, sls[hh]] * LOG2E for hh in heads}
        G = {hh: jnp.dot(tri, g[hh], precision=HIGHEST, preferred_element_type=F32) for hh in heads}
        st = {hh: st_ref[hh] for hh in heads}
        o = {hh: _dot_nt((q[hh] * jnp.exp2(G[hh])).astype(BF16), st[hh].astype(BF16)) for hh in heads}
        A = {hh: jnp.where(level == C.bit_length() - 1, _dot_nt(q[hh].astype(BF16), k[hh].astype(BF16)), 0.0)
             for hh in heads}
        yq = {hh: G[hh] - g[hh] for hh in heads}
        yk = dict(G)
        w, log_w = 1, 0
        while w < C:
            upper = (trow & w) != 0
            for hh in heads:
                qt = (q[hh] * jnp.exp2(G[hh] - yq[hh])).astype(BF16)
                kt = (k[hh] * jnp.exp2(yk[hh] - G[hh])).astype(BF16)
                A[hh] = jnp.where(level == log_w, _dot_nt(qt, kt), A[hh])
                yq[hh] = jnp.where(upper, pltpu.roll(yq[hh], w, 0), yq[hh])
                yk[hh] = jnp.where(upper, yk[hh], pltpu.roll(yk[hh], C - w, 0))
            w, log_w = 2 * w, log_w + 1
        for hh in heads:
            out = o[hh] + _dot(A[hh].astype(BF16), v[hh])
            g_last = G[hh][C - 1:C, :]
            ks = (k[hh] * jnp.exp2(g_last - G[hh])).astype(BF16)
            st_ref[hh] = st[hh] * jnp.exp2(g_last) + _dot_tn(v[hh], ks)
            o_ref[:, sls[hh]] = (_rms(out, gn_ref[...]) * gate_ref[:, sls[hh]]).astype(o_ref.dtype)

    group = math.gcd(hpb, HGRN_HEADS_PER_GROUP)
    for h0 in range(0, hpb, group):
        recur(range(h0, h0 + group))

    @pl.when(c == n_chunks - 1)
    def _():
        for hh in range(hpb):
            sf_ref[0, hh] = st_ref[hh].T


def _hgrn_stage(main, k, gnorm, s0, row0, B, S, H, layer, n_layers, o_dst, s_dst):
    T = k.shape[0]
    C = min(S, LANES)
    nc = S // C
    assert row0 % C == 0
    r0 = row0 // C
    hpb = math.gcd(H, HGRN_HEADS_PER_STEP)
    nhb = H // hpb
    seq = lambda s: s // (nhb * nc)
    hblk = lambda s: (s // nc) % nhb
    row = lambda s: r0 + seq(s) * nc + s % nc
    tok = pl.BlockSpec((C, hpb * LANES), lambda s: (row(s), hblk(s)))
    seg = lambda n: pl.BlockSpec((C, hpb * LANES), lambda s: (row(s), n * nhb + hblk(s)))
    state = pl.BlockSpec((None, 1, hpb, LANES, LANES), lambda s: (layer, seq(s), hblk(s), 0, 0))
    in_specs = [seg(0), tok, seg(1), seg(2), seg(3), pl.BlockSpec((1, LANES), lambda s: (0, 0))]
    args = [main, k, main, main, main, gnorm.reshape(1, LANES)]
    if s0 is not None:
        in_specs.append(state)
        args.append(s0)
    return dict(
        steps=B * nhb * nc,
        body=functools.partial(_hgrn_body, C=C, hpb=hpb, has_init=s0 is not None, n_chunks=nc),
        in_specs=in_specs, args=args, out_specs=[tok, state],
        out_shapes=[jax.ShapeDtypeStruct((T, H * LANES), BF16),
                    jax.ShapeDtypeStruct((n_layers, B, H, LANES, LANES), F32)],
        dsts=[o_dst, s_dst], scratch=[pltpu.VMEM((hpb, LANES, LANES), F32)])


def _cmix_body(u_ref, v_ref, lng_ref, lnb_ref, ws_ref, bs_ref, y_ref, *vout, groups):
    Lc, Cd = bs_ref.shape
    cg = Cd // groups
    r = lax.broadcasted_iota(jnp.int32, (Lc, Lc), 0)
    c = lax.broadcasted_iota(jnp.int32, (Lc, Lc), 1)
    w = [jnp.where(c <= r, ws_ref[gi], 0.0).astype(BF16) for gi in range(groups)]
    for n in range(v_ref.shape[0] // Lc):
        rows = slice(n * Lc, (n + 1) * Lc)
        v = v_ref[rows]
        mu = jnp.mean(v, axis=-1, keepdims=True)
        xc = v - mu
        var = jnp.mean(xc * xc, axis=-1, keepdims=True)
        vn = xc * lax.rsqrt(var + EPS) * lng_ref[...] + lnb_ref[...]
        if vout:
            vout[0][rows] = vn
        vb = vn.astype(BF16)
        for gi in range(groups):
            sl = slice(gi * cg, (gi + 1) * cg)
            mixed = _dot(w[gi], vb[:, sl]) + bs_ref[:, sl]
            y_ref[rows, sl] = (u_ref[rows, sl] * mixed).astype(y_ref.dtype)


def _cmix(uv, ln_g, ln_b, ws, bs, row0, n_rows, Lc, y_dst, v_out):
    T, Cd = uv.shape[0], uv.shape[1] // 2
    G = ws.shape[0]
    tm = Lc * math.gcd(n_rows // Lc, CMIX_CHUNKS_PER_STEP)
    assert row0 % tm == 0
    r0 = row0 // tm
    tok = pl.BlockSpec((tm, Cd), lambda i: (i + r0, 0))
    tok_v = pl.BlockSpec((tm, Cd), lambda i: (i + r0, 1))
    vec = pl.BlockSpec((1, Cd), lambda i: (0, 0))
    out_specs, out_shapes, dsts = [tok], [jax.ShapeDtypeStruct((T, Cd), BF16)], [y_dst]
    if v_out:
        layer, n_layers, dst = v_out
        out_specs.append(pl.BlockSpec((None, tm, Cd), lambda i: (layer, i, 0)))
        out_shapes.append(jax.ShapeDtypeStruct((n_layers, n_rows, Cd), F32))
        dsts.append(dst)
    return _call(
        functools.partial(_cmix_body, groups=G), (n_rows // tm,),
        [tok, tok_v, vec, vec, pl.BlockSpec((G, Lc, Lc), lambda i: (0, 0, 0)),
         pl.BlockSpec((Lc, Cd), lambda i: (0, 0))],
        [uv, uv, ln_g.reshape(1, Cd), ln_b.reshape(1, Cd), ws, bs],
        out_specs, out_shapes, ("parallel",), dsts=dsts)


def _epi_gelu(z):
    return (jax.nn.gelu(z),)


def _epi_hgrn_forget(z, lb_all, *, layer):
    rows = [lb_all[n:n + 1] for n in range(lb_all.shape[0])]
    top = functools.reduce(jnp.maximum, rows)
    e = [jnp.exp(r - top) for r in rows]
    total = functools.reduce(jnp.add, e)
    sm = [a / total for a in e]
    lb = functools.reduce(jnp.add, sm[:layer + 1]) - sm[0]
    f = lb + (1.0 - lb) * jax.nn.sigmoid(z)
    g = jnp.log(jnp.maximum(f, TINY))
    k = (1.0 - lb) * jax.nn.sigmoid(-z)
    return g, k


def kernel(x_prompt, x_sample, cache_k, cache_v, cache_logf, state_hgrn, norm_ffn1, ffn1_gate, ffn1_up, ffn1_down, norm_mix, ab_w_in, ab_b_f, hgrn_lb, hgrn_gnorm, ab_w_out, c_w_in, c_ln_g, c_ln_b, c_w_s, c_b_s, c_w_out, norm_ffn2, ffn2_gate, ffn2_up, ffn2_down, norm_final):
    B, S, D = x_prompt.shape
    DB, DT, _ = x_sample.shape
    depth = norm_ffn1.shape[0]
    NAB, NC = ab_w_in.shape[0], c_w_in.shape[0]
    HA, P, HD = cache_k.shape[2], cache_k.shape[3], cache_k.shape[4]
    HB, DK, DV = state_hgrn.shape[2], state_hgrn.shape[3], state_hgrn.shape[4]
    assert HD == LANES and DK == LANES and DV == LANES and HA <= 8
    WA, WBK, WBV = HA * HD, HB * DK, HB * DV
    G, CL = c_w_s.shape[1], c_w_s.shape[2]
    CD = c_w_out.shape[1]
    Tp, Ts = B * S, DB * DT
    T = Tp + Ts
    bf = lambda a: a.astype(BF16)

    pk = pv = plf = ps = sk = sv = slf = ss = scv = None
    x = None
    for l in range(depth):
        j = l // 2
        w1 = (norm_ffn1[l], ffn1_gate, ffn1_up, ffn1_down, l, norm_mix[l])
        if l == 0:
            x, h = _ffn(x_prompt.reshape(Tp, D), *w1, out_rows=T)
            x, h = _ffn(x_sample.reshape(Ts, D), *w1, out_rows=T, out_row0=Tp, dsts=(x, h))
        else:
            x, h = _ffn(x, *w1)
        if l % 2 == 0:
            w_in = ab_w_in[j]
            o1 = 3 * WA + HA
            w_qkvf = bf(jnp.pad(w_in[:, :o1], ((0, 0), (0, LANES - HA))))
            w_b = bf(w_in[:, o1:])
            b_f = jnp.pad(ab_b_f[j], (0, LANES - HA)).reshape(1, LANES)
            qs, sk, ksb, sv, vsb, lf_tok = _run_stages(
                [_proj_qkv_stage(h, w_qkvf, b_f, Tp, DB, DT, j, NAB, sk, sv, None)])
            qp, pk, kpb, pv, vpb, lf_tok = _run_stages(
                [_proj_qkv_stage(h, w_qkvf, b_f, 0, B, S, j, NAB, pk, pv, lf_tok)])
            assert WBK == WBV
            zb, kb = _proj_hgrn(h, w_b, hgrn_lb, j)

            _, lp_row, plf = _cumsum(lf_tok, jnp.zeros((Tp, LANES), F32), 0, B, S, HA, j, NAB, plf)
            lc_row = _row_cumsum(cache_logf.reshape(NAB * DB * HA, P), j * DB * HA, DB * HA).reshape(DB, HA, P)
            base = jnp.pad(lc_row[:, :, P - 1], ((0, 0), (0, LANES - HA)))
            base = jnp.broadcast_to(base[:, None, :], (DB, DT, LANES)).reshape(Ts, LANES)
            ls_tok, ls_row, slf = _cumsum(lf_tok, base, Tp, DB, DT, HA, j, NAB, slf)

            rec_p = _hgrn_stage(zb, kb, hgrn_gnorm[j], None, 0, B, S, HB, j, NAB, None, ps)
            fox_s = functools.partial(_fox_sample_stage, qs, cache_k, cache_v, j, ksb, vsb, ls_tok,
                                      lc_row, ls_row, Tp, T, None)
            hps = DB * HA // rec_p["steps"]
            if 1 <= hps <= FOX_SAMPLE_MAX_HEADS_PER_STEP and HA % hps == 0 and DB * (HA // hps) == rec_p["steps"]:
                ob, ps, oa = _run_stages([rec_p, fox_s(hps)])
            else:
                ob, ps = _run_stages([rec_p])
                (oa,) = _run_stages([fox_s(math.gcd(HA, FOX_SAMPLE_HEADS_PER_STEP))])
            oa = _fox_prompt(qp, kpb, vpb, lp_row, T, oa)
            ob, ss = _run_stages([_hgrn_stage(zb, kb, hgrn_gnorm[j], state_hgrn, Tp, DB, DT, HB, j, NAB, ob, ss)])

            x = _mm_res([oa, ob], ab_w_out, j, x)
        else:
            (uv,) = _proj(h, bf(c_w_in[j]), [], _epi_gelu, [F32], 512)
            y = None
            for row0, n_rows, seq in ((0, Tp, S), (Tp, Ts, DT)):
                Lc = min(seq, CL)
                ws = c_w_s[j][:, :Lc, :Lc]
                bs = jnp.repeat(c_b_s[j][:, :Lc].T, CD // G, axis=1)
                if row0 == 0:
                    (y,) = _cmix(uv, c_ln_g[j], c_ln_b[j], ws, bs, row0, n_rows, Lc, y, None)
                else:
                    y, scv = _cmix(uv, c_ln_g[j], c_ln_b[j], ws, bs, row0, n_rows, Lc, y, (j, NC, scv))
            x = _mm_res([y], c_w_out, j, x)
        x = _ffn(x, norm_ffn2[l], ffn2_gate, ffn2_up, ffn2_down, l)

    y_p = _norm(x, norm_final, F32, 0, Tp)
    y_s = _norm(x, norm_final, F32, Tp, Ts)
    return (y_p.reshape(B, S, D), y_s.reshape(DB, DT, D), pk, pv, plf, ps,
            sk, sv, slf, ss, scv.reshape(NC, DB, DT, CD))
```

```python
import functools
import math

import jax
import jax.numpy as jnp
from jax import lax
from jax.experimental import pallas as pl
from jax.experimental.pallas import tpu as pltpu

F32 = jnp.float32
BF16 = jnp.bfloat16
EPS = 1e-6
TINY = 1e-30
MASK_VALUE = -1e30
LOG2E = math.log2(math.e)
HIGHEST = lax.Precision.HIGHEST

LANES = 128
MIB = 1 << 20
VMEM_LIMIT = 60 * MIB
FFN_TOKEN_TILE = 1024
FFN_FF_TILE = 256
PROJ_COL_TILE = 512
HGRN_HEADS_PER_STEP = 8
HGRN_HEADS_PER_GROUP = 8
CMIX_CHUNKS_PER_STEP = 4
FOX_SAMPLE_HEADS_PER_STEP = 2
FOX_SAMPLE_MAX_HEADS_PER_STEP = 4
FOX_SAMPLE_KEY_SPLITS = 2


def _tile(n, target):
    if n <= target:
        return n
    for t in range(target, 7, -1):
        if n % t == 0 and t % 8 == 0:
            return t
    return n


def _call(body, grid, in_specs, args, out_specs, out_shapes, sem, dsts=None, scratch=()):
    in_specs, args = list(in_specs), list(args)
    n_in = len(args)
    aliases = {}
    for k, d in enumerate(dsts or ()):
        if d is not None:
            aliases[len(args)] = k
            in_specs.append(pl.BlockSpec(memory_space=pl.ANY))
            args.append(d)
    n_all = len(args)

    def wrapped(*refs):
        body(*refs[:n_in], *refs[n_all:])

    return pl.pallas_call(
        wrapped, grid=grid, in_specs=in_specs, out_specs=out_specs, out_shape=out_shapes,
        input_output_aliases=aliases, scratch_shapes=list(scratch),
        compiler_params=pltpu.CompilerParams(dimension_semantics=sem, vmem_limit_bytes=VMEM_LIMIT),
    )(*args)


def _run_stages(stages):
    steps = stages[0]["steps"]
    assert all(st["steps"] == steps for st in stages)
    n_in = [len(st["args"]) for st in stages]
    n_out = [len(st["out_specs"]) for st in stages]
    n_scr = [len(st["scratch"]) for st in stages]

    def body(*refs):
        ins, outs, scr = refs[:sum(n_in)], refs[sum(n_in):sum(n_in) + sum(n_out)], refs[sum(n_in) + sum(n_out):]
        a = b = c = 0
        for st, ni, no, ns in zip(stages, n_in, n_out, n_scr):
            st["body"](*ins[a:a + ni], *outs[b:b + no], *scr[c:c + ns])
            a, b, c = a + ni, b + no, c + ns

    cat = lambda key: [x for st in stages for x in st[key]]
    return _call(body, (steps,), cat("in_specs"), cat("args"), cat("out_specs"), cat("out_shapes"),
                 ("arbitrary",), dsts=cat("dsts"), scratch=cat("scratch"))


def _dot(a, b):
    return jnp.dot(a, b, preferred_element_type=F32)


def _dot_nt(a, b):
    return lax.dot_general(a, b, (((1,), (1,)), ((), ())), preferred_element_type=F32)


def _dot_tn(a, b):
    return lax.dot_general(a, b, (((0,), (0,)), ((), ())), preferred_element_type=F32)


def _rms(x, g):
    return x * lax.rsqrt(jnp.mean(x * x, axis=-1, keepdims=True) + EPS) * g


def _ffn_body(x_ref, g_ref, wg_ref, wu_ref, wd_ref, *refs, next_norm, final_norm):
    if next_norm:
        g2_ref, o_ref, h2_ref, h_ref = refs
    elif final_norm:
        g2_ref, o_ref, h_ref = refs
    else:
        o_ref, h_ref = refs

    @pl.when(pl.program_id(1) == 0)
    def _():
        x = x_ref[...]
        h_ref[...] = _rms(x, g_ref[...]).astype(BF16)
        o_ref[...] = x

    h = h_ref[...]
    a = _dot(h, wg_ref[...].astype(BF16))
    b = _dot(h, wu_ref[...].astype(BF16))
    act = (0.5 * a * jax.nn.sigmoid(a) * b).astype(BF16)
    o_ref[...] += _dot(act, wd_ref[...].astype(BF16))

    if next_norm or final_norm:
        @pl.when(pl.program_id(1) == pl.num_programs(1) - 1)
        def _():
            y = _rms(o_ref[...], g2_ref[...])
            if next_norm:
                h2_ref[...] = y.astype(h2_ref.dtype)
            else:
                o_ref[...] = y


def _ffn(x, g, wg, wu, wd, layer, g_next=None, g_final=None, in_row0=0, n_rows=None, out_rows=None, out_row0=0,
         dsts=(None, None)):
    D = x.shape[1]
    n_rows = n_rows or x.shape[0]
    F = wg.shape[2]
    tm = _tile(n_rows, FFN_TOKEN_TILE)
    tf = _tile(F, FFN_FF_TILE)
    out_rows = out_rows or n_rows
    assert out_row0 % tm == 0 and in_row0 % tm == 0 and (g_next is None or g_final is None)
    i0, i1 = out_row0 // tm, in_row0 // tm
    vec = pl.BlockSpec((1, D), lambda i, j: (0, 0))
    tok = pl.BlockSpec((tm, D), lambda i, j: (i + i0, 0))
    in_specs = [pl.BlockSpec((tm, D), lambda i, j: (i + i1, 0)), vec,
                pl.BlockSpec((None, D, tf), lambda i, j: (layer, 0, j)),
                pl.BlockSpec((None, D, tf), lambda i, j: (layer, 0, j)),
                pl.BlockSpec((None, tf, D), lambda i, j: (layer, j, 0))]
    args = [x, g.reshape(1, D), wg, wu, wd]
    out_specs, out_shapes = [tok], [jax.ShapeDtypeStruct((out_rows, D), F32)]
    if g_next is not None or g_final is not None:
        in_specs.append(vec)
        args.append((g_final if g_next is None else g_next).reshape(1, D))
    if g_next is not None:
        out_specs.append(tok)
        out_shapes.append(jax.ShapeDtypeStruct((out_rows, D), BF16))
    outs = _call(
        functools.partial(_ffn_body, next_norm=g_next is not None, final_norm=g_final is not None),
        (n_rows // tm, F // tf), in_specs, args, out_specs, out_shapes,
        ("parallel", "arbitrary"), dsts=list(dsts[:len(out_specs)]),
        scratch=[pltpu.VMEM((tm, D), BF16)])
    return outs if g_next is not None else outs[0]


def _resident(shape):
    return pl.BlockSpec(shape, lambda i: (0,) * len(shape), pipeline_mode=pl.Buffered(1))


def _col_tiles(n):
    tn = _tile(n, PROJ_COL_TILE)
    return [slice(c * tn, (c + 1) * tn) for c in range(n // tn)]


def _proj_body(h_ref, w_ref, *refs, epilogue, n_aux):
    aux, outs = refs[:n_aux], refs[n_aux:]
    h = h_ref[...]
    for cols in _col_tiles(w_ref.shape[1]):
        vals = epilogue(_dot(h, w_ref[:, cols]), *[a[:, cols] for a in aux])
        for o_ref, v in zip(outs, vals):
            o_ref[:, cols] = v.astype(o_ref.dtype)


def _proj(h, w, aux, epilogue, out_dtypes, tm_target):
    T, K = h.shape
    N = w.shape[1]
    tm = _tile(T, tm_target)
    return _call(
        functools.partial(_proj_body, epilogue=epilogue, n_aux=len(aux)),
        (T // tm,),
        [pl.BlockSpec((tm, K), lambda i: (i, 0)), _resident((K, N))] + [_resident(a.shape) for a in aux],
        [h, w, *aux],
        [pl.BlockSpec((tm, N), lambda i: (i, 0)) for _ in out_dtypes],
        [jax.ShapeDtypeStruct((T, N), d) for d in out_dtypes],
        ("parallel",))


def _proj_hgrn_body(h_ref, w_ref, lb_ref, main_ref, k_ref, *, layer):
    W = k_ref.shape[1]
    h = h_ref[...]
    for seg in range(4):
        for kc in _col_tiles(W):
            cols = slice(seg * W + kc.start, seg * W + kc.stop)
            z = _dot(h, w_ref[:, cols])
            if seg == 1:
                g, k = _epi_hgrn_forget(z, lb_ref[:, kc], layer=layer)
                main_ref[:, cols] = g
                k_ref[:, kc] = k
            else:
                main_ref[:, cols] = z if seg == 2 else z * jax.nn.sigmoid(z)


def _proj_hgrn(h, w, lb_all, layer):
    T, K = h.shape
    W = w.shape[1] // 4
    tm = _tile(T, 512)
    return _call(
        functools.partial(_proj_hgrn_body, layer=layer),
        (T // tm,),
        [pl.BlockSpec((tm, K), lambda i: (i, 0)), _resident((K, 4 * W)), _resident(lb_all.shape)],
        [h, w, lb_all],
        [pl.BlockSpec((tm, 4 * W), lambda i: (i, 0)), pl.BlockSpec((tm, W), lambda i: (i, 0))],
        [jax.ShapeDtypeStruct((T, 4 * W), F32), jax.ShapeDtypeStruct((T, W), F32)],
        ("parallel",))


def _qkv_body(h_ref, w_ref, bf_ref, q_ref, k_ref, kb_ref, v_ref, vb_ref, lf_ref, *, nb, ts):
    H = q_ref.shape[1]
    h = h_ref[...]
    gate_cols = slice(3 * H * LANES, 3 * H * LANES + LANES)
    lf_ref[...] = jax.nn.log_sigmoid(_dot(h, w_ref[:, gate_cols]) + bf_ref[...])
    targets = ((q_ref,), (k_ref, kb_ref), (v_ref, vb_ref))
    for cols in _col_tiles(3 * H * LANES):
        z = _dot(h, w_ref[:, cols])
        for n in range((cols.stop - cols.start) // LANES):
            head = cols.start // LANES + n
            zz = z[:, n * LANES:(n + 1) * LANES].reshape(nb, ts, LANES)
            for o_ref in targets[head // H]:
                o_ref[:, head % H] = zz.astype(o_ref.dtype)


def _proj_qkv_stage(h, w, b_f, row0, B, S, layer, n_layers, k_dst, v_dst, lf_dst, rows_per_step=512):
    T, K = h.shape
    H = (w.shape[1] - LANES) // (3 * LANES)
    if S >= rows_per_step:
        ts, nb = _tile(S, rows_per_step), 1
    else:
        nb = _tile(B, max(1, rows_per_step // S))
        ts = S
    tm = nb * ts
    spt = S // ts
    assert row0 % tm == 0
    i0 = row0 // tm
    if nb == 1:
        omap = lambda i: (i // spt, 0, i % spt, 0)
    else:
        omap = lambda i: (i, 0, 0, 0)
    plain = pl.BlockSpec((nb, H, ts, LANES), omap)
    slab = pl.BlockSpec((None, nb, H, ts, LANES), lambda i: (layer, *omap(i)))
    bf_shape = jax.ShapeDtypeStruct((B, H, S, LANES), BF16)
    slab_shape = jax.ShapeDtypeStruct((n_layers, B, H, S, LANES), F32)
    return dict(
        steps=B * S // tm, body=functools.partial(_qkv_body, nb=nb, ts=ts),
        in_specs=[pl.BlockSpec((tm, K), lambda i: (i + i0, 0)), _resident(w.shape), _resident(b_f.shape)],
        args=[h, w, b_f],
        out_specs=[plain, slab, plain, slab, plain, pl.BlockSpec((tm, LANES), lambda i: (i + i0, 0))],
        out_shapes=[bf_shape, slab_shape, bf_shape, slab_shape, bf_shape, jax.ShapeDtypeStruct((T, LANES), F32)],
        dsts=[None, k_dst, None, v_dst, None, lf_dst], scratch=[])


def _mm_res_body(*refs, n):
    lhs = [r[...] for r in refs[:n]]
    w_ref, x_ref, o_ref, wb_ref = refs[n:]

    @pl.when(pl.program_id(0) == 0)
    def _():
        wb_ref[...] = w_ref[...].astype(BF16)

    for cols in _col_tiles(x_ref.shape[1]):
        z, k0 = None, 0
        for a in lhs:
            part = _dot(a, wb_ref[k0:k0 + a.shape[1], cols])
            z = part if z is None else z + part
            k0 += a.shape[1]
        o_ref[:, cols] = x_ref[:, cols] + z


def _mm_res(lhs, w, layer, x):
    T, N = x.shape
    K = w.shape[1]
    assert sum(a.shape[1] for a in lhs) == K
    tm = _tile(T, 512)
    tok = pl.BlockSpec((tm, N), lambda i: (i, 0))
    return _call(
        functools.partial(_mm_res_body, n=len(lhs)), (T // tm,),
        ([pl.BlockSpec((tm, a.shape[1]), lambda i: (i, 0)) for a in lhs]
         + [pl.BlockSpec((None, K, N), lambda i: (layer, 0, 0), pipeline_mode=pl.Buffered(1)), tok]),
        [*lhs, w, x],
        tok, jax.ShapeDtypeStruct((T, N), F32), ("arbitrary",),
        scratch=[pltpu.VMEM((K, N), BF16)])


def _cumsum_body(lf_ref, base_ref, ltok_ref, lrow_ref, lfrow_ref, carry_ref, *, ts, seg, tiles_per_seq, n_heads):
    lf = lf_ref[...]
    r = lax.broadcasted_iota(jnp.int32, (ts, ts), 0)
    c = lax.broadcasted_iota(jnp.int32, (ts, ts), 1)
    tri = jnp.where(r >= c, 1.0, 0.0).astype(F32)
    if seg < ts:
        tri = jnp.where(r // seg == c // seg, tri, 0.0)
    L = jnp.dot(tri, lf, precision=HIGHEST, preferred_element_type=F32)
    if tiles_per_seq:
        @pl.when(pl.program_id(0) % tiles_per_seq == 0)
        def _():
            carry_ref[...] = jnp.zeros_like(carry_ref)
        L = L + carry_ref[...]
        carry_ref[...] = L[ts - 1:ts, :]
    else:
        L = L + base_ref[...]
    ltok_ref[...] = L
    Lt = L.T[:n_heads]
    lft = lf.T[:n_heads]
    for s in range(ts // seg):
        lrow_ref[s] = Lt[:, s * seg:(s + 1) * seg]
        lfrow_ref[s] = lft[:, s * seg:(s + 1) * seg]


def _cumsum(lf_tok, base, row0, B, S, n_heads, layer, n_layers, dst):
    if S >= LANES:
        ts = _tile(S, 512)
        seg, tps, nseg = ts, S // ts, 1
    else:
        ts, seg, tps = LANES, S, 0
        nseg = ts // seg
    assert row0 % ts == 0 and (B * S) % ts == 0
    i0 = row0 // ts
    if tps:
        omap = lambda i: (i // tps, 0, i % tps)
    else:
        omap = lambda i: (i, 0, 0)
    return _call(
        functools.partial(_cumsum_body, ts=ts, seg=seg, tiles_per_seq=tps, n_heads=n_heads),
        (B * S // ts,),
        [pl.BlockSpec((ts, LANES), lambda i: (i + i0, 0)), pl.BlockSpec((ts, LANES), lambda i: (i, 0))],
        [lf_tok, base],
        [pl.BlockSpec((ts, LANES), lambda i: (i, 0)),
         pl.BlockSpec((nseg, n_heads, seg), omap),
         pl.BlockSpec((None, nseg, n_heads, seg), lambda i: (layer, *omap(i)))],
        [jax.ShapeDtypeStruct((B * S, LANES), F32), jax.ShapeDtypeStruct((B, n_heads, S), F32),
         jax.ShapeDtypeStruct((n_layers, B, n_heads, S), F32)],
        ("arbitrary",), dsts=[None, None, dst], scratch=[pltpu.VMEM((1, LANES), F32)])


def _row_cumsum_body(x_ref, o_ref):
    R, P = x_ref.shape
    r = lax.broadcasted_iota(jnp.int32, (LANES, LANES), 0)
    c = lax.broadcasted_iota(jnp.int32, (LANES, LANES), 1)
    tri = jnp.where(r <= c, 1.0, 0.0).astype(F32)
    carry = jnp.zeros((R, 1), F32)
    for s in range(P // LANES):
        y = jnp.dot(x_ref[:, s * LANES:(s + 1) * LANES], tri, precision=HIGHEST, preferred_element_type=F32) + carry
        o_ref[:, s * LANES:(s + 1) * LANES] = y
        carry = y[:, LANES - 1:LANES]


def _row_cumsum(x, row0, n_rows):
    P = x.shape[1]
    tr = _tile(n_rows, 256)
    assert row0 % tr == 0
    i0 = row0 // tr
    return _call(
        _row_cumsum_body, (n_rows // tr,),
        [pl.BlockSpec((tr, P), lambda i: (i + i0, 0))], [x],
        pl.BlockSpec((tr, P), lambda i: (i, 0)),
        jax.ShapeDtypeStruct((n_rows, P), F32), ("parallel",))


def _head_row(rows, h):
    sublane = lax.broadcasted_iota(jnp.int32, rows.shape, 0)
    return jnp.sum(jnp.where(sublane == h, rows, 0.0), axis=0, keepdims=True)


def _fox_prompt_body(q_ref, k_ref, v_ref, l_ref, o_ref, m_ref, d_ref, acc_ref, s0_ref, s1_ref, *, scale, tq, sub):
    qi = pl.program_id(2)
    c1 = scale * LOG2E
    tk = tq // 2
    n_sub = tq // sub
    every = tuple(range(n_sub))
    m_ref[...] = jnp.full_like(m_ref, MASK_VALUE)
    d_ref[...] = jnp.zeros_like(d_ref)
    acc_ref[...] = jnp.zeros_like(acc_ref)
    head = pl.program_id(1)
    l_here = _head_row(l_ref[0, :, pl.ds(pl.multiple_of(qi * tq, tq), LANES)], head)[:, 0:1]

    def scores(ki, s_ref, subs):
        k0 = pl.multiple_of(ki * tk, tk)
        k = k_ref[0, 0, pl.ds(k0, tk), :]
        bias = (l_here - _head_row(l_ref[0, :, pl.ds(k0, tk)], head)) * LOG2E
        for u in subs:
            rows = slice(u * sub, (u + 1) * sub)
            s_ref[rows] = _dot_nt(q_ref[0, 0, rows, :], k) * c1 + bias

    def update(ki, s_ref, subs, diagonal=None):
        k0 = pl.multiple_of(ki * tk, tk)
        v = v_ref[0, 0, pl.ds(k0, tk), :]
        for n in range(0, len(subs), 2):
            update_pair(s_ref, v, subs[n:n + 2], diagonal)

    def update_pair(s_ref, v, subs, diagonal):
        rows = [slice(u * sub, (u + 1) * sub) for u in subs]
        s = []
        for u, rw in zip(subs, rows):
            su = s_ref[rw]
            if diagonal is not None and u * sub < (diagonal + 1) * tk - 1:
                r = lax.broadcasted_iota(jnp.int32, su.shape, 0) + u * sub
                c = lax.broadcasted_iota(jnp.int32, su.shape, 1) + diagonal * tk
                su = jnp.where(c <= r, su, MASK_VALUE)
            s.append(su)
        m_prev = [m_ref[rw] for rw in rows]
        m_new = [jnp.maximum(mp, jnp.max(su, axis=-1, keepdims=True)) for mp, su in zip(m_prev, s)]
        alpha = [jnp.exp2(mp - mn) for mp, mn in zip(m_prev, m_new)]
        p = [jnp.exp2(su - mn) for su, mn in zip(s, m_new)]
        for n, rw in enumerate(rows):
            d_ref[rw] = alpha[n] * d_ref[rw] + jnp.sum(p[n], axis=-1, keepdims=True)
            m_ref[rw] = m_new[n]
        for n, rw in enumerate(rows):
            acc_ref[rw] = alpha[n] * acc_ref[rw] + _dot(p[n].astype(BF16), v)

    def two_blocks(t, carry):
        scores(2 * t + 1, s1_ref, every)
        update(2 * t, s0_ref, every)
        scores(2 * t + 2, s0_ref, every)
        update(2 * t + 1, s1_ref, every)
        return carry

    scores(0, s0_ref, every)
    lax.fori_loop(0, qi, two_blocks, 0)
    late = tuple(u for u in every if (u + 1) * sub > tk)
    scores(2 * qi + 1, s1_ref, late)
    update(2 * qi, s0_ref, every, diagonal=0)
    update(2 * qi + 1, s1_ref, late, diagonal=1)
    o_ref[...] = (acc_ref[...] / d_ref[...]).astype(o_ref.dtype)


def _fox_prompt(q, k, v, l_row, out_rows, dst):
    B, H, S, D = q.shape
    tq = _tile(S, 1024)
    sub = min(256, tq // 2)
    assert tq % (2 * LANES) == 0 and (tq // 2) % sub == 0
    nq = S // tq
    full = pl.BlockSpec((1, 1, S, D), lambda b, h, qi: (b, h, 0, 0))
    return _call(
        functools.partial(_fox_prompt_body, scale=D ** -0.5, tq=tq, sub=sub),
        (B, H, nq),
        [pl.BlockSpec((1, 1, tq, D), lambda b, h, qi: (b, h, qi, 0)), full, full,
         pl.BlockSpec((1, H, S), lambda b, h, qi: (b, 0, 0))],
        [q, k, v, l_row],
        pl.BlockSpec((tq, D), lambda b, h, qi: (b * nq + qi, h)),
        jax.ShapeDtypeStruct((out_rows, H * D), BF16),
        ("parallel", "parallel", "arbitrary"), dsts=[dst],
        scratch=[pltpu.VMEM((tq, 1), F32), pltpu.VMEM((tq, 1), F32), pltpu.VMEM((tq, D), F32),
                 pltpu.VMEM((tq, tq // 2), F32), pltpu.VMEM((tq, tq // 2), F32)])


def _head_column(l_tok, h):
    lane = lax.broadcasted_iota(jnp.int32, l_tok.shape, 1)
    return jnp.sum(jnp.where(lane == h, l_tok, 0.0), axis=-1, keepdims=True)


def _fox_sample_body(q_ref, *refs, scale, hps, n_split, head_groups):
    ck_refs, cv_refs = refs[:n_split], refs[n_split:2 * n_split]
    kn_ref, vn_ref, lq_ref, lc_ref, ln_ref, o_ref = refs[2 * n_split:]
    hg = pl.program_id(0) % head_groups
    pn = ck_refs[0].shape[-2]
    D = q_ref.shape[-1]
    for hh in range(hps):
        q = q_ref[0, hh]
        lq = _head_column(lq_ref[...], hg * hps + hh)
        s_c = [_dot_nt(q, ck_refs[n][0, hh].astype(BF16)) * scale + (lq - _head_row(lc_ref[0, :, n * pn:(n + 1) * pn], hg * hps + hh))
               for n in range(n_split)]
        s_n = _dot_nt(q, kn_ref[0, hh]) * scale + (lq - _head_row(ln_ref[0], hg * hps + hh))
        r = lax.broadcasted_iota(jnp.int32, s_n.shape, 0)
        c = lax.broadcasted_iota(jnp.int32, s_n.shape, 1)
        s_n = jnp.where(c <= r, s_n, MASK_VALUE)
        m = jnp.max(s_n, axis=-1, keepdims=True)
        for s in s_c:
            m = jnp.maximum(m, jnp.max(s, axis=-1, keepdims=True))
        p_n = jnp.exp(s_n - m)
        l = jnp.sum(p_n, axis=-1, keepdims=True)
        o = _dot(p_n.astype(BF16), vn_ref[0, hh])
        for n in range(n_split):
            p = jnp.exp(s_c[n] - m)
            l = l + jnp.sum(p, axis=-1, keepdims=True)
            o = o + _dot(p.astype(BF16), cv_refs[n][0, hh].astype(BF16))
        o_ref[:, hh * D:(hh + 1) * D] = (o / l).astype(o_ref.dtype)


def _fox_sample_stage(q, cache_k, cache_v, layer, kn, vn, ln_tok, lc_row, ln_row, out_row0, out_rows, dst, hps):
    B, H, T, D = q.shape
    P = cache_k.shape[3]
    assert out_row0 % T == 0 and H % hps == 0
    r0 = out_row0 // T
    ng = H // hps
    n_split = FOX_SAMPLE_KEY_SPLITS if P % (FOX_SAMPLE_KEY_SPLITS * LANES) == 0 else 1
    pn = P // n_split
    new = pl.BlockSpec((1, hps, T, D), lambda s: (s // ng, s % ng, 0, 0))
    old = [pl.BlockSpec((None, 1, hps, pn, D), lambda s, n=n: (layer, s // ng, s % ng, n, 0)) for n in range(n_split)]
    return dict(
        steps=B * ng,
        body=functools.partial(_fox_sample_body, scale=D ** -0.5, hps=hps, n_split=n_split, head_groups=ng),
        in_specs=[new, *old, *old, new, new,
                  pl.BlockSpec((T, LANES), lambda s: (s // ng, 0)),
                  pl.BlockSpec((1, H, P), lambda s: (s // ng, 0, 0)),
                  pl.BlockSpec((1, H, T), lambda s: (s // ng, 0, 0))],
        args=[q, *([cache_k] * n_split), *([cache_v] * n_split), kn, vn, ln_tok, lc_row, ln_row],
        out_specs=[pl.BlockSpec((T, hps * D), lambda s: (s // ng + r0, s % ng))],
        out_shapes=[jax.ShapeDtypeStruct((out_rows, H * D), BF16)],
        dsts=[dst], scratch=[])


def _hgrn_body(*refs, C, hpb, has_init, n_chunks):
    if has_init:
        q_ref, k_ref, g_ref, i_ref, gate_ref, gn_ref, s0_ref, o_ref, sf_ref, st_ref = refs
    else:
        q_ref, k_ref, g_ref, i_ref, gate_ref, gn_ref, o_ref, sf_ref, st_ref = refs
        s0_ref = None
    c = pl.program_id(0) % n_chunks

    @pl.when(c == 0)
    def _():
        for hh in range(hpb):
            st_ref[hh] = s0_ref[0, hh].T if has_init else jnp.zeros((LANES, LANES), F32)

    row = lax.broadcasted_iota(jnp.int32, (C, C), 0)
    col = lax.broadcasted_iota(jnp.int32, (C, C), 1)
    tri = jnp.where(row >= col, 1.0, 0.0).astype(F32)
    trow = lax.broadcasted_iota(jnp.int32, (C, LANES), 0)
    level = 31 - lax.clz(jnp.where(row > col, row ^ col, 0))
    level = jnp.where(row == col, C.bit_length() - 1, level)

    def recur(heads):
        sls = {hh: slice(hh * LANES, (hh + 1) * LANES) for hh in heads}
        q = {hh: q_ref[:, sls[hh]] for hh in heads}
        k = {hh: k_ref[:, sls[hh]] for hh in heads}
        v = {hh: i_ref[:, sls[hh]].astype(BF16) for hh in heads}
        g = {hh: g_ref[:, sls[hh]] * LOG2E for hh in heads}
        G = {hh: jnp.dot(tri, g[hh], precision=HIGHEST, preferred_element_type=F32) for hh in heads}
        st = {hh: st_ref[hh] for hh in heads}
        o = {hh: _dot_nt((q[hh] * jnp.exp2(G[hh])).astype(BF16), st[hh].astype(BF16)) for hh in heads}
        A = {hh: jnp.where(level == C.bit_length() - 1, _dot_nt(q[hh].astype(BF16), k[hh].astype(BF16)), 0.0)
             for hh in heads}
        yq = {hh: G[hh] - g[hh] for hh in heads}
        yk = dict(G)
        w, log_w = 1, 0
        while w < C:
            upper = (trow & w) != 0
            for hh in heads:
                qt = (q[hh] * jnp.exp2(G[hh] - yq[hh])).astype(BF16)
                kt = (k[hh] * jnp.exp2(yk[hh] - G[hh])).astype(BF16)
                A[hh] = jnp.where(level == log_w, _dot_nt(qt, kt), A[hh])
                yq[hh] = jnp.where(upper, pltpu.roll(yq[hh], w, 0), yq[hh])
                yk[hh] = jnp.where(upper, yk[hh], pltpu.roll(yk[hh], C - w, 0))
            w, log_w = 2 * w, log_w + 1
        for hh in heads:
            out = o[hh] + _dot(A[hh].astype(BF16), v[hh])
            g_last = G[hh][C - 1:C, :]
            ks = (k[hh] * jnp.exp2(g_last - G[hh])).astype(BF16)
            st_ref[hh] = st[hh] * jnp.exp2(g_last) + _dot_tn(v[hh], ks)
            o_ref[:, sls[hh]] = (_rms(out, gn_ref[...]) * gate_ref[:, sls[hh]]).astype(o_ref.dtype)

    group = math.gcd(hpb, HGRN_HEADS_PER_GROUP)
    for h0 in range(0, hpb, group):
        recur(range(h0, h0 + group))

    @pl.when(c == n_chunks - 1)
    def _():
        for hh in range(hpb):
            sf_ref[0, hh] = st_ref[hh].T


def _hgrn_stage(main, k, gnorm, s0, row0, B, S, H, layer, n_layers, o_dst, s_dst):
    T = k.shape[0]
    C = min(S, LANES)
    nc = S // C
    assert row0 % C == 0
    r0 = row0 // C
    hpb = math.gcd(H, HGRN_HEADS_PER_STEP)
    nhb = H // hpb
    seq = lambda s: s // (nhb * nc)
    hblk = lambda s: (s // nc) % nhb
    row = lambda s: r0 + seq(s) * nc + s % nc
    tok = pl.BlockSpec((C, hpb * LANES), lambda s: (row(s), hblk(s)))
    seg = lambda n: pl.BlockSpec((C, hpb * LANES), lambda s: (row(s), n * nhb + hblk(s)))
    state = pl.BlockSpec((None, 1, hpb, LANES, LANES), lambda s: (layer, seq(s), hblk(s), 0, 0))
    in_specs = [seg(0), tok, seg(1), seg(2), seg(3), pl.BlockSpec((1, LANES), lambda s: (0, 0))]
    args = [main, k, main, main, main, gnorm.reshape(1, LANES)]
    if s0 is not None:
        in_specs.append(state)
        args.append(s0)
    return dict(
        steps=B * nhb * nc,
        body=functools.partial(_hgrn_body, C=C, hpb=hpb, has_init=s0 is not None, n_chunks=nc),
        in_specs=in_specs, args=args, out_specs=[tok, state],
        out_shapes=[jax.ShapeDtypeStruct((T, H * LANES), BF16),
                    jax.ShapeDtypeStruct((n_layers, B, H, LANES, LANES), F32)],
        dsts=[o_dst, s_dst], scratch=[pltpu.VMEM((hpb, LANES, LANES), F32)])


def _cmix_body(u_ref, v_ref, lng_ref, lnb_ref, ws_ref, bs_ref, y_ref, *vout, groups):
    Lc, Cd = bs_ref.shape
    cg = Cd // groups
    r = lax.broadcasted_iota(jnp.int32, (Lc, Lc), 0)
    c = lax.broadcasted_iota(jnp.int32, (Lc, Lc), 1)
    w = [jnp.where(c <= r, ws_ref[gi], 0.0).astype(BF16) for gi in range(groups)]
    for n in range(v_ref.shape[0] // Lc):
        rows = slice(n * Lc, (n + 1) * Lc)
        v = v_ref[rows]
        mu = jnp.mean(v, axis=-1, keepdims=True)
        xc = v - mu
        var = jnp.mean(xc * xc, axis=-1, keepdims=True)
        vn = xc * lax.rsqrt(var + EPS) * lng_ref[...] + lnb_ref[...]
        if vout:
            vout[0][rows] = vn
        vb = vn.astype(BF16)
        for gi in range(groups):
            sl = slice(gi * cg, (gi + 1) * cg)
            mixed = _dot(w[gi], vb[:, sl]) + bs_ref[:, sl]
            y_ref[rows, sl] = (u_ref[rows, sl] * mixed).astype(y_ref.dtype)


def _cmix(uv, ln_g, ln_b, ws, bs, row0, n_rows, Lc, y_dst, v_out):
    T, Cd = uv.shape[0], uv.shape[1] // 2
    G = ws.shape[0]
    tm = Lc * math.gcd(n_rows // Lc, CMIX_CHUNKS_PER_STEP)
    assert row0 % tm == 0
    r0 = row0 // tm
    tok = pl.BlockSpec((tm, Cd), lambda i: (i + r0, 0))
    tok_v = pl.BlockSpec((tm, Cd), lambda i: (i + r0, 1))
    vec = pl.BlockSpec((1, Cd), lambda i: (0, 0))
    out_specs, out_shapes, dsts = [tok], [jax.ShapeDtypeStruct((T, Cd), BF16)], [y_dst]
    if v_out:
        layer, n_layers, dst = v_out
        out_specs.append(pl.BlockSpec((None, tm, Cd), lambda i: (layer, i, 0)))
        out_shapes.append(jax.ShapeDtypeStruct((n_layers, n_rows, Cd), F32))
        dsts.append(dst)
    return _call(
        functools.partial(_cmix_body, groups=G), (n_rows // tm,),
        [tok, tok_v, vec, vec, pl.BlockSpec((G, Lc, Lc), lambda i: (0, 0, 0)),
         pl.BlockSpec((Lc, Cd), lambda i: (0, 0))],
        [uv, uv, ln_g.reshape(1, Cd), ln_b.reshape(1, Cd), ws, bs],
        out_specs, out_shapes, ("parallel",), dsts=dsts)


def _epi_gelu(z):
    return (jax.nn.gelu(z),)


def _epi_hgrn_forget(z, lb_all, *, layer):
    rows = [lb_all[n:n + 1] for n in range(lb_all.shape[0])]
    top = functools.reduce(jnp.maximum, rows)
    e = [jnp.exp(r - top) for r in rows]
    total = functools.reduce(jnp.add, e)
    sm = [a / total for a in e]
    lb = functools.reduce(jnp.add, sm[:layer + 1]) - sm[0]
    f = lb + (1.0 - lb) * jax.nn.sigmoid(z)
    g = jnp.log(jnp.maximum(f, TINY))
    k = (1.0 - lb) * jax.nn.sigmoid(-z)
    return g, k


def kernel(x_prompt, x_sample, cache_k, cache_v, cache_logf, state_hgrn, norm_ffn1, ffn1_gate, ffn1_up, ffn1_down, norm_mix, ab_w_in, ab_b_f, hgrn_lb, hgrn_gnorm, ab_w_out, c_w_in, c_ln_g, c_ln_b, c_w_s, c_b_s, c_w_out, norm_ffn2, ffn2_gate, ffn2_up, ffn2_down, norm_final):
    B, S, D = x_prompt.shape
    DB, DT, _ = x_sample.shape
    depth = norm_ffn1.shape[0]
    NAB, NC = ab_w_in.shape[0], c_w_in.shape[0]
    HA, P, HD = cache_k.shape[2], cache_k.shape[3], cache_k.shape[4]
    HB, DK, DV = state_hgrn.shape[2], state_hgrn.shape[3], state_hgrn.shape[4]
    assert HD == LANES and DK == LANES and DV == LANES and HA <= 8
    WA, WBK, WBV = HA * HD, HB * DK, HB * DV
    G, CL = c_w_s.shape[1], c_w_s.shape[2]
    CD = c_w_out.shape[1]
    Tp, Ts = B * S, DB * DT
    T = Tp + Ts
    bf = lambda a: a.astype(BF16)

    pk = pv = plf = ps = sk = sv = slf = ss = scv = None
    x = None
    for l in range(depth):
        j = l // 2
        w1 = (norm_ffn1[l], ffn1_gate, ffn1_up, ffn1_down, l, norm_mix[l])
        if l == 0:
            x, h = _ffn(x_prompt.reshape(Tp, D), *w1, out_rows=T)
            x, h = _ffn(x_sample.reshape(Ts, D), *w1, out_rows=T, out_row0=Tp, dsts=(x, h))
        else:
            x, h = _ffn(x, *w1)
        if l % 2 == 0:
            w_in = ab_w_in[j]
            o1 = 3 * WA + HA
            w_qkvf = bf(jnp.pad(w_in[:, :o1], ((0, 0), (0, LANES - HA))))
            w_b = bf(w_in[:, o1:])
            b_f = jnp.pad(ab_b_f[j], (0, LANES - HA)).reshape(1, LANES)
            qs, sk, ksb, sv, vsb, lf_tok = _run_stages(
                [_proj_qkv_stage(h, w_qkvf, b_f, Tp, DB, DT, j, NAB, sk, sv, None)])
            qp, pk, kpb, pv, vpb, lf_tok = _run_stages(
                [_proj_qkv_stage(h, w_qkvf, b_f, 0, B, S, j, NAB, pk, pv, lf_tok)])
            assert WBK == WBV
            zb, kb = _proj_hgrn(h, w_b, hgrn_lb, j)

            _, lp_row, plf = _cumsum(lf_tok, jnp.zeros((Tp, LANES), F32), 0, B, S, HA, j, NAB, plf)
            lc_row = _row_cumsum(cache_logf.reshape(NAB * DB * HA, P), j * DB * HA, DB * HA).reshape(DB, HA, P)
            base = jnp.pad(lc_row[:, :, P - 1], ((0, 0), (0, LANES - HA)))
            base = jnp.broadcast_to(base[:, None, :], (DB, DT, LANES)).reshape(Ts, LANES)
            ls_tok, ls_row, slf = _cumsum(lf_tok, base, Tp, DB, DT, HA, j, NAB, slf)

            rec_p = _hgrn_stage(zb, kb, hgrn_gnorm[j], None, 0, B, S, HB, j, NAB, None, ps)
            fox_s = functools.partial(_fox_sample_stage, qs, cache_k, cache_v, j, ksb, vsb, ls_tok,
                                      lc_row, ls_row, Tp, T, None)
            hps = DB * HA // rec_p["steps"]
            if 1 <= hps <= FOX_SAMPLE_MAX_HEADS_PER_STEP and HA % hps == 0 and DB * (HA // hps) == rec_p["steps"]:
                ob, ps, oa = _run_stages([rec_p, fox_s(hps)])
            else:
                ob, ps = _run_stages([rec_p])
                (oa,) = _run_stages([fox_s(math.gcd(HA, FOX_SAMPLE_HEADS_PER_STEP))])
            oa = _fox_prompt(qp, kpb, vpb, lp_row, T, oa)
            ob, ss = _run_stages([_hgrn_stage(zb, kb, hgrn_gnorm[j], state_hgrn, Tp, DB, DT, HB, j, NAB, ob, ss)])

            x = _mm_res([oa, ob], ab_w_out, j, x)
        else:
            (uv,) = _proj(h, bf(c_w_in[j]), [], _epi_gelu, [F32], 512)
            y = None
            for row0, n_rows, seq in ((0, Tp, S), (Tp, Ts, DT)):
                Lc = min(seq, CL)
                ws = c_w_s[j][:, :Lc, :Lc]
                bs = jnp.repeat(c_b_s[j][:, :Lc].T, CD // G, axis=1)
                if row0 == 0:
                    (y,) = _cmix(uv, c_ln_g[j], c_ln_b[j], ws, bs, row0, n_rows, Lc, y, None)
                else:
                    y, scv = _cmix(uv, c_ln_g[j], c_ln_b[j], ws, bs, row0, n_rows, Lc, y, (j, NC, scv))
            x = _mm_res([y], c_w_out, j, x)
        w2 = (norm_ffn2[l], ffn2_gate, ffn2_up, ffn2_down, l)
        if l + 1 < depth:
            x = _ffn(x, *w2)
        else:
            y_p = _ffn(x, *w2, g_final=norm_final, in_row0=0, n_rows=Tp)
            y_s = _ffn(x, *w2, g_final=norm_final, in_row0=Tp, n_rows=Ts)
    return (y_p.reshape(B, S, D), y_s.reshape(DB, DT, D), pk, pv, plf, ps,
            sk, sv, slf, ss, scv.reshape(NC, DB, DT, CD))
```

```python
import functools
import math

import jax
import jax.numpy as jnp
from jax import lax
from jax.experimental import pallas as pl
from jax.experimental.pallas import tpu as pltpu

F32 = jnp.float32
BF16 = jnp.bfloat16
EPS = 1e-6
TINY = 1e-30
MASK_VALUE = -1e30
LOG2E = math.log2(math.e)
HIGHEST = lax.Precision.HIGHEST

LANES = 128
MIB = 1 << 20
VMEM_LIMIT = 60 * MIB
FFN_TOKEN_TILE = 1024
FFN_FF_TILE = 256
PROJ_COL_TILE = 512
HGRN_HEADS_PER_STEP = 8
HGRN_HEADS_PER_GROUP = 8
CMIX_CHUNKS_PER_STEP = 4
FOX_SAMPLE_HEADS_PER_STEP = 2
FOX_SAMPLE_MAX_HEADS_PER_STEP = 4
FOX_SAMPLE_KEY_SPLITS = 2


def _tile(n, target):
    if n <= target:
        return n
    for t in range(target, 7, -1):
        if n % t == 0 and t % 8 == 0:
            return t
    return n


def _call(body, grid, in_specs, args, out_specs, out_shapes, sem, dsts=None, scratch=()):
    in_specs, args = list(in_specs), list(args)
    n_in = len(args)
    aliases = {}
    for k, d in enumerate(dsts or ()):
        if d is not None:
            aliases[len(args)] = k
            in_specs.append(pl.BlockSpec(memory_space=pl.ANY))
            args.append(d)
    n_all = len(args)

    def wrapped(*refs):
        body(*refs[:n_in], *refs[n_all:])

    return pl.pallas_call(
        wrapped, grid=grid, in_specs=in_specs, out_specs=out_specs, out_shape=out_shapes,
        input_output_aliases=aliases, scratch_shapes=list(scratch),
        compiler_params=pltpu.CompilerParams(dimension_semantics=sem, vmem_limit_bytes=VMEM_LIMIT),
    )(*args)


def _run_stages(stages):
    steps = stages[0]["steps"]
    assert all(st["steps"] == steps for st in stages)
    n_in = [len(st["args"]) for st in stages]
    n_out = [len(st["out_specs"]) for st in stages]
    n_scr = [len(st["scratch"]) for st in stages]

    def body(*refs):
        ins, outs, scr = refs[:sum(n_in)], refs[sum(n_in):sum(n_in) + sum(n_out)], refs[sum(n_in) + sum(n_out):]
        a = b = c = 0
        for st, ni, no, ns in zip(stages, n_in, n_out, n_scr):
            st["body"](*ins[a:a + ni], *outs[b:b + no], *scr[c:c + ns])
            a, b, c = a + ni, b + no, c + ns

    cat = lambda key: [x for st in stages for x in st[key]]
    return _call(body, (steps,), cat("in_specs"), cat("args"), cat("out_specs"), cat("out_shapes"),
                 ("arbitrary",), dsts=cat("dsts"), scratch=cat("scratch"))


def _dot(a, b):
    return jnp.dot(a, b, preferred_element_type=F32)


def _dot_nt(a, b):
    return lax.dot_general(a, b, (((1,), (1,)), ((), ())), preferred_element_type=F32)


def _dot_tn(a, b):
    return lax.dot_general(a, b, (((0,), (0,)), ((), ())), preferred_element_type=F32)


def _rms(x, g):
    return x * lax.rsqrt(jnp.mean(x * x, axis=-1, keepdims=True) + EPS) * g


def _ffn_body(x_ref, g_ref, wg_ref, wu_ref, wd_ref, *refs, next_norm, final_norm):
    if next_norm:
        g2_ref, o_ref, h2_ref, h_ref = refs
    elif final_norm:
        g2_ref, o_ref, h_ref = refs
    else:
        o_ref, h_ref = refs

    @pl.when(pl.program_id(1) == 0)
    def _():
        x = x_ref[...]
        h_ref[...] = _rms(x, g_ref[...]).astype(BF16)
        o_ref[...] = x

    h = h_ref[...]
    a = _dot(h, wg_ref[...].astype(BF16))
    b = _dot(h, wu_ref[...].astype(BF16))
    act = (0.5 * a * jax.nn.sigmoid(a) * b).astype(BF16)
    o_ref[...] += _dot(act, wd_ref[...].astype(BF16))

    if next_norm or final_norm:
        @pl.when(pl.program_id(1) == pl.num_programs(1) - 1)
        def _():
            y = _rms(o_ref[...], g2_ref[...])
            if next_norm:
                h2_ref[...] = y.astype(h2_ref.dtype)
            else:
                o_ref[...] = y


def _ffn(x, g, wg, wu, wd, layer, g_next=None, g_final=None, in_row0=0, n_rows=None, out_rows=None, out_row0=0,
         dsts=(None, None)):
    D = x.shape[1]
    n_rows = n_rows or x.shape[0]
    F = wg.shape[2]
    tm = _tile(n_rows, FFN_TOKEN_TILE)
    tf = _tile(F, FFN_FF_TILE)
    out_rows = out_rows or n_rows
    assert out_row0 % tm == 0 and in_row0 % tm == 0 and (g_next is None or g_final is None)
    i0, i1 = out_row0 // tm, in_row0 // tm
    vec = pl.BlockSpec((1, D), lambda i, j: (0, 0))
    tok = pl.BlockSpec((tm, D), lambda i, j: (i + i0, 0))
    in_specs = [pl.BlockSpec((tm, D), lambda i, j: (i + i1, 0)), vec,
                pl.BlockSpec((None, D, tf), lambda i, j: (layer, 0, j)),
                pl.BlockSpec((None, D, tf), lambda i, j: (layer, 0, j)),
                pl.BlockSpec((None, tf, D), lambda i, j: (layer, j, 0))]
    args = [x, g.reshape(1, D), wg, wu, wd]
    out_specs, out_shapes = [tok], [jax.ShapeDtypeStruct((out_rows, D), F32)]
    if g_next is not None or g_final is not None:
        in_specs.append(vec)
        args.append((g_final if g_next is None else g_next).reshape(1, D))
    if g_next is not None:
        out_specs.append(tok)
        out_shapes.append(jax.ShapeDtypeStruct((out_rows, D), BF16))
    outs = _call(
        functools.partial(_ffn_body, next_norm=g_next is not None, final_norm=g_final is not None),
        (n_rows // tm, F // tf), in_specs, args, out_specs, out_shapes,
        ("parallel", "arbitrary"), dsts=list(dsts[:len(out_specs)]),
        scratch=[pltpu.VMEM((tm, D), BF16)])
    return outs if g_next is not None else outs[0]


def _resident(shape):
    return pl.BlockSpec(shape, lambda i: (0,) * len(shape), pipeline_mode=pl.Buffered(1))


def _col_tiles(n):
    tn = _tile(n, PROJ_COL_TILE)
    return [slice(c * tn, (c + 1) * tn) for c in range(n // tn)]


def _proj_body(h_ref, w_ref, *refs, epilogue, n_aux):
    aux, outs = refs[:n_aux], refs[n_aux:]
    h = h_ref[...]
    for cols in _col_tiles(w_ref.shape[1]):
        vals = epilogue(_dot(h, w_ref[:, cols]), *[a[:, cols] for a in aux])
        for o_ref, v in zip(outs, vals):
            o_ref[:, cols] = v.astype(o_ref.dtype)


def _proj(h, w, aux, epilogue, out_dtypes, tm_target):
    T, K = h.shape
    N = w.shape[1]
    tm = _tile(T, tm_target)
    return _call(
        functools.partial(_proj_body, epilogue=epilogue, n_aux=len(aux)),
        (T // tm,),
        [pl.BlockSpec((tm, K), lambda i: (i, 0)), _resident((K, N))] + [_resident(a.shape) for a in aux],
        [h, w, *aux],
        [pl.BlockSpec((tm, N), lambda i: (i, 0)) for _ in out_dtypes],
        [jax.ShapeDtypeStruct((T, N), d) for d in out_dtypes],
        ("parallel",))


def _proj_hgrn_body(h_ref, w_ref, lb_ref, main_ref, k_ref, *, layer):
    W = k_ref.shape[1]
    h = h_ref[...]
    for seg in range(4):
        for kc in _col_tiles(W):
            cols = slice(seg * W + kc.start, seg * W + kc.stop)
            z = _dot(h, w_ref[:, cols])
            if seg == 1:
                g, k = _epi_hgrn_forget(z, lb_ref[:, kc], layer=layer)
                main_ref[:, cols] = g
                k_ref[:, kc] = k
            else:
                main_ref[:, cols] = z if seg == 2 else z * jax.nn.sigmoid(z)


def _proj_hgrn(h, w, lb_all, layer):
    T, K = h.shape
    W = w.shape[1] // 4
    tm = _tile(T, 512)
    return _call(
        functools.partial(_proj_hgrn_body, layer=layer),
        (T // tm,),
        [pl.BlockSpec((tm, K), lambda i: (i, 0)), _resident((K, 4 * W)), _resident(lb_all.shape)],
        [h, w, lb_all],
        [pl.BlockSpec((tm, 4 * W), lambda i: (i, 0)), pl.BlockSpec((tm, W), lambda i: (i, 0))],
        [jax.ShapeDtypeStruct((T, 4 * W), F32), jax.ShapeDtypeStruct((T, W), F32)],
        ("parallel",))


def _qkv_body(h_ref, w_ref, bf_ref, q_ref, k_ref, kb_ref, v_ref, vb_ref, lf_ref, *, nb, ts):
    H = q_ref.shape[1]
    h = h_ref[...]
    gate_cols = slice(3 * H * LANES, 3 * H * LANES + LANES)
    lf_ref[...] = jax.nn.log_sigmoid(_dot(h, w_ref[:, gate_cols]) + bf_ref[...])
    lane = lax.broadcasted_iota(jnp.int32, (nb, ts, LANES), 2)
    one_hot = jnp.where(lane == 0, 1.0, 0.0).astype(vb_ref.dtype)
    for hd in range(H):
        vb_ref[:, hd, :, LANES:] = one_hot
    targets = ((q_ref,), (k_ref, kb_ref), (v_ref, vb_ref))
    for cols in _col_tiles(3 * H * LANES):
        z = _dot(h, w_ref[:, cols])
        for n in range((cols.stop - cols.start) // LANES):
            head = cols.start // LANES + n
            zz = z[:, n * LANES:(n + 1) * LANES].reshape(nb, ts, LANES)
            for o_ref in targets[head // H]:
                o_ref[:, head % H, :, :LANES] = zz.astype(o_ref.dtype)


def _proj_qkv_stage(h, w, b_f, row0, B, S, layer, n_layers, k_dst, v_dst, lf_dst, rows_per_step=512):
    T, K = h.shape
    H = (w.shape[1] - LANES) // (3 * LANES)
    if S >= rows_per_step:
        ts, nb = _tile(S, rows_per_step), 1
    else:
        nb = _tile(B, max(1, rows_per_step // S))
        ts = S
    tm = nb * ts
    spt = S // ts
    assert row0 % tm == 0
    i0 = row0 // tm
    if nb == 1:
        omap = lambda i: (i // spt, 0, i % spt, 0)
    else:
        omap = lambda i: (i, 0, 0, 0)
    plain = pl.BlockSpec((nb, H, ts, LANES), omap)
    slab = pl.BlockSpec((None, nb, H, ts, LANES), lambda i: (layer, *omap(i)))
    bf_shape = jax.ShapeDtypeStruct((B, H, S, LANES), BF16)
    slab_shape = jax.ShapeDtypeStruct((n_layers, B, H, S, LANES), F32)
    return dict(
        steps=B * S // tm, body=functools.partial(_qkv_body, nb=nb, ts=ts),
        in_specs=[pl.BlockSpec((tm, K), lambda i: (i + i0, 0)), _resident(w.shape), _resident(b_f.shape)],
        args=[h, w, b_f],
        out_specs=[plain, slab, plain, slab, pl.BlockSpec((nb, H, ts, 2 * LANES), omap),
                   pl.BlockSpec((tm, LANES), lambda i: (i + i0, 0))],
        out_shapes=[bf_shape, slab_shape, bf_shape, slab_shape, jax.ShapeDtypeStruct((B, H, S, 2 * LANES), BF16),
                    jax.ShapeDtypeStruct((T, LANES), F32)],
        dsts=[None, k_dst, None, v_dst, None, lf_dst], scratch=[])


def _mm_res_body(*refs, n):
    lhs = [r[...] for r in refs[:n]]
    w_ref, x_ref, o_ref, wb_ref = refs[n:]

    @pl.when(pl.program_id(0) == 0)
    def _():
        wb_ref[...] = w_ref[...].astype(BF16)

    for cols in _col_tiles(x_ref.shape[1]):
        z, k0 = None, 0
        for a in lhs:
            part = _dot(a, wb_ref[k0:k0 + a.shape[1], cols])
            z = part if z is None else z + part
            k0 += a.shape[1]
        o_ref[:, cols] = x_ref[:, cols] + z


def _mm_res(lhs, w, layer, x):
    T, N = x.shape
    K = w.shape[1]
    assert sum(a.shape[1] for a in lhs) == K
    tm = _tile(T, 512)
    tok = pl.BlockSpec((tm, N), lambda i: (i, 0))
    return _call(
        functools.partial(_mm_res_body, n=len(lhs)), (T // tm,),
        ([pl.BlockSpec((tm, a.shape[1]), lambda i: (i, 0)) for a in lhs]
         + [pl.BlockSpec((None, K, N), lambda i: (layer, 0, 0), pipeline_mode=pl.Buffered(1)), tok]),
        [*lhs, w, x],
        tok, jax.ShapeDtypeStruct((T, N), F32), ("arbitrary",),
        scratch=[pltpu.VMEM((K, N), BF16)])


def _cumsum_body(lf_ref, base_ref, ltok_ref, lrow_ref, lfrow_ref, carry_ref, *, ts, seg, tiles_per_seq, n_heads):
    lf = lf_ref[...]
    r = lax.broadcasted_iota(jnp.int32, (ts, ts), 0)
    c = lax.broadcasted_iota(jnp.int32, (ts, ts), 1)
    tri = jnp.where(r >= c, 1.0, 0.0).astype(F32)
    if seg < ts:
        tri = jnp.where(r // seg == c // seg, tri, 0.0)
    L = jnp.dot(tri, lf, precision=HIGHEST, preferred_element_type=F32)
    if tiles_per_seq:
        @pl.when(pl.program_id(0) % tiles_per_seq == 0)
        def _():
            carry_ref[...] = jnp.zeros_like(carry_ref)
        L = L + carry_ref[...]
        carry_ref[...] = L[ts - 1:ts, :]
    else:
        L = L + base_ref[...]
    ltok_ref[...] = L
    Lt = L.T[:n_heads]
    lft = lf.T[:n_heads]
    for s in range(ts // seg):
        lrow_ref[s] = Lt[:, s * seg:(s + 1) * seg]
        lfrow_ref[s] = lft[:, s * seg:(s + 1) * seg]


def _cumsum(lf_tok, base, row0, B, S, n_heads, layer, n_layers, dst):
    if S >= LANES:
        ts = _tile(S, 512)
        seg, tps, nseg = ts, S // ts, 1
    else:
        ts, seg, tps = LANES, S, 0
        nseg = ts // seg
    assert row0 % ts == 0 and (B * S) % ts == 0
    i0 = row0 // ts
    if tps:
        omap = lambda i: (i // tps, 0, i % tps)
    else:
        omap = lambda i: (i, 0, 0)
    return _call(
        functools.partial(_cumsum_body, ts=ts, seg=seg, tiles_per_seq=tps, n_heads=n_heads),
        (B * S // ts,),
        [pl.BlockSpec((ts, LANES), lambda i: (i + i0, 0)), pl.BlockSpec((ts, LANES), lambda i: (i, 0))],
        [lf_tok, base],
        [pl.BlockSpec((ts, LANES), lambda i: (i, 0)),
         pl.BlockSpec((nseg, n_heads, seg), omap),
         pl.BlockSpec((None, nseg, n_heads, seg), lambda i: (layer, *omap(i)))],
        [jax.ShapeDtypeStruct((B * S, LANES), F32), jax.ShapeDtypeStruct((B, n_heads, S), F32),
         jax.ShapeDtypeStruct((n_layers, B, n_heads, S), F32)],
        ("arbitrary",), dsts=[None, None, dst], scratch=[pltpu.VMEM((1, LANES), F32)])


def _row_cumsum_body(x_ref, o_ref):
    R, P = x_ref.shape
    r = lax.broadcasted_iota(jnp.int32, (LANES, LANES), 0)
    c = lax.broadcasted_iota(jnp.int32, (LANES, LANES), 1)
    tri = jnp.where(r <= c, 1.0, 0.0).astype(F32)
    carry = jnp.zeros((R, 1), F32)
    for s in range(P // LANES):
        y = jnp.dot(x_ref[:, s * LANES:(s + 1) * LANES], tri, precision=HIGHEST, preferred_element_type=F32) + carry
        o_ref[:, s * LANES:(s + 1) * LANES] = y
        carry = y[:, LANES - 1:LANES]


def _row_cumsum(x, row0, n_rows):
    P = x.shape[1]
    tr = _tile(n_rows, 256)
    assert row0 % tr == 0
    i0 = row0 // tr
    return _call(
        _row_cumsum_body, (n_rows // tr,),
        [pl.BlockSpec((tr, P), lambda i: (i + i0, 0))], [x],
        pl.BlockSpec((tr, P), lambda i: (i, 0)),
        jax.ShapeDtypeStruct((n_rows, P), F32), ("parallel",))


def _head_row(rows, h):
    sublane = lax.broadcasted_iota(jnp.int32, rows.shape, 0)
    return jnp.sum(jnp.where(sublane == h, rows, 0.0), axis=0, keepdims=True)


def _fox_prompt_body(q_ref, k_ref, v_ref, l_ref, o_ref, m_ref, acc_ref, s0_ref, s1_ref, *, scale, tq, sub):
    qi = pl.program_id(2)
    c1 = scale * LOG2E
    tk = tq // 2
    n_sub = tq // sub
    every = tuple(range(n_sub))
    m_ref[...] = jnp.full_like(m_ref, MASK_VALUE)
    acc_ref[...] = jnp.zeros_like(acc_ref)
    head = pl.program_id(1)
    l_here = _head_row(l_ref[0, :, pl.ds(pl.multiple_of(qi * tq, tq), LANES)], head)[:, 0:1]

    def scores(ki, s_ref, subs):
        k0 = pl.multiple_of(ki * tk, tk)
        k = k_ref[0, 0, pl.ds(k0, tk), :]
        bias = (l_here - _head_row(l_ref[0, :, pl.ds(k0, tk)], head)) * LOG2E
        for u in subs:
            rows = slice(u * sub, (u + 1) * sub)
            s_ref[rows] = _dot_nt(q_ref[0, 0, rows, :], k) * c1 + bias

    def update(ki, s_ref, subs, diagonal=None):
        k0 = pl.multiple_of(ki * tk, tk)
        v = v_ref[0, 0, pl.ds(k0, tk), :]
        for n in range(0, len(subs), 2):
            update_pair(s_ref, v, subs[n:n + 2], diagonal)

    def update_pair(s_ref, v, subs, diagonal):
        rows = [slice(u * sub, (u + 1) * sub) for u in subs]
        s = []
        for u, rw in zip(subs, rows):
            su = s_ref[rw]
            if diagonal is not None and u * sub < (diagonal + 1) * tk - 1:
                r = lax.broadcasted_iota(jnp.int32, su.shape, 0) + u * sub
                c = lax.broadcasted_iota(jnp.int32, su.shape, 1) + diagonal * tk
                su = jnp.where(c <= r, su, MASK_VALUE)
            s.append(su)
        m_prev = [m_ref[rw] for rw in rows]
        m_new = [jnp.maximum(mp, jnp.max(su, axis=-1, keepdims=True)) for mp, su in zip(m_prev, s)]
        alpha = [jnp.exp2(mp - mn) for mp, mn in zip(m_prev, m_new)]
        p = [jnp.exp2(su - mn) for su, mn in zip(s, m_new)]
        for n, rw in enumerate(rows):
            m_ref[rw] = m_new[n]
        for n, rw in enumerate(rows):
            acc_ref[rw] = alpha[n] * acc_ref[rw] + _dot(p[n].astype(BF16), v)

    def two_blocks(t, carry):
        scores(2 * t + 1, s1_ref, every)
        update(2 * t, s0_ref, every)
        scores(2 * t + 2, s0_ref, every)
        update(2 * t + 1, s1_ref, every)
        return carry

    scores(0, s0_ref, every)
    lax.fori_loop(0, qi, two_blocks, 0)
    late = tuple(u for u in every if (u + 1) * sub > tk)
    scores(2 * qi + 1, s1_ref, late)
    update(2 * qi, s0_ref, every, diagonal=0)
    update(2 * qi + 1, s1_ref, late, diagonal=1)
    D = o_ref.shape[1]
    o_ref[...] = (acc_ref[:, :D] / acc_ref[:, D:D + 1]).astype(o_ref.dtype)


def _fox_prompt(q, k, v, l_row, out_rows, dst):
    B, H, S, D = q.shape
    tq = _tile(S, 1024)
    sub = min(256, tq // 2)
    assert tq % (2 * LANES) == 0 and (tq // 2) % sub == 0
    nq = S // tq
    full = pl.BlockSpec((1, 1, S, D), lambda b, h, qi: (b, h, 0, 0))
    return _call(
        functools.partial(_fox_prompt_body, scale=D ** -0.5, tq=tq, sub=sub),
        (B, H, nq),
        [pl.BlockSpec((1, 1, tq, D), lambda b, h, qi: (b, h, qi, 0)), full,
         pl.BlockSpec((1, 1, S, 2 * D), lambda b, h, qi: (b, h, 0, 0)),
         pl.BlockSpec((1, H, S), lambda b, h, qi: (b, 0, 0))],
        [q, k, v, l_row],
        pl.BlockSpec((tq, D), lambda b, h, qi: (b * nq + qi, h)),
        jax.ShapeDtypeStruct((out_rows, H * D), BF16),
        ("parallel", "parallel", "arbitrary"), dsts=[dst],
        scratch=[pltpu.VMEM((tq, 1), F32), pltpu.VMEM((tq, 2 * D), F32),
                 pltpu.VMEM((tq, tq // 2), F32), pltpu.VMEM((tq, tq // 2), F32)])


def _head_column(l_tok, h):
    lane = lax.broadcasted_iota(jnp.int32, l_tok.shape, 1)
    return jnp.sum(jnp.where(lane == h, l_tok, 0.0), axis=-1, keepdims=True)


def _fox_sample_body(q_ref, *refs, scale, hps, n_split, head_groups):
    ck_refs, cv_refs = refs[:n_split], refs[n_split:2 * n_split]
    kn_ref, vn_ref, lq_ref, lc_ref, ln_ref, o_ref = refs[2 * n_split:]
    hg = pl.program_id(0) % head_groups
    pn = ck_refs[0].shape[-2]
    D = q_ref.shape[-1]
    for hh in range(hps):
        q = q_ref[0, hh]
        lq = _head_column(lq_ref[...], hg * hps + hh)
        s_c = [_dot_nt(q, ck_refs[n][0, hh].astype(BF16)) * scale + (lq - _head_row(lc_ref[0, :, n * pn:(n + 1) * pn], hg * hps + hh))
               for n in range(n_split)]
        s_n = _dot_nt(q, kn_ref[0, hh]) * scale + (lq - _head_row(ln_ref[0], hg * hps + hh))
        r = lax.broadcasted_iota(jnp.int32, s_n.shape, 0)
        c = lax.broadcasted_iota(jnp.int32, s_n.shape, 1)
        s_n = jnp.where(c <= r, s_n, MASK_VALUE)
        m = jnp.max(s_n, axis=-1, keepdims=True)
        for s in s_c:
            m = jnp.maximum(m, jnp.max(s, axis=-1, keepdims=True))
        p_n = jnp.exp(s_n - m)
        l = jnp.sum(p_n, axis=-1, keepdims=True)
        o = _dot(p_n.astype(BF16), vn_ref[0, hh])
        for n in range(n_split):
            p = jnp.exp(s_c[n] - m)
            l = l + jnp.sum(p, axis=-1, keepdims=True)
            o = o + _dot(p.astype(BF16), cv_refs[n][0, hh].astype(BF16))
        o_ref[:, hh * D:(hh + 1) * D] = (o / l).astype(o_ref.dtype)


def _fox_sample_stage(q, cache_k, cache_v, layer, kn, vn, ln_tok, lc_row, ln_row, out_row0, out_rows, dst, hps):
    B, H, T, D = q.shape
    P = cache_k.shape[3]
    assert out_row0 % T == 0 and H % hps == 0
    r0 = out_row0 // T
    ng = H // hps
    n_split = FOX_SAMPLE_KEY_SPLITS if P % (FOX_SAMPLE_KEY_SPLITS * LANES) == 0 else 1
    pn = P // n_split
    new = pl.BlockSpec((1, hps, T, D), lambda s: (s // ng, s % ng, 0, 0))
    old = [pl.BlockSpec((None, 1, hps, pn, D), lambda s, n=n: (layer, s // ng, s % ng, n, 0)) for n in range(n_split)]
    return dict(
        steps=B * ng,
        body=functools.partial(_fox_sample_body, scale=D ** -0.5, hps=hps, n_split=n_split, head_groups=ng),
        in_specs=[new, *old, *old, new, new,
                  pl.BlockSpec((T, LANES), lambda s: (s // ng, 0)),
                  pl.BlockSpec((1, H, P), lambda s: (s // ng, 0, 0)),
                  pl.BlockSpec((1, H, T), lambda s: (s // ng, 0, 0))],
        args=[q, *([cache_k] * n_split), *([cache_v] * n_split), kn, vn, ln_tok, lc_row, ln_row],
        out_specs=[pl.BlockSpec((T, hps * D), lambda s: (s // ng + r0, s % ng))],
        out_shapes=[jax.ShapeDtypeStruct((out_rows, H * D), BF16)],
        dsts=[dst], scratch=[])


def _hgrn_body(*refs, C, hpb, has_init, n_chunks):
    if has_init:
        q_ref, k_ref, g_ref, i_ref, gate_ref, gn_ref, s0_ref, o_ref, sf_ref, st_ref = refs
    else:
        q_ref, k_ref, g_ref, i_ref, gate_ref, gn_ref, o_ref, sf_ref, st_ref = refs
        s0_ref = None
    c = pl.program_id(0) % n_chunks

    @pl.when(c == 0)
    def _():
        for hh in range(hpb):
            st_ref[hh] = s0_ref[0, hh].T if has_init else jnp.zeros((LANES, LANES), F32)

    row = lax.broadcasted_iota(jnp.int32, (C, C), 0)
    col = lax.broadcasted_iota(jnp.int32, (C, C), 1)
    tri = jnp.where(row >= col, 1.0, 0.0).astype(F32)
    trow = lax.broadcasted_iota(jnp.int32, (C, LANES), 0)
    level = 31 - lax.clz(jnp.where(row > col, row ^ col, 0))
    level = jnp.where(row == col, C.bit_length() - 1, level)

    def recur(heads):
        sls = {hh: slice(hh * LANES, (hh + 1) * LANES) for hh in heads}
        q = {hh: q_ref[:, sls[hh]] for hh in heads}
        k = {hh: k_ref[:, sls[hh]] for hh in heads}
        v = {hh: i_ref[:, sls[hh]].astype(BF16) for hh in heads}
        g = {hh: g_ref[:, sls[hh]] * LOG2E for hh in heads}
        G = {hh: jnp.dot(tri, g[hh], precision=HIGHEST, preferred_element_type=F32) for hh in heads}
        st = {hh: st_ref[hh] for hh in heads}
        o = {hh: _dot_nt((q[hh] * jnp.exp2(G[hh])).astype(BF16), st[hh].astype(BF16)) for hh in heads}
        A = {hh: jnp.where(level == C.bit_length() - 1, _dot_nt(q[hh].astype(BF16), k[hh].astype(BF16)), 0.0)
             for hh in heads}
        yq = {hh: G[hh] - g[hh] for hh in heads}
        yk = dict(G)
        w, log_w = 1, 0
        while w < C:
            upper = (trow & w) != 0
            for hh in heads:
                qt = (q[hh] * jnp.exp2(G[hh] - yq[hh])).astype(BF16)
                kt = (k[hh] * jnp.exp2(yk[hh] - G[hh])).astype(BF16)
                A[hh] = jnp.where(level == log_w, _dot_nt(qt, kt), A[hh])
                yq[hh] = jnp.where(upper, pltpu.roll(yq[hh], w, 0), yq[hh])
                yk[hh] = jnp.where(upper, yk[hh], pltpu.roll(yk[hh], C - w, 0))
            w, log_w = 2 * w, log_w + 1
        for hh in heads:
            out = o[hh] + _dot(A[hh].astype(BF16), v[hh])
            g_last = G[hh][C - 1:C, :]
            ks = (k[hh] * jnp.exp2(g_last - G[hh])).astype(BF16)
            st_ref[hh] = st[hh] * jnp.exp2(g_last) + _dot_tn(v[hh], ks)
            o_ref[:, sls[hh]] = (_rms(out, gn_ref[...]) * gate_ref[:, sls[hh]]).astype(o_ref.dtype)

    group = math.gcd(hpb, HGRN_HEADS_PER_GROUP)
    for h0 in range(0, hpb, group):
        recur(range(h0, h0 + group))

    @pl.when(c == n_chunks - 1)
    def _():
        for hh in range(hpb):
            sf_ref[0, hh] = st_ref[hh].T


def _hgrn_stage(main, k, gnorm, s0, row0, B, S, H, layer, n_layers, o_dst, s_dst):
    T = k.shape[0]
    C = min(S, LANES)
    nc = S // C
    assert row0 % C == 0
    r0 = row0 // C
    hpb = math.gcd(H, HGRN_HEADS_PER_STEP)
    nhb = H // hpb
    seq = lambda s: s // (nhb * nc)
    hblk = lambda s: (s // nc) % nhb
    row = lambda s: r0 + seq(s) * nc + s % nc
    tok = pl.BlockSpec((C, hpb * LANES), lambda s: (row(s), hblk(s)))
    seg = lambda n: pl.BlockSpec((C, hpb * LANES), lambda s: (row(s), n * nhb + hblk(s)))
    state = pl.BlockSpec((None, 1, hpb, LANES, LANES), lambda s: (layer, seq(s), hblk(s), 0, 0))
    in_specs = [seg(0), tok, seg(1), seg(2), seg(3), pl.BlockSpec((1, LANES), lambda s: (0, 0))]
    args = [main, k, main, main, main, gnorm.reshape(1, LANES)]
    if s0 is not None:
        in_specs.append(state)
        args.append(s0)
    return dict(
        steps=B * nhb * nc,
        body=functools.partial(_hgrn_body, C=C, hpb=hpb, has_init=s0 is not None, n_chunks=nc),
        in_specs=in_specs, args=args, out_specs=[tok, state],
        out_shapes=[jax.ShapeDtypeStruct((T, H * LANES), BF16),
                    jax.ShapeDtypeStruct((n_layers, B, H, LANES, LANES), F32)],
        dsts=[o_dst, s_dst], scratch=[pltpu.VMEM((hpb, LANES, LANES), F32)])


def _cmix_body(u_ref, v_ref, lng_ref, lnb_ref, ws_ref, bs_ref, y_ref, *vout, groups):
    Lc, Cd = bs_ref.shape
    cg = Cd // groups
    r = lax.broadcasted_iota(jnp.int32, (Lc, Lc), 0)
    c = lax.broadcasted_iota(jnp.int32, (Lc, Lc), 1)
    w = [jnp.where(c <= r, ws_ref[gi], 0.0).astype(BF16) for gi in range(groups)]
    for n in range(v_ref.shape[0] // Lc):
        rows = slice(n * Lc, (n + 1) * Lc)
        v = v_ref[rows]
        mu = jnp.mean(v, axis=-1, keepdims=True)
        xc = v - mu
        var = jnp.mean(xc * xc, axis=-1, keepdims=True)
        vn = xc * lax.rsqrt(var + EPS) * lng_ref[...] + lnb_ref[...]
        if vout:
            vout[0][rows] = vn
        vb = vn.astype(BF16)
        for gi in range(groups):
            sl = slice(gi * cg, (gi + 1) * cg)
            mixed = _dot(w[gi], vb[:, sl]) + bs_ref[:, sl]
            y_ref[rows, sl] = (u_ref[rows, sl] * mixed).astype(y_ref.dtype)


def _cmix(uv, ln_g, ln_b, ws, bs, row0, n_rows, Lc, y_dst, v_out):
    T, Cd = uv.shape[0], uv.shape[1] // 2
    G = ws.shape[0]
    tm = Lc * math.gcd(n_rows // Lc, CMIX_CHUNKS_PER_STEP)
    assert row0 % tm == 0
    r0 = row0 // tm
    tok = pl.BlockSpec((tm, Cd), lambda i: (i + r0, 0))
    tok_v = pl.BlockSpec((tm, Cd), lambda i: (i + r0, 1))
    vec = pl.BlockSpec((1, Cd), lambda i: (0, 0))
    out_specs, out_shapes, dsts = [tok], [jax.ShapeDtypeStruct((T, Cd), BF16)], [y_dst]
    if v_out:
        layer, n_layers, dst = v_out
        out_specs.append(pl.BlockSpec((None, tm, Cd), lambda i: (layer, i, 0)))
        out_shapes.append(jax.ShapeDtypeStruct((n_layers, n_rows, Cd), F32))
        dsts.append(dst)
    return _call(
        functools.partial(_cmix_body, groups=G), (n_rows // tm,),
        [tok, tok_v, vec, vec, pl.BlockSpec((G, Lc, Lc), lambda i: (0, 0, 0)),
         pl.BlockSpec((Lc, Cd), lambda i: (0, 0))],
        [uv, uv, ln_g.reshape(1, Cd), ln_b.reshape(1, Cd), ws, bs],
        out_specs, out_shapes, ("parallel",), dsts=dsts)


def _epi_gelu(z):
    return (jax.nn.gelu(z),)


def _epi_hgrn_forget(z, lb_all, *, layer):
    rows = [lb_all[n:n + 1] for n in range(lb_all.shape[0])]
    top = functools.reduce(jnp.maximum, rows)
    e = [jnp.exp(r - top) for r in rows]
    total = functools.reduce(jnp.add, e)
    sm = [a / total for a in e]
    lb = functools.reduce(jnp.add, sm[:layer + 1]) - sm[0]
    f = lb + (1.0 - lb) * jax.nn.sigmoid(z)
    g = jnp.log(jnp.maximum(f, TINY))
    k = (1.0 - lb) * jax.nn.sigmoid(-z)
    return g, k


def kernel(x_prompt, x_sample, cache_k, cache_v, cache_logf, state_hgrn, norm_ffn1, ffn1_gate, ffn1_up, ffn1_down, norm_mix, ab_w_in, ab_b_f, hgrn_lb, hgrn_gnorm, ab_w_out, c_w_in, c_ln_g, c_ln_b, c_w_s, c_b_s, c_w_out, norm_ffn2, ffn2_gate, ffn2_up, ffn2_down, norm_final):
    B, S, D = x_prompt.shape
    DB, DT, _ = x_sample.shape
    depth = norm_ffn1.shape[0]
    NAB, NC = ab_w_in.shape[0], c_w_in.shape[0]
    HA, P, HD = cache_k.shape[2], cache_k.shape[3], cache_k.shape[4]
    HB, DK, DV = state_hgrn.shape[2], state_hgrn.shape[3], state_hgrn.shape[4]
    assert HD == LANES and DK == LANES and DV == LANES and HA <= 8
    WA, WBK, WBV = HA * HD, HB * DK, HB * DV
    G, CL = c_w_s.shape[1], c_w_s.shape[2]
    CD = c_w_out.shape[1]
    Tp, Ts = B * S, DB * DT
    T = Tp + Ts
    bf = lambda a: a.astype(BF16)

    pk = pv = plf = ps = sk = sv = slf = ss = scv = None
    x = None
    for l in range(depth):
        j = l // 2
        w1 = (norm_ffn1[l], ffn1_gate, ffn1_up, ffn1_down, l, norm_mix[l])
        if l == 0:
            x, h = _ffn(x_prompt.reshape(Tp, D), *w1, out_rows=T)
            x, h = _ffn(x_sample.reshape(Ts, D), *w1, out_rows=T, out_row0=Tp, dsts=(x, h))
        else:
            x, h = _ffn(x, *w1)
        if l % 2 == 0:
            w_in = ab_w_in[j]
            o1 = 3 * WA + HA
            w_qkvf = bf(jnp.pad(w_in[:, :o1], ((0, 0), (0, LANES - HA))))
            w_b = bf(w_in[:, o1:])
            b_f = jnp.pad(ab_b_f[j], (0, LANES - HA)).reshape(1, LANES)
            qs, sk, ksb, sv, vsb, lf_tok = _run_stages(
                [_proj_qkv_stage(h, w_qkvf, b_f, Tp, DB, DT, j, NAB, sk, sv, None)])
            qp, pk, kpb, pv, vpb, lf_tok = _run_stages(
                [_proj_qkv_stage(h, w_qkvf, b_f, 0, B, S, j, NAB, pk, pv, lf_tok)])
            assert WBK == WBV
            zb, kb = _proj_hgrn(h, w_b, hgrn_lb, j)

            _, lp_row, plf = _cumsum(lf_tok, jnp.zeros((Tp, LANES), F32), 0, B, S, HA, j, NAB, plf)
            lc_row = _row_cumsum(cache_logf.reshape(NAB * DB * HA, P), j * DB * HA, DB * HA).reshape(DB, HA, P)
            base = jnp.pad(lc_row[:, :, P - 1], ((0, 0), (0, LANES - HA)))
            base = jnp.broadcast_to(base[:, None, :], (DB, DT, LANES)).reshape(Ts, LANES)
            ls_tok, ls_row, slf = _cumsum(lf_tok, base, Tp, DB, DT, HA, j, NAB, slf)

            rec_p = _hgrn_stage(zb, kb, hgrn_gnorm[j], None, 0, B, S, HB, j, NAB, None, ps)
            fox_s = functools.partial(_fox_sample_stage, qs, cache_k, cache_v, j, ksb, vsb, ls_tok,
                                      lc_row, ls_row, Tp, T, None)
            hps = DB * HA // rec_p["steps"]
            if 1 <= hps <= FOX_SAMPLE_MAX_HEADS_PER_STEP and HA % hps == 0 and DB * (HA // hps) == rec_p["steps"]:
                ob, ps, oa = _run_stages([rec_p, fox_s(hps)])
            else:
                ob, ps = _run_stages([rec_p])
                (oa,) = _run_stages([fox_s(math.gcd(HA, FOX_SAMPLE_HEADS_PER_STEP))])
            oa = _fox_prompt(qp, kpb, vpb, lp_row, T, oa)
            ob, ss = _run_stages([_hgrn_stage(zb, kb, hgrn_gnorm[j], state_hgrn, Tp, DB, DT, HB, j, NAB, ob, ss)])

            x = _mm_res([oa, ob], ab_w_out, j, x)
        else:
            (uv,) = _proj(h, bf(c_w_in[j]), [], _epi_gelu, [F32], 512)
            y = None
            for row0, n_rows, seq in ((0, Tp, S), (Tp, Ts, DT)):
                Lc = min(seq, CL)
                ws = c_w_s[j][:, :Lc, :Lc]
                bs = jnp.repeat(c_b_s[j][:, :Lc].T, CD // G, axis=1)
                if row0 == 0:
                    (y,) = _cmix(uv, c_ln_g[j], c_ln_b[j], ws, bs, row0, n_rows, Lc, y, None)
                else:
                    y, scv = _cmix(uv, c_ln_g[j], c_ln_b[j], ws, bs, row0, n_rows, Lc, y, (j, NC, scv))
            x = _mm_res([y], c_w_out, j, x)
        w2 = (norm_ffn2[l], ffn2_gate, ffn2_up, ffn2_down, l)
        if l + 1 < depth:
            x = _ffn(x, *w2)
        else:
            y_p = _ffn(x, *w2, g_final=norm_final, in_row0=0, n_rows=Tp)
            y_s = _ffn(x, *w2, g_final=norm_final, in_row0=Tp, n_rows=Ts)
    return (y_p.reshape(B, S, D), y_s.reshape(DB, DT, D), pk, pv, plf, ps,
            sk, sv, slf, ss, scv.reshape(NC, DB, DT, CD))
```
